```python
import math
import jax, jax.numpy as jnp
from jax import lax
import numpy as np


D_MODEL = 2048
BATCH = 4
SEQ = 2048
DEPTH = 2
DEC_BATCH = 8
DEC_SEQ = 1
PAST_LEN = 16384
PAGE_SIZE = 128

H_A = 8
DQK_A = 64
DV_A = 2 * DQK_A
H_B = 8
KVH_B = 2
DH_B = 128
H_IDX = 16
D_IDX = 64
TOPK_MAX = 256
ROPE_THETA = 500000.0
ROPE_FRAC = 4
DENSE_FF = 5632
N_EXPERTS = 8
TOP_K_EXPERTS = 2
EXPERT_FF = 7168
RMS_EPS = 1e-6
Q_BLOCK = 128
IN_WIDTHS = (H_A * 2 * DQK_A, H_A * 2 * DQK_A, H_A * DV_A,
             H_B * DH_B, KVH_B * DH_B, KVH_B * DH_B,
             H_IDX * D_IDX, D_IDX, H_IDX,
             D_MODEL, D_MODEL)

kernel_name = 'hybrid_diffattn_dsa_decoder_step'


def _split_points():
    pts, acc = [], 0
    for w in IN_WIDTHS[:-1]:
        acc += w
        pts.append(acc)
    return pts


def rmsnorm(x, g):
    xf = x.astype(jnp.float32)
    y = xf * lax.rsqrt(jnp.mean(xf * xf, axis=-1, keepdims=True) + RMS_EPS) * g.astype(jnp.float32)
    return y.astype(x.dtype)


def rope(x, pos):
    d = x.shape[-1]
    rot = d // ROPE_FRAC
    half = rot // 2
    inv_freq = jnp.power(ROPE_THETA, -jnp.arange(half, dtype=jnp.float32) * 2.0 / rot)
    ang = (pos[:, None] * inv_freq[None, :]).reshape((pos.shape[0],) + (1,) * (x.ndim - 3) + (half,))
    cos, sin = jnp.cos(ang), jnp.sin(ang)
    xf = x.astype(jnp.float32)
    x1, x2 = xf[..., :half], xf[..., half:rot]
    out = jnp.concatenate([x1 * cos - x2 * sin, x2 * cos + x1 * sin, xf[..., rot:]], axis=-1)
    return out.astype(x.dtype)


def project(xn, w_in, pos):
    B, T = xn.shape[:2]
    z = xn @ w_in
    qa, ka, va, qb, kb, vb, qi, ki, wi, ga, gb = jnp.split(z, _split_points(), axis=-1)
    qa = rope(qa.reshape(B, T, H_A, 2, DQK_A), pos)
    ka = rope(ka.reshape(B, T, H_A, 2, DQK_A), pos)
    va = va.reshape(B, T, H_A, DV_A)
    qb = rope(qb.reshape(B, T, H_B, DH_B), pos)
    kb = rope(kb.reshape(B, T, KVH_B, DH_B), pos)
    vb = vb.reshape(B, T, KVH_B, DH_B)
    qi = rope(qi.reshape(B, T, H_IDX, D_IDX), pos)
    ki = rope(ki, pos)
    return (qa, ka, va, qb, kb, vb, qi, ki, wi, ga, gb)


def diff_attend(q, k, v, mask, lam):
    s = jnp.einsum('bqhmd,bkhmd->bhmqk', q, k).astype(jnp.float32) * (DQK_A ** -0.5)
    s = jnp.where(mask[:, None, None], s, -jnp.inf)
    p = jax.nn.softmax(s, axis=-1)
    p = p[:, :, 0] - lam * p[:, :, 1]
    return jnp.einsum('bhqk,bkhd->bqhd', p.astype(v.dtype), v)


def indexer_scores(qi, ki, wi, mask):
    s = jax.nn.relu(jnp.einsum('bqhd,bkd->bqhk', qi, ki).astype(jnp.float32))
    w = wi.astype(jnp.float32) * (H_IDX ** -0.5 * D_IDX ** -0.5)
    score = jnp.einsum('bqhk,bqh->bqk', s, w)
    return jnp.where(mask, score, -jnp.inf)


def dsa_attend(q, k_sel, v_sel, valid):
    B, T = q.shape[:2]
    q = q.reshape(B, T, KVH_B, H_B // KVH_B, DH_B)
    s = jnp.einsum('bqgrd,bqkgd->bqgrk', q, k_sel).astype(jnp.float32) * (DH_B ** -0.5)
    s = jnp.where(valid[:, :, None, None, :], s, -jnp.inf)
    p = jax.nn.softmax(s, axis=-1)
    o = jnp.einsum('bqgrk,bqkgd->bqgrd', p.astype(v_sel.dtype), v_sel)
    return o.reshape(B, T, H_B * DH_B)


def prompt_mixers(p, lam, topk):
    qa, ka, va, qb, kb, vb, qi, ki, wi = p[:9]
    B, S = qa.shape[:2]
    nb = S // Q_BLOCK
    kpos = jnp.arange(S)
    b_idx = jnp.arange(B)[:, None, None]

    def to_blocks(t):
        return jnp.moveaxis(t.reshape((B, nb, Q_BLOCK) + t.shape[2:]), 1, 0)

    def block(args):
        i, qa_b, qb_b, qi_b, wi_b = args
        qpos = i * Q_BLOCK + jnp.arange(Q_BLOCK)
        mask = (kpos[None, :] <= qpos[:, None])[None]
        oa = diff_attend(qa_b, ka, va, mask, lam)
        score = indexer_scores(qi_b, ki, wi_b, mask)
        _, sel = lax.top_k(score, topk)
        valid = sel <= qpos[None, :, None]
        ob = dsa_attend(qb_b, kb[b_idx, sel], vb[b_idx, sel], valid)
        return oa, ob

    oa, ob = lax.map(block, (jnp.arange(nb), to_blocks(qa), to_blocks(qb), to_blocks(qi), to_blocks(wi)))
    oa = jnp.moveaxis(oa, 0, 1).reshape((B, S) + oa.shape[3:])
    ob = jnp.moveaxis(ob, 0, 1).reshape((B, S) + ob.shape[3:])
    return oa, ob


def gather_paged(pool, page_table):
    g = pool[page_table]
    return g.reshape((page_table.shape[0], page_table.shape[1] * PAGE_SIZE) + pool.shape[2:])


def gather_paged_rows(pool, new_rows, page_table, idx, past_len):
    n_new = new_rows.shape[1]
    b = jnp.arange(page_table.shape[0])[:, None, None]
    pidx = jnp.minimum(idx, past_len - 1)
    phys = page_table[b, pidx // PAGE_SIZE]
    past = pool[phys, pidx % PAGE_SIZE]
    new = new_rows[b, jnp.clip(idx - past_len, 0, n_new - 1)]
    is_new = (idx >= past_len).reshape(idx.shape + (1,) * (past.ndim - 3))
    return jnp.where(is_new, new, past)


def sample_mixers(p, lam, topk, pool_ak, pool_av, pool_bk, pool_bv, pool_ik, page_table):
    qa, ka, va, qb, kb, vb, qi, ki, wi = p[:9]
    DB, T = qa.shape[:2]
    past_len = page_table.shape[1] * PAGE_SIZE
    L = past_len + T
    ka_all = jnp.concatenate([gather_paged(pool_ak, page_table).reshape(DB, past_len, H_A, 2, DQK_A), ka], axis=1)
    va_all = jnp.concatenate([gather_paged(pool_av, page_table), va], axis=1)
    ki_all = jnp.concatenate([gather_paged(pool_ik, page_table), ki], axis=1)
    qpos = past_len + jnp.arange(T)
    mask = (jnp.arange(L)[None, :] <= qpos[:, None])[None]
    oa = diff_attend(qa, ka_all, va_all, mask, lam)
    score = indexer_scores(qi, ki_all, wi, mask)
    _, sel = lax.top_k(score, topk)
    valid = sel <= qpos[None, :, None]
    kb_sel = gather_paged_rows(pool_bk, kb, page_table, sel, past_len)
    vb_sel = gather_paged_rows(pool_bv, vb, page_table, sel, past_len)
    ob = dsa_attend(qb, kb_sel, vb_sel, valid)
    return oa, ob


def merge_branches(oa, ob, ga, gb, subln_g, lam_init, w_pa, w_pb, w_o):
    B, T = oa.shape[:2]
    oa = rmsnorm(oa, subln_g) * (1.0 - lam_init)
    ya = oa.reshape(B, T, H_A * DV_A) @ w_pa
    yb = ob @ w_pb
    return (jax.nn.sigmoid(ga) * ya + jax.nn.sigmoid(gb) * yb) @ w_o


def swiglu(h, wg, wu, wd):
    return (jax.nn.silu(h @ wg) * (h @ wu)) @ wd


def moe_ffn(h, w_router, wg, wu, wd):
    logits = (h @ w_router).astype(jnp.float32)
    top_vals, top_idx = lax.top_k(logits, TOP_K_EXPERTS)
    gates = jax.nn.softmax(top_vals, axis=-1)
    combine = jnp.sum(jax.nn.one_hot(top_idx, N_EXPERTS, dtype=jnp.float32) * gates[..., None], axis=-2)
    y = jnp.zeros_like(h)
    for e in range(N_EXPERTS):
        y = y + combine[..., e:e + 1].astype(h.dtype) * swiglu(h, wg[e], wu[e], wd[e])
    return y


def setup_inputs(seed: int = 0) -> dict:
    key = jax.random.key(seed)
    ks = jax.random.split(key, 32)
    n_pages = PAST_LEN // PAGE_SIZE
    used = DEC_BATCH * n_pages
    n_pool = used + max(1, used // 4)
    n_dense = (DEPTH + 1) // 2
    n_moe = DEPTH // 2
    n_in = sum(IN_WIDTHS)

    def nrm(k, shape, scale=1.0):
        return jax.random.normal(k, shape, jnp.float32) * scale

    page_table = jax.random.permutation(ks[7], n_pool)[:used].reshape(DEC_BATCH, n_pages).astype(jnp.int32)
    return {
        'x_prompt': nrm(ks[0], (BATCH, SEQ, D_MODEL)),
        'x_sample': nrm(ks[1], (DEC_BATCH, DEC_SEQ, D_MODEL)),
        'cache_a_k': nrm(ks[2], (DEPTH, n_pool, PAGE_SIZE, H_A, 2 * DQK_A)),
        'cache_a_v': nrm(ks[3], (DEPTH, n_pool, PAGE_SIZE, H_A, DV_A)),
        'cache_b_k': nrm(ks[4], (DEPTH, n_pool, PAGE_SIZE, KVH_B, DH_B)),
        'cache_b_v': nrm(ks[5], (DEPTH, n_pool, PAGE_SIZE, KVH_B, DH_B)),
        'cache_idx_k': nrm(ks[6], (DEPTH, n_pool, PAGE_SIZE, D_IDX)),
        'page_table': page_table,
        'w_in': nrm(ks[8], (DEPTH, D_MODEL, n_in), D_MODEL ** -0.5),
        'lambda_q': nrm(ks[9], (DEPTH, 2, DQK_A), 0.1),
        'lambda_k': nrm(ks[10], (DEPTH, 2, DQK_A), 0.1),
        'subln_g': 1.0 + nrm(ks[11], (DEPTH, DV_A), 0.01),
        'w_branch_a': nrm(ks[12], (DEPTH, H_A * DV_A, D_MODEL), (H_A * DV_A) ** -0.5),
        'w_branch_b': nrm(ks[13], (DEPTH, H_B * DH_B, D_MODEL), (H_B * DH_B) ** -0.5),
        'w_out': nrm(ks[14], (DEPTH, D_MODEL, D_MODEL), D_MODEL ** -0.5),
        'norm_mix_g': 1.0 + nrm(ks[15], (DEPTH, D_MODEL), 0.01),
        'norm_ffn_g': 1.0 + nrm(ks[16], (DEPTH, D_MODEL), 0.01),
        'w_dense_gate': nrm(ks[17], (n_dense, D_MODEL, DENSE_FF), D_MODEL ** -0.5),
        'w_dense_up': nrm(ks[18], (n_dense, D_MODEL, DENSE_FF), D_MODEL ** -0.5),
        'w_dense_down': nrm(ks[19], (n_dense, DENSE_FF, D_MODEL), DENSE_FF ** -0.5),
        'w_router': nrm(ks[20], (n_moe, D_MODEL, N_EXPERTS), D_MODEL ** -0.5),
        'w_exp_gate': nrm(ks[21], (n_moe, N_EXPERTS, D_MODEL, EXPERT_FF), D_MODEL ** -0.5),
        'w_exp_up': nrm(ks[22], (n_moe, N_EXPERTS, D_MODEL, EXPERT_FF), D_MODEL ** -0.5),
        'w_exp_down': nrm(ks[23], (n_moe, N_EXPERTS, EXPERT_FF, D_MODEL), EXPERT_FF ** -0.5),
        'norm_final_g': 1.0 + nrm(ks[24], (D_MODEL,), 0.01),
    }


def reference(x_prompt, x_sample, cache_a_k, cache_a_v, cache_b_k, cache_b_v, cache_idx_k, page_table,
              w_in, lambda_q, lambda_k, subln_g, w_branch_a, w_branch_b, w_out, norm_mix_g, norm_ffn_g,
              w_dense_gate, w_dense_up, w_dense_down, w_router, w_exp_gate, w_exp_up, w_exp_down, norm_final_g):
    B, S = x_prompt.shape[:2]
    DB, T = x_sample.shape[:2]
    past_len = page_table.shape[1] * PAGE_SIZE
    topk_prompt = min(TOPK_MAX, S // 4)
    topk_sample = min(TOPK_MAX, (past_len + T) // 4)
    pos_p = jnp.arange(S, dtype=jnp.float32)
    pos_s = past_len + jnp.arange(T, dtype=jnp.float32)
    xp, xs = x_prompt, x_sample
    rows_p = ([], [], [], [], [])
    rows_s = ([], [], [], [], [])
    for l in range(DEPTH):
        lam_init = 0.8 - 0.6 * math.exp(-0.3 * l)
        lq = lambda_q[l].astype(jnp.float32)
        lk = lambda_k[l].astype(jnp.float32)
        lam = jnp.exp(jnp.sum(lq[0] * lk[0])) - jnp.exp(jnp.sum(lq[1] * lk[1])) + lam_init

        pp = project(rmsnorm(xp, norm_mix_g[l]), w_in[l], pos_p)
        oa, ob = prompt_mixers(pp, lam, topk_prompt)
        xp = xp + merge_branches(oa, ob, pp[9], pp[10], subln_g[l], lam_init, w_branch_a[l], w_branch_b[l], w_out[l])

        ps = project(rmsnorm(xs, norm_mix_g[l]), w_in[l], pos_s)
        oa, ob = sample_mixers(ps, lam, topk_sample, cache_a_k[l], cache_a_v[l], cache_b_k[l], cache_b_v[l],
                               cache_idx_k[l], page_table)
        xs = xs + merge_branches(oa, ob, ps[9], ps[10], subln_g[l], lam_init, w_branch_a[l], w_branch_b[l], w_out[l])

        for dst, src, bt, tt in ((rows_p, pp, B, S), (rows_s, ps, DB, T)):
            dst[0].append(src[1].reshape(bt, tt, H_A, 2 * DQK_A))
            dst[1].append(src[2])
            dst[2].append(src[4])
            dst[3].append(src[5])
            dst[4].append(src[7])

        i = l // 2
        hp = rmsnorm(xp, norm_ffn_g[l])
        hs = rmsnorm(xs, norm_ffn_g[l])
        if l % 2 == 0:
            xp = xp + swiglu(hp, w_dense_gate[i], w_dense_up[i], w_dense_down[i])
            xs = xs + swiglu(hs, w_dense_gate[i], w_dense_up[i], w_dense_down[i])
        else:
            xp = xp + moe_ffn(hp, w_router[i], w_exp_gate[i], w_exp_up[i], w_exp_down[i])
            xs = xs + moe_ffn(hs, w_router[i], w_exp_gate[i], w_exp_up[i], w_exp_down[i])

    y_prompt = rmsnorm(xp, norm_final_g)
    y_sample = rmsnorm(xs, norm_final_g)
    a_k_p, a_v_p, b_k_p, b_v_p, i_k_p = [jnp.stack(r, axis=0) for r in rows_p]
    a_k_s, a_v_s, b_k_s, b_v_s, i_k_s = [jnp.stack(r, axis=0) for r in rows_s]
    return (y_prompt, y_sample, a_k_p, a_v_p, b_k_p, b_v_p, i_k_p, a_k_s, a_v_s, b_k_s, b_v_s, i_k_s)
```

```python
import functools
import math

import jax
import jax.numpy as jnp
from jax import lax
from jax.experimental import pallas as pl
from jax.experimental.pallas import tpu as pltpu

LANES = 128
VMEM_LIMIT_BYTES = 56 * 1024 * 1024
ROPE_THETA = 500000.0
ROPE_FRAC = 4
RMS_EPS = 1e-6
TOPK_MAX = 256
PAGE_SIZE = 128
TOP_K_EXPERTS = 2
INT_MIN = -2 ** 31
F32 = jnp.float32
BF16 = jnp.bfloat16


def _pick(n, pref, mult):
    if n <= pref:
        return n
    t = (pref // mult) * mult
    while t >= mult:
        if n % t == 0:
            return t
        t -= mult
    return n


def _params(*sem):
    return pltpu.CompilerParams(dimension_semantics=sem, vmem_limit_bytes=VMEM_LIMIT_BYTES)


def _dot(a, b):
    return jnp.dot(a, b, preferred_element_type=F32)


def _dot_nt(a, b):
    return lax.dot_general(a, b, (((1,), (1,)), ((), ())), preferred_element_type=F32)


def _rms(x, g):
    return x * lax.rsqrt(jnp.mean(x * x, axis=-1, keepdims=True) + RMS_EPS) * g


def _rmsnorm_kernel(x_ref, g_ref, o_ref):
    o_ref[...] = _rms(x_ref[...], g_ref[...]).astype(o_ref.dtype)


def rmsnorm(x, g, out_dtype):
    m, d = x.shape
    tm = _pick(m, 512, 8)
    return pl.pallas_call(
        _rmsnorm_kernel,
        grid=(m // tm,),
        in_specs=[pl.BlockSpec((tm, d), lambda i: (i, 0)), pl.BlockSpec((1, d), lambda i: (0, 0))],
        out_specs=pl.BlockSpec((tm, d), lambda i: (i, 0)),
        out_shape=jax.ShapeDtypeStruct((m, d), out_dtype),
        compiler_params=_params("parallel"),
        name="rmsnorm",
    )(x, g.reshape(1, d))


def _proj_kernel(*refs, half, tn):
    if half:
        x_ref, w_ref, tab_ref, *o_refs = refs
    else:
        x_ref, w_ref, *o_refs = refs
    z = _dot(x_ref[...], w_ref[...])
    if not half:
        for o in o_refs:
            o[...] = z.astype(o.dtype)
        return
    c, s_lo, s_hi = tab_ref[0], tab_ref[1], tab_ref[2]
    for g in range(tn // LANES):
        zg = z[:, g * LANES:(g + 1) * LANES]
        r = zg * c + pltpu.roll(zg, LANES - half, 1) * s_lo + pltpu.roll(zg, half, 1) * s_hi
        for o in o_refs:
            o[:, g * LANES:(g + 1) * LANES] = r.astype(o.dtype)


def project(xn, w, out_dtypes, tab=None, half=0, tn_pref=512):
    m, k = xn.shape
    n = w.shape[1]
    p_rows = m if tab is None else tab.shape[1]
    assert m % p_rows == 0
    tm = _pick(p_rows, 1024, 8)
    tn = _pick(n, tn_pref, LANES)
    in_specs = [pl.BlockSpec((tm, k), lambda i, j: (i, 0)), pl.BlockSpec((k, tn), lambda i, j: (0, j))]
    args = [xn, w]
    if tab is not None:
        nblk = p_rows // tm
        in_specs.append(pl.BlockSpec((3, tm, LANES), lambda i, j: (0, i % nblk, 0)))
        args.append(tab)
    outs = pl.pallas_call(
        functools.partial(_proj_kernel, half=half if tab is not None else 0, tn=tn),
        grid=(m // tm, n // tn),
        in_specs=in_specs,
        out_specs=[pl.BlockSpec((tm, tn), lambda i, j: (i, j)) for _ in out_dtypes],
        out_shape=[jax.ShapeDtypeStruct((m, n), dt) for dt in out_dtypes],
        compiler_params=_params("parallel", "arbitrary"),
        name="in_proj",
    )(*args)
    return outs


def rope_table(pos, head_dim, valid_lanes=LANES):
    rot = head_dim // ROPE_FRAC
    half = rot // 2
    inv_freq = jnp.power(ROPE_THETA, -jnp.arange(half, dtype=F32) * 2.0 / rot)
    ang = pos[:, None] * inv_freq[None, :]
    cos, sin = jnp.cos(ang), jnp.sin(ang)
    n = pos.shape[0]
    ones = jnp.ones((n, head_dim - rot), F32)
    zeros_h = jnp.zeros((n, half), F32)
    zeros_r = jnp.zeros((n, head_dim - rot), F32)
    c = jnp.concatenate([cos, cos, ones], axis=1)
    s_lo = jnp.concatenate([-sin, zeros_h, zeros_r], axis=1)
    s_hi = jnp.concatenate([zeros_h, sin, zeros_r], axis=1)
    reps = LANES // head_dim
    tabs = [jnp.tile(t, (1, reps)) for t in (c, s_lo, s_hi)]
    if valid_lanes < LANES:
        lane = jnp.arange(LANES)[None, :]
        tabs = [jnp.where(lane < valid_lanes, tabs[0], 1.0), jnp.where(lane < valid_lanes, tabs[1], 0.0),
                jnp.where(lane < valid_lanes, tabs[2], 0.0)]
    return jnp.stack(tabs, axis=0), half


def _lambda(lq_ref, lk_ref, lam_init):
    lq = lq_ref[...].astype(F32)
    lk = lk_ref[...].astype(F32)
    prod = lq * lk
    return (jnp.exp(jnp.sum(prod[0:1, :], axis=1, keepdims=True))
            - jnp.exp(jnp.sum(prod[1:2, :], axis=1, keepdims=True)) + lam_init)


def _diffattn_kernel(lq_ref, lk_ref, g_ref, q_ref, k_ref, v_ref, o_ref, m_ref, l_ref, acc_ref,
                     *, tq, dqk, lam_init):
    i = pl.program_id(2)
    scale = dqk ** -0.5
    q = q_ref[...].astype(F32)
    lane = lax.broadcasted_iota(jnp.int32, q.shape, 1)
    qm = (jnp.where(lane < dqk, q, 0.0).astype(q_ref.dtype), jnp.where(lane >= dqk, q, 0.0).astype(q_ref.dtype))
    m_ref[...] = jnp.full(m_ref.shape, -jnp.inf, F32)
    l_ref[...] = jnp.zeros(l_ref.shape, F32)
    acc_ref[...] = jnp.zeros(acc_ref.shape, F32)

    def step(j, masked):
        start = pl.multiple_of(j * tq, tq)
        kt = k_ref[pl.ds(start, tq), :]
        vt = v_ref[pl.ds(start, tq), :]
        for m in range(2):
            s = _dot_nt(qm[m], kt) * scale
            if masked:
                r = lax.broadcasted_iota(jnp.int32, s.shape, 0)
                c = lax.broadcasted_iota(jnp.int32, s.shape, 1)
                s = jnp.where(c <= r, s, -jnp.inf)
            m_old = m_ref[m]
            m_new = jnp.maximum(m_old, jnp.max(s, axis=1, keepdims=True))
            alpha = jnp.exp(m_old - m_new)
            p = jnp.exp(s - m_new)
            l_ref[m] = alpha * l_ref[m] + jnp.sum(p, axis=1, keepdims=True)
            acc_ref[m] = alpha * acc_ref[m] + _dot(p.astype(vt.dtype), vt)
            m_ref[m] = m_new

    def body(j, carry):
        step(j, False)
        return carry

    lax.fori_loop(0, i, body, 0)
    step(i, True)

    lam = _lambda(lq_ref, lk_ref, lam_init)
    o = acc_ref[0] / l_ref[0] - lam * (acc_ref[1] / l_ref[1])
    o_ref[...] = (_rms(o, g_ref[...]) * (1.0 - lam_init)).astype(o_ref.dtype)


def diff_attention_prompt(q, k, v, lq, lk, g, *, batch, seq, n_heads, dqk, lam_init):
    m, width = q.shape
    hd = width // n_heads
    tq = _pick(seq, 256, LANES)
    nq = seq // tq
    return pl.pallas_call(
        functools.partial(_diffattn_kernel, tq=tq, dqk=dqk, lam_init=lam_init),
        grid=(batch, n_heads, nq),
        in_specs=[
            pl.BlockSpec(lq.shape, lambda b, h, i: (0, 0)),
            pl.BlockSpec(lk.shape, lambda b, h, i: (0, 0)),
            pl.BlockSpec((1, hd), lambda b, h, i: (0, 0)),
            pl.BlockSpec((tq, hd), lambda b, h, i: (b * nq + i, h)),
            pl.BlockSpec((seq, hd), lambda b, h, i: (b, h)),
            pl.BlockSpec((seq, hd), lambda b, h, i: (b, h)),
        ],
        out_specs=pl.BlockSpec((tq, hd), lambda b, h, i: (b * nq + i, h)),
        out_shape=jax.ShapeDtypeStruct((m, width), BF16),
        scratch_shapes=[pltpu.VMEM((2, tq, 1), F32), pltpu.VMEM((2, tq, 1), F32), pltpu.VMEM((2, tq, hd), F32)],
        compiler_params=_params("parallel", "parallel", "arbitrary"),
        name="diff_attn_prompt",
    )(lq, lk, g.reshape(1, hd), q, k, v)


def _order_key(score):
    score = jnp.where(score == 0.0, 0.0, score)
    bits = lax.bitcast_convert_type(score, jnp.int32)
    return jnp.where(bits < 0, bits ^ jnp.int32(0x7FFFFFFF), bits)


def _count(mask):
    return jnp.sum(mask.astype(jnp.int32), axis=1, keepdims=True)


def _kth_largest_key(load_keys, rows, k, count_fn):
    t0 = jnp.full((rows, 1), INT_MIN, jnp.int32)
    zero = jnp.zeros((rows, 1), jnp.int32)
    t0 = jnp.where(count_fn(load_keys() >= zero) >= k, zero, t0)

    def body(it, t):
        cand = t | (jnp.int32(1) << (30 - it))
        return jnp.where(count_fn(load_keys() >= cand) >= k, cand, t)

    return lax.fori_loop(0, 31, body, t0)


def _tie_cut(load_eq_pos, rows, need, n_bits, count_fn):
    def body(it, x):
        cand = x | (jnp.int32(1) << (n_bits - 1 - it))
        eq, pos = load_eq_pos()
        return jnp.where(count_fn(eq & (pos < cand)) < need, cand, x)

    return lax.fori_loop(0, n_bits, body, jnp.zeros((rows, 1), jnp.int32))


def _dsa_prompt_kernel(qi_ref, kiwi_k_ref, kiwi_q_ref, qb_ref, kb_ref, vb_ref, o_ref,
                       kdup_ref, key_ref, bias_ref, cut_ref,
                       *, tq, seq, n_idx, d_idx, n_heads, n_kv, dh, topk):
    i = pl.program_id(1)

    @pl.when(i == 0)
    def _():
        kf = kiwi_k_ref[...]
        lane = lax.broadcasted_iota(jnp.int32, kf.shape, 1)
        klo = jnp.where(lane < d_idx, kf, 0.0)
        kdup_ref[...] = (klo + pltpu.roll(klo, d_idx, 1)).astype(kdup_ref.dtype)

    kd = kdup_ref[...]
    w_all = kiwi_q_ref[...]
    wscale = n_idx ** -0.5 * d_idx ** -0.5
    score = jnp.zeros((tq, seq), F32)
    for h in range(n_idx):
        grp = qi_ref[:, (h // 2) * LANES:(h // 2 + 1) * LANES].astype(F32)
        lane = lax.broadcasted_iota(jnp.int32, grp.shape, 1)
        keep = (lane < d_idx) if h % 2 == 0 else (lane >= d_idx)
        qh = jnp.where(keep, grp, 0.0).astype(qi_ref.dtype)
        w = w_all[:, d_idx + h:d_idx + h + 1] * wscale
        score = score + jnp.maximum(_dot_nt(qh, kd), 0.0) * w

    row = i * tq + lax.broadcasted_iota(jnp.int32, (tq, seq), 0)
    col = lax.broadcasted_iota(jnp.int32, (tq, seq), 1)
    causal = col <= row
    key_ref[...] = jnp.where(causal, _order_key(score), INT_MIN)

    t = _kth_largest_key(lambda: key_ref[...], tq, topk, _count)
    key = key_ref[...]
    eq = key == t
    need = topk - _count(key > t)
    tie = (_count(eq) > need) & (t > INT_MIN)
    cut_ref[...] = jnp.full((tq, 1), seq, jnp.int32)

    @pl.when(jnp.max(tie.astype(jnp.int32)) > 0)
    def _():
        def load():
            return key_ref[...] == t, lax.broadcasted_iota(jnp.int32, (tq, seq), 1)
        cut_ref[...] = _tie_cut(load, tq, need, int(math.log2(seq)), _count)

    sel = ((key > t) | (eq & (col <= cut_ref[...]))) & causal
    bias_ref[...] = jnp.where(sel, 0.0, -jnp.inf)

    scale = dh ** -0.5
    rep = n_heads // n_kv
    for h in range(n_heads):
        g = h // rep
        s = _dot_nt(qb_ref[:, h * dh:(h + 1) * dh], kb_ref[:, g * dh:(g + 1) * dh]) * scale + bias_ref[...]
        p = jnp.exp(s - jnp.max(s, axis=1, keepdims=True))
        l = jnp.sum(p, axis=1, keepdims=True)
        o = _dot(p.astype(vb_ref.dtype), vb_ref[:, g * dh:(g + 1) * dh]) / l
        o_ref[:, h * dh:(h + 1) * dh] = o.astype(o_ref.dtype)


def dsa_prompt(qi, kiwi, qb, kb, vb, *, batch, seq, n_idx, d_idx, n_heads, n_kv, dh, topk):
    m = qi.shape[0]
    assert 2 * d_idx == LANES and seq & (seq - 1) == 0
    tq = _pick(seq, 128, LANES)
    nq = seq // tq
    return pl.pallas_call(
        functools.partial(_dsa_prompt_kernel, tq=tq, seq=seq, n_idx=n_idx, d_idx=d_idx, n_heads=n_heads,
                          n_kv=n_kv, dh=dh, topk=topk),
        grid=(batch, nq),
        in_specs=[
            pl.BlockSpec((tq, qi.shape[1]), lambda b, i: (b * nq + i, 0)),
            pl.BlockSpec((seq, LANES), lambda b, i: (b, 0)),
            pl.BlockSpec((tq, LANES), lambda b, i: (b * nq + i, 0)),
            pl.BlockSpec((tq, qb.shape[1]), lambda b, i: (b * nq + i, 0)),
            pl.BlockSpec((seq, kb.shape[1]), lambda b, i: (b, 0)),
            pl.BlockSpec((seq, vb.shape[1]), lambda b, i: (b, 0)),
        ],
        out_specs=pl.BlockSpec((tq, qb.shape[1]), lambda b, i: (b * nq + i, 0)),
        out_shape=jax.ShapeDtypeStruct((m, qb.shape[1]), BF16),
        scratch_shapes=[pltpu.VMEM((seq, LANES), BF16), pltpu.VMEM((tq, seq), jnp.int32),
                        pltpu.VMEM((tq, seq), F32), pltpu.VMEM((tq, 1), jnp.int32)],
        compiler_params=_params("parallel", "arbitrary"),
        name="dsa_prompt",
    )(qi, kiwi, kiwi, qb, kb, vb)


def _merge_kernel(oa_ref, ob_ref, wa_ref, wb_ref, ga_ref, gb_ref, o_ref):
    ya = _dot(oa_ref[...], wa_ref[...])
    yb = _dot(ob_ref[...], wb_ref[...])
    u = jax.nn.sigmoid(ga_ref[...]) * ya + jax.nn.sigmoid(gb_ref[...]) * yb
    o_ref[...] = u.astype(o_ref.dtype)


def merge_branches(oa, ob, w_pa, w_pb, ga, gb):
    m, ka = oa.shape
    kb = ob.shape[1]
    n = w_pa.shape[1]
    tm = _pick(m, 1024, 8)
    tn = _pick(n, 512, LANES)
    return pl.pallas_call(
        _merge_kernel,
        grid=(m // tm, n // tn),
        in_specs=[
            pl.BlockSpec((tm, ka), lambda i, j: (i, 0)),
            pl.BlockSpec((tm, kb), lambda i, j: (i, 0)),
            pl.BlockSpec((ka, tn), lambda i, j: (0, j)),
            pl.BlockSpec((kb, tn), lambda i, j: (0, j)),
            pl.BlockSpec((tm, tn), lambda i, j: (i, j)),
            pl.BlockSpec((tm, tn), lambda i, j: (i, j)),
        ],
        out_specs=pl.BlockSpec((tm, tn), lambda i, j: (i, j)),
        out_shape=jax.ShapeDtypeStruct((m, n), BF16),
        compiler_params=_params("parallel", "arbitrary"),
        name="merge_branches",
    )(oa, ob, w_pa, w_pb, ga, gb)


def _matmul_res_kernel(*refs, expert):
    if expert is None:
        a_ref, w_ref, r_ref, o_ref = refs
        o_ref[...] = r_ref[...] + _dot(a_ref[...], w_ref[...])
    else:
        a_ref, w_ref, r_ref, c_ref, o_ref = refs
        o_ref[...] = r_ref[...] + c_ref[:, expert:expert + 1] * _dot(a_ref[...], w_ref[...])


def matmul_residual(a, w, res, combine=None, expert=None):
    m, k = a.shape
    n = w.shape[1]
    tm = _pick(m, 512, 8)
    tn = _pick(n, 512, LANES)
    in_specs = [
        pl.BlockSpec((tm, k), lambda i, j: (i, 0)),
        pl.BlockSpec((k, tn), lambda i, j: (0, j)),
        pl.BlockSpec((tm, tn), lambda i, j: (i, j)),
    ]
    args = [a, w, res]
    if combine is not None:
        in_specs.append(pl.BlockSpec((tm, LANES), lambda i, j: (i, 0)))
        args.append(combine)
    return pl.pallas_call(
        functools.partial(_matmul_res_kernel, expert=expert),
        grid=(m // tm, n // tn),
        in_specs=in_specs,
        out_specs=pl.BlockSpec((tm, tn), lambda i, j: (i, j)),
        out_shape=jax.ShapeDtypeStruct((m, n), F32),
        compiler_params=_params("parallel", "arbitrary"),
        name="matmul_residual",
    )(*args)


def _gateup_kernel(x_ref, wg_ref, wu_ref, o_ref):
    x = x_ref[...]
    gate = _dot(x, wg_ref[...])
    up = _dot(x, wu_ref[...])
    o_ref[...] = (jax.nn.silu(gate) * up).astype(o_ref.dtype)


def gate_up(xn, wg, wu):
    m, k = xn.shape
    n = wg.shape[1]
    tm = _pick(m, 1024, 8)
    tn = _pick(n, 512, LANES)
    return pl.pallas_call(
        _gateup_kernel,
        grid=(m // tm, n // tn),
        in_specs=[
            pl.BlockSpec((tm, k), lambda i, j: (i, 0)),
            pl.BlockSpec((k, tn), lambda i, j: (0, j)),
            pl.BlockSpec((k, tn), lambda i, j: (0, j)),
        ],
        out_specs=pl.BlockSpec((tm, tn), lambda i, j: (i, j)),
        out_shape=jax.ShapeDtypeStruct((m, n), BF16),
        compiler_params=_params("parallel", "arbitrary"),
        name="ffn_gate_up",
    )(xn, wg, wu)


def _router_kernel(x_ref, g_ref, wr_ref, c_ref, *, n_exp):
    hn = _rms(x_ref[...], g_ref[...])
    tm = hn.shape[0]
    lane = lax.broadcasted_iota(jnp.int32, (tm, LANES), 1)
    logits = jnp.full((tm, LANES), -jnp.inf, F32)
    for e in range(n_exp):
        le = jnp.sum(hn * wr_ref[e:e + 1, :], axis=1, keepdims=True)
        logits = jnp.where(lane == e, le, logits)
    v1 = jnp.max(logits, axis=1, keepdims=True)
    i1 = jnp.min(jnp.where(logits == v1, lane, LANES), axis=1, keepdims=True)
    rest = jnp.where(lane == i1, -jnp.inf, logits)
    v2 = jnp.max(rest, axis=1, keepdims=True)
    i2 = jnp.min(jnp.where(rest == v2, lane, LANES), axis=1, keepdims=True)
    e2 = jnp.exp(v2 - v1)
    den = 1.0 + e2
    c_ref[...] = jnp.where(lane == i1, 1.0 / den, 0.0) + jnp.where(lane == i2, e2 / den, 0.0)


def moe_router(x, g, w_router):
    m, d = x.shape
    n_exp = w_router.shape[1]
    assert TOP_K_EXPERTS == 2 and n_exp <= LANES
    tm = _pick(m, 256, 8)
    return pl.pallas_call(
        functools.partial(_router_kernel, n_exp=n_exp),
        grid=(m // tm,),
        in_specs=[
            pl.BlockSpec((tm, d), lambda i: (i, 0)),
            pl.BlockSpec((1, d), lambda i: (0, 0)),
            pl.BlockSpec((n_exp, d), lambda i: (0, 0)),
        ],
        out_specs=pl.BlockSpec((tm, LANES), lambda i: (i, 0)),
        out_shape=jax.ShapeDtypeStruct((m, LANES), F32),
        compiler_params=_params("parallel"),
        name="moe_router",
    )(x, g.reshape(1, d), w_router.T)


def _sample_attn_kernel(pt_ref, lq_ref, lk_ref, g_ref, qa_ref, qi_ref, wi_ref, kp_ref, vp_ref, ip_ref,
                        kn_ref, vn_ref, in_ref, o_ref, sc_ref, scn_ref, m_ref, l_ref, acc_ref,
                        *, n_pages, n_heads, dqk, lam_init, n_idx, d_idx):
    p = pl.program_id(1)
    hd = 2 * dqk
    width = n_heads * hd
    rows = 2 * n_heads
    scale = dqk ** -0.5

    q = qa_ref[0].astype(F32)
    r_id = lax.broadcasted_iota(jnp.int32, (rows, width), 0)
    c_id = lax.broadcasted_iota(jnp.int32, (rows, width), 1)
    on_diag = (c_id // dqk) == r_id
    qbd = jnp.where(on_diag, jnp.broadcast_to(q, (rows, width)), 0.0).astype(BF16)

    @pl.when(p == 0)
    def _():
        m_ref[...] = jnp.full(m_ref.shape, -jnp.inf, F32)
        l_ref[...] = jnp.zeros(l_ref.shape, F32)
        acc_ref[...] = jnp.zeros(acc_ref.shape, F32)

    kp = kp_ref[0, 0].astype(BF16)
    vp = vp_ref[0, 0].astype(BF16)
    s = _dot_nt(qbd, kp) * scale
    m_old = m_ref[...]
    m_new = jnp.maximum(m_old, jnp.max(s, axis=1, keepdims=True))
    alpha = jnp.exp(m_old - m_new)
    pr = jnp.exp(s - m_new)
    l_ref[...] = alpha * l_ref[...] + jnp.sum(pr, axis=1, keepdims=True)
    acc_ref[...] = alpha * acc_ref[...] + _dot(pr.astype(BF16), vp)
    m_ref[...] = m_new

    wscale = n_idx ** -0.5 * d_idx ** -0.5
    qi = qi_ref[0]
    w = wi_ref[0] * wscale
    si = jnp.maximum(_dot_nt(qi, ip_ref[0, 0].astype(BF16)), 0.0) * w
    sc_ref[0] = jnp.sum(si, axis=0, keepdims=True)

    @pl.when(p == n_pages - 1)
    def _():
        kn = kn_ref[0]
        vn = vn_ref[0]
        s_new = jnp.sum(qbd.astype(F32) * kn, axis=1, keepdims=True) * scale
        m_o = m_ref[...]
        m_n = jnp.maximum(m_o, s_new)
        a = jnp.exp(m_o - m_n)
        p_new = jnp.exp(s_new - m_n)
        l_fin = a * l_ref[...] + p_new
        acc_fin = a * acc_ref[...] + p_new * vn
        o_full = acc_fin / l_fin
        lam = _lambda(lq_ref, lk_ref, lam_init)
        for h in range(n_heads):
            o_h = (o_full[2 * h:2 * h + 1, h * hd:(h + 1) * hd]
                   - lam * o_full[2 * h + 1:2 * h + 2, h * hd:(h + 1) * hd])
            o_ref[0, :, h * hd:(h + 1) * hd] = (_rms(o_h, g_ref[...]) * (1.0 - lam_init)).astype(o_ref.dtype)
        kin = in_ref[0]
        s_in = jnp.maximum(jnp.sum(qi.astype(F32) * kin, axis=1, keepdims=True), 0.0) * w
        scn_ref[0] = jnp.broadcast_to(jnp.sum(s_in, axis=0, keepdims=True), (1, LANES))


def sample_attention(page_table, layer, cache_k, cache_v, cache_i, qa, qi, wi, k_new, v_new, i_new, lq, lk, g,
                     *, n_heads, dqk, lam_init, n_idx, d_idx):
    db, n_pages = page_table.shape
    width = qa.shape[1]
    page = cache_k.shape[2]
    hd = 2 * dqk
    rows = 2 * n_heads
    pt = page_table.reshape(-1)

    def pool_map(b, p, pt_ref):
        return (layer, pt_ref[b * n_pages + p], 0, 0)

    row3 = lambda b, p, pt_ref: (b, 0, 0)
    const2 = lambda b, p, pt_ref: (0, 0)
    grid_spec = pltpu.PrefetchScalarGridSpec(
        num_scalar_prefetch=1,
        grid=(db, n_pages),
        in_specs=[
            pl.BlockSpec(lq.shape, const2),
            pl.BlockSpec(lk.shape, const2),
            pl.BlockSpec((1, hd), const2),
            pl.BlockSpec((1, 1, width), row3),
            pl.BlockSpec((1, n_idx, d_idx), row3),
            pl.BlockSpec((1, n_idx, 1), row3),
            pl.BlockSpec((1, 1, page, width), pool_map),
            pl.BlockSpec((1, 1, page, width), pool_map),
            pl.BlockSpec((1, 1, page, d_idx), pool_map),
            pl.BlockSpec((1, 1, width), row3),
            pl.BlockSpec((1, 1, width), row3),
            pl.BlockSpec((1, 1, d_idx), row3),
        ],
        out_specs=[
            pl.BlockSpec((1, 1, width), row3),
            pl.BlockSpec((1, 1, page), lambda b, p, pt_ref: (b * n_pages + p, 0, 0)),
            pl.BlockSpec((1, 1, LANES), row3),
        ],
        scratch_shapes=[pltpu.VMEM((rows, 1), F32), pltpu.VMEM((rows, 1), F32), pltpu.VMEM((rows, width), F32)],
    )
    out, scores, score_new = pl.pallas_call(
        functools.partial(_sample_attn_kernel, n_pages=n_pages, n_heads=n_heads, dqk=dqk, lam_init=lam_init,
                          n_idx=n_idx, d_idx=d_idx),
        grid_spec=grid_spec,
        out_shape=[jax.ShapeDtypeStruct((db, 1, width), BF16),
                   jax.ShapeDtypeStruct((db * n_pages, 1, page), F32),
                   jax.ShapeDtypeStruct((db, 1, LANES), F32)],
        compiler_params=_params("parallel", "arbitrary"),
        name="sample_attention",
    )(pt, lq, lk, g.reshape(1, hd), qa.reshape(db, 1, width), qi.reshape(db, n_idx, d_idx),
      wi.reshape(db, n_idx, 1), cache_k, cache_v, cache_i,
      k_new.reshape(db, 1, width), v_new.reshape(db, 1, width), i_new.reshape(db, 1, d_idx))
    return out.reshape(db, width), scores.reshape(db, n_pages, page), score_new


def _count_all(mask):
    c = jnp.sum(mask.astype(jnp.int32), axis=1, keepdims=True)
    return jnp.sum(c, axis=0, keepdims=True)


def _sample_dsa_kernel(pt_ref, sc_ref, scn_ref, qb_ref, kp_ref, vp_ref, kn_ref, vn_ref, o_ref,
                       key_ref, keyn_ref, t_ref, cut_ref, m_ref, l_ref, acc_ref,
                       *, n_pages, page, n_heads, n_kv, dh, topk):
    p = pl.program_id(1)
    rep = n_heads // n_kv
    scale = dh ** -0.5
    past = n_pages * page

    @pl.when(p == 0)
    def _():
        m_ref[...] = jnp.full(m_ref.shape, -jnp.inf, F32)
        l_ref[...] = jnp.zeros(l_ref.shape, F32)
        acc_ref[...] = jnp.zeros(acc_ref.shape, F32)
        key_ref[...] = _order_key(sc_ref[0])
        keyn_ref[...] = _order_key(scn_ref[0])
        key_new = keyn_ref[:, 0:1]

        def cnt(t):
            return _count_all(key_ref[...] >= t) + (key_new >= t).astype(jnp.int32)

        t0 = jnp.full((1, 1), INT_MIN, jnp.int32)
        zero = jnp.zeros((1, 1), jnp.int32)
        t0 = jnp.where(cnt(zero) >= topk, zero, t0)

        def body(it, t):
            cand = t | (jnp.int32(1) << (30 - it))
            return jnp.where(cnt(cand) >= topk, cand, t)

        t = lax.fori_loop(0, 31, body, t0)
        t_ref[...] = t
        key = key_ref[...]
        n_gt = _count_all(key > t) + (key_new > t).astype(jnp.int32)
        need = topk - n_gt
        pos = (lax.broadcasted_iota(jnp.int32, key.shape, 0) * page
               + lax.broadcasted_iota(jnp.int32, key.shape, 1))
        n_bits = int(math.ceil(math.log2(past + 1)))

        def body2(it, x):
            cand = x | (jnp.int32(1) << (n_bits - 1 - it))
            c = _count_all((key_ref[...] == t) & (pos < cand)) + ((key_new == t) & (past < cand)).astype(jnp.int32)
            return jnp.where(c < need, cand, x)

        cut_ref[...] = lax.fori_loop(0, n_bits, body2, jnp.zeros((1, 1), jnp.int32))

    t = t_ref[...]
    cut = cut_ref[...]
    q = qb_ref[0].astype(F32)
    r_id = lax.broadcasted_iota(jnp.int32, (n_heads, dh), 0)
    qbd = jnp.concatenate([jnp.where(r_id // rep == g, q, 0.0) for g in range(n_kv)], axis=1).astype(BF16)

    key_row = key_ref[pl.ds(p, 1), :]
    pos_row = p * page + lax.broadcasted_iota(jnp.int32, (1, page), 1)
    sel = (key_row > t) | ((key_row == t) & (pos_row <= cut))
    s = _dot_nt(qbd, kp_ref[0, 0].astype(BF16)) * scale
    s = jnp.where(sel, s, -jnp.inf)
    m_old = m_ref[...]
    m_new = jnp.maximum(m_old, jnp.max(s, axis=1, keepdims=True))
    m_safe = jnp.where(m_new == -jnp.inf, 0.0, m_new)
    alpha = jnp.exp(m_old - m_safe)
    pr = jnp.exp(s - m_safe)
    l_ref[...] = alpha * l_ref[...] + jnp.sum(pr, axis=1, keepdims=True)
    acc_ref[...] = alpha * acc_ref[...] + _dot(pr.astype(BF16), vp_ref[0, 0].astype(BF16))
    m_ref[...] = m_new

    @pl.when(p == n_pages - 1)
    def _():
        key_new = keyn_ref[:, 0:1]
        sel_new = (key_new > t) | ((key_new == t) & (past <= cut))
        s_new = jnp.sum(qbd.astype(F32) * kn_ref[0], axis=1, keepdims=True) * scale
        s_new = jnp.where(sel_new, s_new, -jnp.inf)
        m_o = m_ref[...]
        m_n = jnp.maximum(m_o, s_new)
        m_s = jnp.where(m_n == -jnp.inf, 0.0, m_n)
        a = jnp.exp(m_o - m_s)
        p_new = jnp.exp(s_new - m_s)
        l_fin = a * l_ref[...] + p_new
        o_full = (a * acc_ref[...] + p_new * vn_ref[0]) / l_fin
        o = jnp.zeros((n_heads, dh), F32)
        for g in range(n_kv):
            o = jnp.where(r_id // rep == g, o_full[:, g * dh:(g + 1) * dh], o)
        o_ref[0] = o.astype(o_ref.dtype)


def sample_dsa(page_table, layer, cache_k, cache_v, scores, score_new, qb, k_new, v_new,
               *, n_heads, n_kv, dh, topk):
    db, n_pages = page_table.shape
    page = cache_k.shape[2]
    kvw = n_kv * dh
    pt = page_table.reshape(-1)

    def pool_map(b, p, pt_ref):
        return (layer, pt_ref[b * n_pages + p], 0, 0)

    row3 = lambda b, p, pt_ref: (b, 0, 0)
    grid_spec = pltpu.PrefetchScalarGridSpec(
        num_scalar_prefetch=1,
        grid=(db, n_pages),
        in_specs=[
            pl.BlockSpec((1, n_pages, page), row3),
            pl.BlockSpec((1, 1, LANES), row3),
            pl.BlockSpec((1, n_heads, dh), row3),
            pl.BlockSpec((1, 1, page, kvw), pool_map),
            pl.BlockSpec((1, 1, page, kvw), pool_map),
            pl.BlockSpec((1, 1, kvw), row3),
            pl.BlockSpec((1, 1, kvw), row3),
        ],
        out_specs=pl.BlockSpec((1, n_heads, dh), row3),
        scratch_shapes=[pltpu.VMEM((n_pages, page), jnp.int32), pltpu.VMEM((1, LANES), jnp.int32),
                        pltpu.VMEM((1, 1), jnp.int32), pltpu.VMEM((1, 1), jnp.int32),
                        pltpu.VMEM((n_heads, 1), F32), pltpu.VMEM((n_heads, 1), F32),
                        pltpu.VMEM((n_heads, kvw), F32)],
    )
    out = pl.pallas_call(
        functools.partial(_sample_dsa_kernel, n_pages=n_pages, page=page, n_heads=n_heads, n_kv=n_kv, dh=dh,
                          topk=topk),
        grid_spec=grid_spec,
        out_shape=jax.ShapeDtypeStruct((db, n_heads, dh), BF16),
        compiler_params=_params("parallel", "arbitrary"),
        name="sample_dsa",
    )(pt, scores, score_new, qb.reshape(db, n_heads, dh), cache_k, cache_v,
      k_new.reshape(db, 1, kvw), v_new.reshape(db, 1, kvw))
    return out.reshape(db, n_heads * dh)


def kernel(x_prompt, x_sample, cache_a_k, cache_a_v, cache_b_k, cache_b_v, cache_idx_k, page_table, w_in, lambda_q, lambda_k, subln_g, w_branch_a, w_branch_b, w_out, norm_mix_g, norm_ffn_g, w_dense_gate, w_dense_up, w_dense_down, w_router, w_exp_gate, w_exp_up, w_exp_down, norm_final_g):
    batch, seq, d_model = x_prompt.shape
    db, dec_seq, _ = x_sample.shape
    assert dec_seq == 1, "one new token per sample row"
    depth, n_pool, page, n_ha, a_width = cache_a_k.shape
    assert page == PAGE_SIZE
    dqk = a_width // 2
    dv = cache_a_v.shape[4]
    n_kv, dh = cache_b_k.shape[3:]
    d_idx = cache_idx_k.shape[3]
    n_hb = w_branch_b.shape[1] // dh
    n_in = w_in.shape[2]
    wa, wva, wqb, wkb = n_ha * 2 * dqk, n_ha * dv, n_hb * dh, n_kv * dh
    n_idx = (n_in - (2 * wa + wva + wqb + 2 * wkb + d_idx + 2 * d_model)) // (d_idx + 1)
    assert dv == 2 * dqk == LANES and dh == LANES and 2 * d_idx == LANES
    n_pages = page_table.shape[1]
    past_len = n_pages * page
    topk_p = min(TOPK_MAX, seq // 4)
    topk_s = min(TOPK_MAX, (past_len + dec_seq) // 4)
    n_exp = w_router.shape[2]

    offs = [0]
    for wdt in (wa, wa, wva, wqb, wkb, wkb, n_idx * d_idx, d_idx, n_idx, d_model, d_model):
        offs.append(offs[-1] + wdt)
    o_qa, o_ka, o_va, o_qb, o_kb, o_vb, o_qi, o_ki, o_wi, o_ga, o_gb, _ = offs

    pos_p = jnp.arange(seq, dtype=F32)
    pos_s = jnp.full((db,), float(past_len), F32)
    tabs = {}
    for name, pos in (("p", pos_p), ("s", pos_s)):
        tabs[name, "a"] = rope_table(pos, dqk)
        tabs[name, "b"] = rope_table(pos, dh)
        tabs[name, "i"] = rope_table(pos, d_idx, valid_lanes=d_idx)

    ck_a = cache_a_k.reshape(depth, n_pool, page, wa)
    cv_a = cache_a_v.reshape(depth, n_pool, page, wva)
    ck_b = cache_b_k.reshape(depth, n_pool, page, wkb)
    cv_b = cache_b_v.reshape(depth, n_pool, page, wkb)

    xp = x_prompt.reshape(batch * seq, d_model)
    xs = x_sample.reshape(db * dec_seq, d_model)
    rows = {"p": [[] for _ in range(5)], "s": [[] for _ in range(5)]}

    for l in range(depth):
        lam_init = 0.8 - 0.6 * math.exp(-0.3 * l)
        wl = w_in[l].astype(BF16)
        w_kiwi = jnp.pad(wl[:, o_ki:o_ga], ((0, 0), (0, LANES - d_idx - n_idx)))
        w_pa = w_branch_a[l].astype(BF16)
        w_pb = w_branch_b[l].astype(BF16)
        w_o = w_out[l].astype(BF16)
        lq, lk = lambda_q[l], lambda_k[l]

        def in_proj(x, grp):
            xn = rmsnorm(x, norm_mix_g[l], BF16)
            ta, ha = tabs[grp, "a"]
            tb, hb = tabs[grp, "b"]
            ti, hi = tabs[grp, "i"]
            z = {}
            z["qa"], = project(xn, wl[:, o_qa:o_ka], [BF16], ta, ha)
            z["ka32"], z["ka"] = project(xn, wl[:, o_ka:o_va], [F32, BF16], ta, ha)
            z["va32"], z["va"] = project(xn, wl[:, o_va:o_qb], [F32, BF16])
            z["qb"], = project(xn, wl[:, o_qb:o_kb], [BF16], tb, hb)
            z["kb32"], z["kb"] = project(xn, wl[:, o_kb:o_vb], [F32, BF16], tb, hb)
            z["vb32"], z["vb"] = project(xn, wl[:, o_vb:o_qi], [F32, BF16])
            z["qi"], = project(xn, wl[:, o_qi:o_ki], [BF16], ta, ha)
            z["kiwi"], = project(xn, w_kiwi, [F32], ti, hi)
            z["ga"], = project(xn, wl[:, o_ga:o_gb], [F32])
            z["gb"], = project(xn, wl[:, o_gb:], [F32])
            return z

        def mix_out(x, z, oa, ob):
            u = merge_branches(oa, ob, w_pa, w_pb, z["ga"], z["gb"])
            return matmul_residual(u, w_o, x)

        zp = in_proj(xp, "p")
        oa = diff_attention_prompt(zp["qa"], zp["ka"], zp["va"], lq, lk, subln_g[l], batch=batch, seq=seq,
                                   n_heads=n_ha, dqk=dqk, lam_init=lam_init)
        ob = dsa_prompt(zp["qi"], zp["kiwi"], zp["qb"], zp["kb"], zp["vb"], batch=batch, seq=seq, n_idx=n_idx,
                        d_idx=d_idx, n_heads=n_hb, n_kv=n_kv, dh=dh, topk=topk_p)
        xp = mix_out(xp, zp, oa, ob)

        zs = in_proj(xs, "s")
        ki_s = zs["kiwi"][:, :d_idx]
        wi_s = zs["kiwi"][:, d_idx:d_idx + n_idx]
        oa, scores, score_new = sample_attention(
            page_table, l, ck_a, cv_a, cache_idx_k, zs["qa"], zs["qi"], wi_s, zs["ka32"], zs["va32"], ki_s,
            lq, lk, subln_g[l], n_heads=n_ha, dqk=dqk, lam_init=lam_init, n_idx=n_idx, d_idx=d_idx)
        ob = sample_dsa(page_table, l, ck_b, cv_b, scores, score_new, zs["qb"], zs["kb32"], zs["vb32"],
                        n_heads=n_hb, n_kv=n_kv, dh=dh, topk=topk_s)
        xs = mix_out(xs, zs, oa, ob)

        for grp, z, bt, tt in (("p", zp, batch, seq), ("s", zs, db, dec_seq)):
            rows[grp][0].append(z["ka32"].reshape(bt, tt, n_ha, 2 * dqk))
            rows[grp][1].append(z["va32"].reshape(bt, tt, n_ha, dv))
            rows[grp][2].append(z["kb32"].reshape(bt, tt, n_kv, dh))
            rows[grp][3].append(z["vb32"].reshape(bt, tt, n_kv, dh))
            rows[grp][4].append(z["kiwi"][:, :d_idx].reshape(bt, tt, d_idx))

        i = l // 2
        hp = rmsnorm(xp, norm_ffn_g[l], BF16)
        hs = rmsnorm(xs, norm_ffn_g[l], BF16)
        if l % 2 == 0:
            wg, wu, wd = (w_dense_gate[i].astype(BF16), w_dense_up[i].astype(BF16), w_dense_down[i].astype(BF16))
            xp = matmul_residual(gate_up(hp, wg, wu), wd, xp)
            xs = matmul_residual(gate_up(hs, wg, wu), wd, xs)
        else:
            cp = moe_router(xp, norm_ffn_g[l], w_router[i])
            cs = moe_router(xs, norm_ffn_g[l], w_router[i])
            for e in range(n_exp):
                wg, wu, wd = (w_exp_gate[i, e].astype(BF16), w_exp_up[i, e].astype(BF16),
                              w_exp_down[i, e].astype(BF16))
                xp = matmul_residual(gate_up(hp, wg, wu), wd, xp, cp, e)
                xs = matmul_residual(gate_up(hs, wg, wu), wd, xs, cs, e)

    y_prompt = rmsnorm(xp, norm_final_g, F32).reshape(batch, seq, d_model)
    y_sample = rmsnorm(xs, norm_final_g, F32).reshape(db, dec_seq, d_model)
    outs_p = [jnp.stack(r, axis=0) for r in rows["p"]]
    outs_s = [jnp.stack(r, axis=0) for r in rows["s"]]
    return (y_prompt, y_sample, *outs_p, *outs_s)
```

```python
import functools
import math

import jax
import jax.numpy as jnp
from jax import lax
from jax.experimental import pallas as pl
from jax.experimental.pallas import tpu as pltpu

LANES = 128
VMEM_LIMIT_BYTES = 56 * 1024 * 1024
ROPE_THETA = 500000.0
ROPE_FRAC = 4
RMS_EPS = 1e-6
TOPK_MAX = 256
PAGE_SIZE = 128
TOP_K_EXPERTS = 2
INT_MIN = -2 ** 31
F32 = jnp.float32
BF16 = jnp.bfloat16


def _pick(n, pref, mult):
    if n <= pref:
        return n
    t = (pref // mult) * mult
    while t >= mult:
        if n % t == 0:
            return t
        t -= mult
    return n


def _params(*sem):
    return pltpu.CompilerParams(dimension_semantics=sem, vmem_limit_bytes=VMEM_LIMIT_BYTES)


def _precision(a, b):
    return lax.Precision.HIGHEST if a.dtype == F32 and b.dtype == F32 else None


def _dot(a, b):
    return jnp.dot(a, b, preferred_element_type=F32, precision=_precision(a, b))


def _dot_nt(a, b):
    return lax.dot_general(a, b, (((1,), (1,)), ((), ())), preferred_element_type=F32,
                           precision=_precision(a, b))


def _rms(x, g):
    return x * lax.rsqrt(jnp.mean(x * x, axis=-1, keepdims=True) + RMS_EPS) * g


def _rmsnorm_kernel(x_ref, g_ref, o_ref):
    o_ref[...] = _rms(x_ref[...], g_ref[...]).astype(o_ref.dtype)


def rmsnorm(x, g, out_dtype):
    m, d = x.shape
    tm = _pick(m, 512, 8)
    return pl.pallas_call(
        _rmsnorm_kernel,
        grid=(m // tm,),
        in_specs=[pl.BlockSpec((tm, d), lambda i: (i, 0)), pl.BlockSpec((1, d), lambda i: (0, 0))],
        out_specs=pl.BlockSpec((tm, d), lambda i: (i, 0)),
        out_shape=jax.ShapeDtypeStruct((m, d), out_dtype),
        compiler_params=_params("parallel"),
        name="rmsnorm",
    )(x, g.reshape(1, d))


def _proj_kernel(*refs, half, tn):
    if half:
        x_ref, w_ref, tab_ref, *o_refs = refs
    else:
        x_ref, w_ref, *o_refs = refs
    z = _dot(x_ref[...], w_ref[...])
    if not half:
        for o in o_refs:
            o[...] = z.astype(o.dtype)
        return
    c, s_lo, s_hi = tab_ref[0], tab_ref[1], tab_ref[2]
    for g in range(tn // LANES):
        zg = z[:, g * LANES:(g + 1) * LANES]
        r = zg * c + pltpu.roll(zg, LANES - half, 1) * s_lo + pltpu.roll(zg, half, 1) * s_hi
        for o in o_refs:
            o[:, g * LANES:(g + 1) * LANES] = r.astype(o.dtype)


def project(xn, w, out_dtypes, tab=None, half=0, tn_pref=512):
    m, k = xn.shape
    n = w.shape[1]
    p_rows = m if tab is None else tab.shape[1]
    assert m % p_rows == 0
    tm = _pick(p_rows, 1024, 8)
    tn = _pick(n, tn_pref, LANES)
    in_specs = [pl.BlockSpec((tm, k), lambda i, j: (i, 0)), pl.BlockSpec((k, tn), lambda i, j: (0, j))]
    args = [xn, w]
    if tab is not None:
        nblk = p_rows // tm
        in_specs.append(pl.BlockSpec((3, tm, LANES), lambda i, j: (0, i % nblk, 0)))
        args.append(tab)
    outs = pl.pallas_call(
        functools.partial(_proj_kernel, half=half if tab is not None else 0, tn=tn),
        grid=(m // tm, n // tn),
        in_specs=in_specs,
        out_specs=[pl.BlockSpec((tm, tn), lambda i, j: (i, j)) for _ in out_dtypes],
        out_shape=[jax.ShapeDtypeStruct((m, n), dt) for dt in out_dtypes],
        compiler_params=_params("parallel", "arbitrary"),
        name="in_proj",
    )(*args)
    return outs


def rope_table(pos, head_dim, valid_lanes=LANES):
    rot = head_dim // ROPE_FRAC
    half = rot // 2
    inv_freq = jnp.power(ROPE_THETA, -jnp.arange(half, dtype=F32) * 2.0 / rot)
    ang = pos[:, None] * inv_freq[None, :]
    cos, sin = jnp.cos(ang), jnp.sin(ang)
    n = pos.shape[0]
    ones = jnp.ones((n, head_dim - rot), F32)
    zeros_h = jnp.zeros((n, half), F32)
    zeros_r = jnp.zeros((n, head_dim - rot), F32)
    c = jnp.concatenate([cos, cos, ones], axis=1)
    s_lo = jnp.concatenate([-sin, zeros_h, zeros_r], axis=1)
    s_hi = jnp.concatenate([zeros_h, sin, zeros_r], axis=1)
    reps = LANES // head_dim
    tabs = [jnp.tile(t, (1, reps)) for t in (c, s_lo, s_hi)]
    if valid_lanes < LANES:
        lane = jnp.arange(LANES)[None, :]
        tabs = [jnp.where(lane < valid_lanes, tabs[0], 1.0), jnp.where(lane < valid_lanes, tabs[1], 0.0),
                jnp.where(lane < valid_lanes, tabs[2], 0.0)]
    return jnp.stack(tabs, axis=0), half


def _lambda(lq_ref, lk_ref, lam_init):
    lq = lq_ref[...].astype(F32)
    lk = lk_ref[...].astype(F32)
    prod = lq * lk
    return (jnp.exp(jnp.sum(prod[0:1, :], axis=1, keepdims=True))
            - jnp.exp(jnp.sum(prod[1:2, :], axis=1, keepdims=True)) + lam_init)


def _diffattn_kernel(lq_ref, lk_ref, g_ref, q_ref, k_ref, v_ref, o_ref, m_ref, l_ref, acc_ref,
                     *, tq, dqk, lam_init):
    i = pl.program_id(2)
    scale = dqk ** -0.5
    q = q_ref[...].astype(F32)
    lane = lax.broadcasted_iota(jnp.int32, q.shape, 1)
    qs = jnp.concatenate([jnp.where(lane < dqk, q, 0.0), jnp.where(lane >= dqk, q, 0.0)], axis=0)
    qs = qs.astype(q_ref.dtype)
    m_ref[...] = jnp.full(m_ref.shape, -jnp.inf, F32)
    l_ref[...] = jnp.zeros(l_ref.shape, F32)
    acc_ref[...] = jnp.zeros(acc_ref.shape, F32)

    def step(j, masked):
        start = pl.multiple_of(j * tq, tq)
        kt = k_ref[pl.ds(start, tq), :]
        vt = v_ref[pl.ds(start, tq), :]
        s = _dot_nt(qs, kt) * scale
        if masked:
            r = lax.broadcasted_iota(jnp.int32, s.shape, 0) & (tq - 1)
            c = lax.broadcasted_iota(jnp.int32, s.shape, 1)
            s = jnp.where(c <= r, s, -jnp.inf)
        m_old = m_ref[...]
        m_new = jnp.maximum(m_old, jnp.max(s, axis=1, keepdims=True))
        alpha = jnp.exp(m_old - m_new)
        p = jnp.exp(s - m_new)
        l_ref[...] = alpha * l_ref[...] + jnp.sum(p, axis=1, keepdims=True)
        acc_ref[...] = alpha * acc_ref[...] + _dot(p.astype(vt.dtype), vt)
        m_ref[...] = m_new

    def body(j, carry):
        step(j, False)
        return carry

    lax.fori_loop(0, i, body, 0)
    step(i, True)

    lam = _lambda(lq_ref, lk_ref, lam_init)
    o_all = acc_ref[...] / l_ref[...]
    o = o_all[:tq] - lam * o_all[tq:]
    o_ref[...] = (_rms(o, g_ref[...]) * (1.0 - lam_init)).astype(o_ref.dtype)


def diff_attention_prompt(q, k, v, lq, lk, g, *, batch, seq, n_heads, dqk, lam_init):
    m, width = q.shape
    hd = width // n_heads
    tq = _pick(seq, 512, LANES)
    assert tq & (tq - 1) == 0
    nq = seq // tq
    return pl.pallas_call(
        functools.partial(_diffattn_kernel, tq=tq, dqk=dqk, lam_init=lam_init),
        grid=(batch, n_heads, nq),
        in_specs=[
            pl.BlockSpec(lq.shape, lambda b, h, i: (0, 0)),
            pl.BlockSpec(lk.shape, lambda b, h, i: (0, 0)),
            pl.BlockSpec((1, hd), lambda b, h, i: (0, 0)),
            pl.BlockSpec((tq, hd), lambda b, h, i: (b * nq + i, h)),
            pl.BlockSpec((seq, hd), lambda b, h, i: (b, h)),
            pl.BlockSpec((seq, hd), lambda b, h, i: (b, h)),
        ],
        out_specs=pl.BlockSpec((tq, hd), lambda b, h, i: (b * nq + i, h)),
        out_shape=jax.ShapeDtypeStruct((m, width), BF16),
        scratch_shapes=[pltpu.VMEM((2 * tq, 1), F32), pltpu.VMEM((2 * tq, 1), F32), pltpu.VMEM((2 * tq, hd), F32)],
        compiler_params=_params("parallel", "parallel", "arbitrary"),
        name="diff_attn_prompt",
    )(lq, lk, g.reshape(1, hd), q, k, v)


def _order_key(score):
    score = jnp.where(score == 0.0, 0.0, score)
    bits = lax.bitcast_convert_type(score, jnp.int32)
    return jnp.where(bits < 0, bits ^ jnp.int32(0x7FFFFFFF), bits)


def _count(mask):
    return jnp.sum(mask.astype(jnp.int32), axis=1, keepdims=True)


def _kth_largest_key(load_keys, rows, k, count_fn):
    t0 = jnp.full((rows, 1), INT_MIN, jnp.int32)
    zero = jnp.zeros((rows, 1), jnp.int32)
    t0 = jnp.where(count_fn(load_keys() >= zero) >= k, zero, t0)

    def body(it, t):
        cand = t | (jnp.int32(1) << (30 - it))
        return jnp.where(count_fn(load_keys() >= cand) >= k, cand, t)

    return lax.fori_loop(0, 31, body, t0)


def _tie_cut(load_eq_pos, rows, need, n_bits, count_fn):
    def body(it, x):
        cand = x | (jnp.int32(1) << (n_bits - 1 - it))
        eq, pos = load_eq_pos()
        return jnp.where(count_fn(eq & (pos < cand)) < need, cand, x)

    return lax.fori_loop(0, n_bits, body, jnp.zeros((rows, 1), jnp.int32))


def _dsa_prompt_kernel(qi_ref, kiwi_k_ref, kiwi_q_ref, qb_ref, kb_ref, vb_ref, o_ref,
                       kdup_ref, key_ref, bias_ref, cut_ref,
                       *, tq, seq, extents, n_idx, d_idx, n_heads, n_kv, dh, topk):
    i = pl.program_id(1)

    @pl.when(i == 0)
    def _():
        kf = kiwi_k_ref[...]
        lane = lax.broadcasted_iota(jnp.int32, kf.shape, 1)
        klo = jnp.where(lane < d_idx, kf, 0.0)
        kdup_ref[...] = (klo + pltpu.roll(klo, d_idx, 1)).astype(kdup_ref.dtype)

    wscale = n_idx ** -0.5 * d_idx ** -0.5
    scale = dh ** -0.5
    rep = n_heads // n_kv

    def run(ncol):
        kd = kdup_ref[:ncol, :]
        w_all = kiwi_q_ref[...]
        score = jnp.zeros((tq, ncol), F32)
        for h in range(n_idx):
            grp = qi_ref[:, (h // 2) * LANES:(h // 2 + 1) * LANES].astype(F32)
            lane = lax.broadcasted_iota(jnp.int32, grp.shape, 1)
            keep = (lane < d_idx) if h % 2 == 0 else (lane >= d_idx)
            qh = jnp.where(keep, grp, 0.0).astype(qi_ref.dtype)
            w = w_all[:, d_idx + h:d_idx + h + 1] * wscale
            score = score + jnp.maximum(_dot_nt(qh, kd), 0.0) * w

        row = i * tq + lax.broadcasted_iota(jnp.int32, (tq, ncol), 0)
        col = lax.broadcasted_iota(jnp.int32, (tq, ncol), 1)
        causal = col <= row
        key_ref[:, :ncol] = jnp.where(causal, _order_key(score), INT_MIN)

        t = _kth_largest_key(lambda: key_ref[:, :ncol], tq, topk, _count)
        key = key_ref[:, :ncol]
        eq = key == t
        need = topk - _count(key > t)
        tie = (_count(eq) > need) & (t > INT_MIN)
        cut_ref[...] = jnp.full((tq, 1), ncol, jnp.int32)

        @pl.when(jnp.max(tie.astype(jnp.int32)) > 0)
        def _():
            def load():
                return key_ref[:, :ncol] == t, lax.broadcasted_iota(jnp.int32, (tq, ncol), 1)
            cut_ref[...] = _tie_cut(load, tq, need, int(math.log2(seq)), _count)

        sel = ((key > t) | (eq & (col <= cut_ref[...]))) & causal
        bias_ref[:, :ncol] = jnp.where(sel, 0.0, -jnp.inf)

        for h in range(n_heads):
            g = h // rep
            s = (_dot_nt(qb_ref[:, h * dh:(h + 1) * dh], kb_ref[:ncol, g * dh:(g + 1) * dh]) * scale
                 + bias_ref[:, :ncol])
            p = jnp.exp(s - jnp.max(s, axis=1, keepdims=True))
            l = jnp.sum(p, axis=1, keepdims=True)
            o = _dot(p.astype(vb_ref.dtype), vb_ref[:ncol, g * dh:(g + 1) * dh]) / l
            o_ref[:, h * dh:(h + 1) * dh] = o.astype(o_ref.dtype)

    lo = 0
    for ncol in extents:
        @pl.when(((i + 1) * tq > lo) & ((i + 1) * tq <= ncol))
        def _(ncol=ncol):
            run(ncol)
        lo = ncol


def dsa_prompt(qi, kiwi, qb, kb, vb, *, batch, seq, n_idx, d_idx, n_heads, n_kv, dh, topk):
    m = qi.shape[0]
    assert 2 * d_idx == LANES and seq & (seq - 1) == 0
    tq = _pick(seq, 256, LANES)
    nq = seq // tq
    n_ext = min(4, nq)
    extents = tuple(seq * (c + 1) // n_ext for c in range(n_ext))
    return pl.pallas_call(
        functools.partial(_dsa_prompt_kernel, tq=tq, seq=seq, extents=extents, n_idx=n_idx, d_idx=d_idx,
                          n_heads=n_heads,
                          n_kv=n_kv, dh=dh, topk=topk),
        grid=(batch, nq),
        in_specs=[
            pl.BlockSpec((tq, qi.shape[1]), lambda b, i: (b * nq + i, 0)),
            pl.BlockSpec((seq, LANES), lambda b, i: (b, 0)),
            pl.BlockSpec((tq, LANES), lambda b, i: (b * nq + i, 0)),
            pl.BlockSpec((tq, qb.shape[1]), lambda b, i: (b * nq + i, 0)),
            pl.BlockSpec((seq, kb.shape[1]), lambda b, i: (b, 0)),
            pl.BlockSpec((seq, vb.shape[1]), lambda b, i: (b, 0)),
        ],
        out_specs=pl.BlockSpec((tq, qb.shape[1]), lambda b, i: (b * nq + i, 0)),
        out_shape=jax.ShapeDtypeStruct((m, qb.shape[1]), BF16),
        scratch_shapes=[pltpu.VMEM((seq, LANES), BF16), pltpu.VMEM((tq, seq), jnp.int32),
                        pltpu.VMEM((tq, seq), F32), pltpu.VMEM((tq, 1), jnp.int32)],
        compiler_params=_params("parallel", "arbitrary"),
        name="dsa_prompt",
    )(qi, kiwi, kiwi, qb, kb, vb)


def _merge_kernel(oa_ref, ob_ref, wa_ref, wb_ref, ga_ref, gb_ref, o_ref):
    ya = _dot(oa_ref[...], wa_ref[...])
    yb = _dot(ob_ref[...], wb_ref[...])
    u = jax.nn.sigmoid(ga_ref[...]) * ya + jax.nn.sigmoid(gb_ref[...]) * yb
    o_ref[...] = u.astype(o_ref.dtype)


def merge_branches(oa, ob, w_pa, w_pb, ga, gb):
    m, ka = oa.shape
    kb = ob.shape[1]
    n = w_pa.shape[1]
    tm = _pick(m, 1024, 8)
    tn = _pick(n, 512, LANES)
    return pl.pallas_call(
        _merge_kernel,
        grid=(m // tm, n // tn),
        in_specs=[
            pl.BlockSpec((tm, ka), lambda i, j: (i, 0)),
            pl.BlockSpec((tm, kb), lambda i, j: (i, 0)),
            pl.BlockSpec((ka, tn), lambda i, j: (0, j)),
            pl.BlockSpec((kb, tn), lambda i, j: (0, j)),
            pl.BlockSpec((tm, tn), lambda i, j: (i, j)),
            pl.BlockSpec((tm, tn), lambda i, j: (i, j)),
        ],
        out_specs=pl.BlockSpec((tm, tn), lambda i, j: (i, j)),
        out_shape=jax.ShapeDtypeStruct((m, n), oa.dtype),
        compiler_params=_params("parallel", "arbitrary"),
        name="merge_branches",
    )(oa, ob, w_pa, w_pb, ga, gb)


def _matmul_res_kernel(a_ref, w_ref, r_ref, o_ref):
    o_ref[...] = r_ref[...] + _dot(a_ref[...], w_ref[...])


def matmul_residual(a, w, res):
    m, k = a.shape
    n = w.shape[1]
    tm = _pick(m, 512, 8)
    tn = _pick(n, 512, LANES)
    return pl.pallas_call(
        _matmul_res_kernel,
        grid=(m // tm, n // tn),
        in_specs=[
            pl.BlockSpec((tm, k), lambda i, j: (i, 0)),
            pl.BlockSpec((k, tn), lambda i, j: (0, j)),
            pl.BlockSpec((tm, tn), lambda i, j: (i, j)),
        ],
        out_specs=pl.BlockSpec((tm, tn), lambda i, j: (i, j)),
        out_shape=jax.ShapeDtypeStruct((m, n), F32),
        compiler_params=_params("parallel", "arbitrary"),
        name="matmul_residual",
    )(a, w, res)


def _gateup_kernel(x_ref, wg_ref, wu_ref, o_ref):
    x = x_ref[...]
    gate = _dot(x, wg_ref[...])
    up = _dot(x, wu_ref[...])
    o_ref[...] = (jax.nn.silu(gate) * up).astype(o_ref.dtype)


def gate_up(xn, wg, wu):
    m, k = xn.shape
    n = wg.shape[1]
    tm = _pick(m, 1024, 8)
    tn = _pick(n, 512, LANES)
    return pl.pallas_call(
        _gateup_kernel,
        grid=(m // tm, n // tn),
        in_specs=[
            pl.BlockSpec((tm, k), lambda i, j: (i, 0)),
            pl.BlockSpec((k, tn), lambda i, j: (0, j)),
            pl.BlockSpec((k, tn), lambda i, j: (0, j)),
        ],
        out_specs=pl.BlockSpec((tm, tn), lambda i, j: (i, j)),
        out_shape=jax.ShapeDtypeStruct((m, n), xn.dtype),
        compiler_params=_params("parallel", "arbitrary"),
        name="ffn_gate_up",
    )(xn, wg, wu)


def _router_kernel(x_ref, g_ref, wr_ref, base_ref, hn_ref, gate_ref, route_ref, cnt_ref, run_ref, *, n_exp):
    @pl.when(pl.program_id(0) == 0)
    def _():
        run_ref[...] = base_ref[...]

    hn = _rms(x_ref[...], g_ref[...])
    hn_ref[...] = hn
    tm = hn.shape[0]
    lane = lax.broadcasted_iota(jnp.int32, (tm, LANES), 1)
    logits = jnp.full((tm, LANES), -jnp.inf, F32)
    for e in range(n_exp):
        le = jnp.sum(hn * wr_ref[e:e + 1, :], axis=1, keepdims=True)
        logits = jnp.where(lane == e, le, logits)
    v1 = jnp.max(logits, axis=1, keepdims=True)
    i1 = jnp.min(jnp.where(logits == v1, lane, LANES), axis=1, keepdims=True)
    rest = jnp.where(lane == i1, -jnp.inf, logits)
    v2 = jnp.max(rest, axis=1, keepdims=True)
    i2 = jnp.min(jnp.where(rest == v2, lane, LANES), axis=1, keepdims=True)
    e2 = jnp.exp(v2 - v1)
    den = 1.0 + e2
    gate_ref[...] = jnp.where(lane == 0, 1.0 / den, jnp.where(lane == 1, e2 / den, 0.0))

    onehot = jnp.where((lane == i1) | (lane == i2), 1.0, 0.0)
    r = lax.broadcasted_iota(jnp.int32, (tm, tm), 0)
    c = lax.broadcasted_iota(jnp.int32, (tm, tm), 1)
    before = _dot(jnp.where(c < r, 1.0, 0.0).astype(BF16), onehot.astype(BF16)) + run_ref[...]
    r1 = jnp.sum(jnp.where(lane == i1, before, 0.0), axis=1, keepdims=True).astype(jnp.int32)
    r2 = jnp.sum(jnp.where(lane == i2, before, 0.0), axis=1, keepdims=True).astype(jnp.int32)
    route_ref[...] = jnp.where(lane == 0, i1, jnp.where(lane == 1, i2, jnp.where(lane == 2, r1,
                               jnp.where(lane == 3, r2, 0))))
    run_ref[...] = run_ref[...] + jnp.sum(onehot, axis=0, keepdims=True)
    cnt_ref[...] = run_ref[...]


def moe_router(x, g, w_router, base_counts):
    m, d = x.shape
    n_exp = w_router.shape[1]
    assert TOP_K_EXPERTS == 2 and n_exp <= LANES
    tm = _pick(m, 256, 8)
    row = lambda i: (i, 0)
    fixed = lambda i: (0, 0)
    return pl.pallas_call(
        functools.partial(_router_kernel, n_exp=n_exp),
        grid=(m // tm,),
        in_specs=[
            pl.BlockSpec((tm, d), row),
            pl.BlockSpec((1, d), fixed),
            pl.BlockSpec((n_exp, d), fixed),
            pl.BlockSpec((1, LANES), fixed),
        ],
        out_specs=[pl.BlockSpec((tm, d), row), pl.BlockSpec((tm, LANES), row), pl.BlockSpec((tm, LANES), row),
                   pl.BlockSpec((1, LANES), fixed)],
        out_shape=[jax.ShapeDtypeStruct((m, d), F32), jax.ShapeDtypeStruct((m, LANES), F32),
                   jax.ShapeDtypeStruct((m, LANES), jnp.int32), jax.ShapeDtypeStruct((1, LANES), F32)],
        scratch_shapes=[pltpu.VMEM((1, LANES), F32)],
        compiler_params=_params("arbitrary"),
        name="moe_router",
    )(x, g.reshape(1, d), w_router.T, base_counts)


def _row_copy(src_ref, src_row, dst_ref, dst_row, sem):
    return pltpu.make_async_copy(src_ref.at[pl.ds(src_row, 1)], dst_ref.at[pl.ds(dst_row, 1)], sem)


def _dispatch_kernel(pos1_ref, pos2_ref, x_ref, xs_in_ref, xs_ref, sem, *, tm):
    del xs_in_ref
    base = pl.program_id(0) * tm

    def start(r, carry):
        _row_copy(x_ref, r, xs_ref, pos1_ref[base + r], sem).start()
        _row_copy(x_ref, r, xs_ref, pos2_ref[base + r], sem).start()
        return carry

    def wait(r, carry):
        _row_copy(x_ref, 0, xs_ref, 0, sem).wait()
        _row_copy(x_ref, 0, xs_ref, 0, sem).wait()
        return carry

    lax.fori_loop(0, tm, start, 0)
    lax.fori_loop(0, tm, wait, 0)


def moe_dispatch(hn, pos1, pos2, x_sorted):
    m, d = hn.shape
    tm = _pick(m, 256, 8)
    grid_spec = pltpu.PrefetchScalarGridSpec(
        num_scalar_prefetch=2,
        grid=(m // tm,),
        in_specs=[pl.BlockSpec((tm, d), lambda i, p1, p2: (i, 0)), pl.BlockSpec(memory_space=pl.ANY)],
        out_specs=pl.BlockSpec(memory_space=pl.ANY),
        scratch_shapes=[pltpu.SemaphoreType.DMA(())],
    )
    return pl.pallas_call(
        functools.partial(_dispatch_kernel, tm=tm),
        grid_spec=grid_spec,
        out_shape=jax.ShapeDtypeStruct(x_sorted.shape, x_sorted.dtype),
        input_output_aliases={3: 0},
        compiler_params=_params("arbitrary"),
        name="moe_dispatch",
    )(pos1, pos2, hn, x_sorted)


def _combine_kernel(pos1_ref, pos2_ref, x_ref, gate_ref, ys_ref, o_ref, buf_ref, sem, *, tm):
    base = pl.program_id(0) * tm

    def start(r, carry):
        _row_copy(ys_ref, pos1_ref[base + r], buf_ref.at[0], r, sem).start()
        _row_copy(ys_ref, pos2_ref[base + r], buf_ref.at[1], r, sem).start()
        return carry

    def wait(r, carry):
        _row_copy(ys_ref, 0, buf_ref.at[0], 0, sem).wait()
        _row_copy(ys_ref, 0, buf_ref.at[1], 0, sem).wait()
        return carry

    lax.fori_loop(0, tm, start, 0)
    lax.fori_loop(0, tm, wait, 0)
    gate = gate_ref[...]
    o_ref[...] = x_ref[...] + gate[:, 0:1] * buf_ref[0] + gate[:, 1:2] * buf_ref[1]


def moe_combine(x, gates, pos1, pos2, y_sorted):
    m, d = x.shape
    tm = _pick(m, 256, 8)
    row = lambda i, p1, p2: (i, 0)
    grid_spec = pltpu.PrefetchScalarGridSpec(
        num_scalar_prefetch=2,
        grid=(m // tm,),
        in_specs=[pl.BlockSpec((tm, d), row), pl.BlockSpec((tm, LANES), row), pl.BlockSpec(memory_space=pl.ANY)],
        out_specs=pl.BlockSpec((tm, d), row),
        scratch_shapes=[pltpu.VMEM((2, tm, d), F32), pltpu.SemaphoreType.DMA(())],
    )
    return pl.pallas_call(
        functools.partial(_combine_kernel, tm=tm),
        grid_spec=grid_spec,
        out_shape=jax.ShapeDtypeStruct((m, d), F32),
        compiler_params=_params("arbitrary"),
        name="moe_combine",
    )(pos1, pos2, x, gates, y_sorted)


def _expert_gateup_kernel(te_ref, nu_ref, x_ref, wg_ref, wu_ref, o_ref):
    @pl.when(pl.program_id(0) < nu_ref[0])
    def _():
        x = x_ref[...].astype(BF16)
        gate = _dot(x, wg_ref[0])
        up = _dot(x, wu_ref[0])
        o_ref[...] = (jax.nn.silu(gate) * up).astype(o_ref.dtype)

    @pl.when(pl.program_id(0) >= nu_ref[0])
    def _():
        o_ref[...] = jnp.zeros(o_ref.shape, o_ref.dtype)


def expert_gate_up(x_sorted, wg, wu, tile_expert, n_used, tm):
    p_rows, d = x_sorted.shape
    ff = wg.shape[2]
    tn = _pick(ff, 512, LANES)
    nj = ff // tn

    def w_map(i, j, te, nu):
        return (te[i], 0, jnp.where(i < nu[0], j, nj - 1))

    grid_spec = pltpu.PrefetchScalarGridSpec(
        num_scalar_prefetch=2,
        grid=(p_rows // tm, nj),
        in_specs=[
            pl.BlockSpec((tm, d), lambda i, j, te, nu: (jnp.minimum(i, nu[0] - 1), 0)),
            pl.BlockSpec((1, d, tn), w_map),
            pl.BlockSpec((1, d, tn), w_map),
        ],
        out_specs=pl.BlockSpec((tm, tn), lambda i, j, te, nu: (i, j)),
    )
    return pl.pallas_call(
        _expert_gateup_kernel,
        grid_spec=grid_spec,
        out_shape=jax.ShapeDtypeStruct((p_rows, ff), BF16),
        compiler_params=_params("parallel", "arbitrary"),
        name="expert_gate_up",
    )(tile_expert, n_used, x_sorted, wg, wu)


def _expert_down_kernel(te_ref, nu_ref, h_ref, wd_ref, o_ref):
    @pl.when(pl.program_id(0) < nu_ref[0])
    def _():
        o_ref[...] = _dot(h_ref[...], wd_ref[0])

    @pl.when(pl.program_id(0) >= nu_ref[0])
    def _():
        o_ref[...] = jnp.zeros(o_ref.shape, o_ref.dtype)


def expert_down(h_sorted, wd, tile_expert, n_used, tm):
    p_rows, ff = h_sorted.shape
    d = wd.shape[2]
    tn = _pick(d, 512, LANES)
    nj = d // tn
    grid_spec = pltpu.PrefetchScalarGridSpec(
        num_scalar_prefetch=2,
        grid=(p_rows // tm, nj),
        in_specs=[
            pl.BlockSpec((tm, ff), lambda i, j, te, nu: (jnp.minimum(i, nu[0] - 1), 0)),
            pl.BlockSpec((1, ff, tn), lambda i, j, te, nu: (te[i], 0, jnp.where(i < nu[0], j, nj - 1))),
        ],
        out_specs=pl.BlockSpec((tm, tn), lambda i, j, te, nu: (i, j)),
    )
    return pl.pallas_call(
        _expert_down_kernel,
        grid_spec=grid_spec,
        out_shape=jax.ShapeDtypeStruct((p_rows, d), F32),
        compiler_params=_params("parallel", "arbitrary"),
        name="expert_down",
    )(tile_expert, n_used, h_sorted, wd)


MOE_ROW_TILE = 512


def moe_ffn(xp, xs, g, w_router, wg, wu, wd):
    n_exp = w_router.shape[1]
    d = xp.shape[1]
    tm = MOE_ROW_TILE
    zeros = jnp.zeros((1, LANES), F32)
    hn_p, gate_p, route_p, cnt_p = moe_router(xp, g, w_router, zeros)
    hn_s, gate_s, route_s, cnt = moe_router(xs, g, w_router, cnt_p)

    counts = cnt[0, :n_exp].astype(jnp.int32)
    sizes = (counts + tm - 1) // tm * tm
    ends = jnp.cumsum(sizes)
    starts = ends - sizes
    n_assign = TOP_K_EXPERTS * (xp.shape[0] + xs.shape[0])
    n_tiles = (n_assign + n_exp * (tm - 1) + tm - 1) // tm
    tile_expert = jnp.minimum(jnp.sum(jnp.arange(n_tiles)[:, None] * tm >= ends[None, :], axis=1), n_exp - 1)
    tile_expert = tile_expert.astype(jnp.int32)
    n_used = (ends[-1:] // tm).astype(jnp.int32)

    def slots(route):
        return (jnp.take(starts, route[:, 0]) + route[:, 2], jnp.take(starts, route[:, 1]) + route[:, 3])

    p1_p, p2_p = slots(route_p)
    p1_s, p2_s = slots(route_s)
    x_sorted = jnp.zeros((n_tiles * tm, d), F32)
    x_sorted = moe_dispatch(hn_p, p1_p, p2_p, x_sorted)
    x_sorted = moe_dispatch(hn_s, p1_s, p2_s, x_sorted)
    h_sorted = expert_gate_up(x_sorted, wg, wu, tile_expert, n_used, tm)
    y_sorted = expert_down(h_sorted, wd, tile_expert, n_used, tm)
    return (moe_combine(xp, gate_p, p1_p, p2_p, y_sorted), moe_combine(xs, gate_s, p1_s, p2_s, y_sorted))


def _sample_attn_kernel(pt_ref, lq_ref, lk_ref, g_ref, qa_ref, qi_ref, wi_ref, kp_ref, vp_ref, ip_ref,
                        kn_ref, vn_ref, in_ref, o_ref, sc_ref, scn_ref, m_ref, l_ref, acc_ref,
                        *, n_pages, n_heads, dqk, lam_init, n_idx, d_idx):
    p = pl.program_id(1)
    hd = 2 * dqk
    page = kp_ref.shape[2]
    cols = page * n_heads
    scale = dqk ** -0.5

    q = qa_ref[0].astype(F32)
    lane = lax.broadcasted_iota(jnp.int32, q.shape, 1)
    qs = jnp.concatenate([jnp.where(lane < dqk, q, 0.0), jnp.where(lane >= dqk, q, 0.0)], axis=0)

    @pl.when(p == 0)
    def _():
        m_ref[...] = jnp.full(m_ref.shape, -jnp.inf, F32)
        l_ref[...] = jnp.zeros(l_ref.shape, F32)
        acc_ref[...] = jnp.zeros(acc_ref.shape, F32)

    k2 = kp_ref[0, 0].reshape(cols, hd).astype(BF16)
    v2 = vp_ref[0, 0].reshape(cols, hd).astype(BF16)
    s = _dot_nt(qs.astype(BF16), k2) * scale
    r_id = lax.broadcasted_iota(jnp.int32, s.shape, 0) & (n_heads - 1)
    c_id = lax.broadcasted_iota(jnp.int32, s.shape, 1) & (n_heads - 1)
    s = jnp.where(r_id == c_id, s, -jnp.inf)
    m_old = m_ref[...]
    m_new = jnp.maximum(m_old, jnp.max(s, axis=1, keepdims=True))
    alpha = jnp.exp(m_old - m_new)
    pr = jnp.exp(s - m_new)
    l_ref[...] = alpha * l_ref[...] + jnp.sum(pr, axis=1, keepdims=True)
    acc_ref[...] = alpha * acc_ref[...] + _dot(pr.astype(BF16), v2)
    m_ref[...] = m_new

    wscale = n_idx ** -0.5 * d_idx ** -0.5
    qi = qi_ref[0]
    w = wi_ref[0] * wscale
    si = jnp.maximum(_dot(qi, ip_ref[0, 0]), 0.0) * w
    sc_ref[0] = jnp.sum(si, axis=0, keepdims=True)

    @pl.when(p == n_pages - 1)
    def _():
        kn = jnp.concatenate([kn_ref[0], kn_ref[0]], axis=0)
        vn = jnp.concatenate([vn_ref[0], vn_ref[0]], axis=0)
        s_new = jnp.sum(qs * kn, axis=1, keepdims=True) * scale
        m_o = m_ref[...]
        m_n = jnp.maximum(m_o, s_new)
        a = jnp.exp(m_o - m_n)
        p_new = jnp.exp(s_new - m_n)
        l_fin = a * l_ref[...] + p_new
        o_all = (a * acc_ref[...] + p_new * vn) / l_fin
        lam = _lambda(lq_ref, lk_ref, lam_init)
        o = o_all[:n_heads] - lam * o_all[n_heads:]
        o_ref[0] = (_rms(o, g_ref[...]) * (1.0 - lam_init)).astype(o_ref.dtype)
        kin = in_ref[0]
        s_in = jnp.maximum(jnp.sum(qi.astype(F32) * kin, axis=1, keepdims=True), 0.0) * w
        scn_ref[0] = jnp.broadcast_to(jnp.sum(s_in, axis=0, keepdims=True), (1, LANES))


def sample_attention(page_table, layer, cache_k, cache_v, cache_i_t, qa, qi, wi, k_new, v_new, i_new, lq, lk, g,
                     *, n_heads, dqk, lam_init, n_idx, d_idx):
    db, n_pages = page_table.shape
    width = qa.shape[1]
    page = cache_k.shape[2]
    hd = 2 * dqk
    rows = 2 * n_heads
    assert n_heads & (n_heads - 1) == 0
    pt = page_table.reshape(-1)

    def pool5(b, p, pt_ref):
        return (layer, pt_ref[b * n_pages + p], 0, 0, 0)

    def pool4(b, p, pt_ref):
        return (layer, pt_ref[b * n_pages + p], 0, 0)

    row3 = lambda b, p, pt_ref: (b, 0, 0)
    const2 = lambda b, p, pt_ref: (0, 0)
    grid_spec = pltpu.PrefetchScalarGridSpec(
        num_scalar_prefetch=1,
        grid=(db, n_pages),
        in_specs=[
            pl.BlockSpec(lq.shape, const2),
            pl.BlockSpec(lk.shape, const2),
            pl.BlockSpec((1, hd), const2),
            pl.BlockSpec((1, n_heads, hd), row3),
            pl.BlockSpec((1, n_idx, d_idx), row3),
            pl.BlockSpec((1, n_idx, 1), row3),
            pl.BlockSpec((1, 1, page, n_heads, hd), pool5),
            pl.BlockSpec((1, 1, page, n_heads, hd), pool5),
            pl.BlockSpec((1, 1, d_idx, page), pool4),
            pl.BlockSpec((1, n_heads, hd), row3),
            pl.BlockSpec((1, n_heads, hd), row3),
            pl.BlockSpec((1, 1, d_idx), row3),
        ],
        out_specs=[
            pl.BlockSpec((1, n_heads, hd), row3),
            pl.BlockSpec((1, 1, page), lambda b, p, pt_ref: (b * n_pages + p, 0, 0)),
            pl.BlockSpec((1, 1, LANES), row3),
        ],
        scratch_shapes=[pltpu.VMEM((rows, 1), F32), pltpu.VMEM((rows, 1), F32), pltpu.VMEM((rows, hd), F32)],
    )
    out, scores, score_new = pl.pallas_call(
        functools.partial(_sample_attn_kernel, n_pages=n_pages, n_heads=n_heads, dqk=dqk, lam_init=lam_init,
                          n_idx=n_idx, d_idx=d_idx),
        grid_spec=grid_spec,
        out_shape=[jax.ShapeDtypeStruct((db, n_heads, hd), F32),
                   jax.ShapeDtypeStruct((db * n_pages, 1, page), F32),
                   jax.ShapeDtypeStruct((db, 1, LANES), F32)],
        compiler_params=_params("parallel", "arbitrary"),
        name="sample_attention",
    )(pt, lq, lk, g.reshape(1, hd), qa.reshape(db, n_heads, hd), qi.reshape(db, n_idx, d_idx),
      wi.reshape(db, n_idx, 1), cache_k, cache_v, cache_i_t,
      k_new.reshape(db, n_heads, hd), v_new.reshape(db, n_heads, hd), i_new.reshape(db, 1, d_idx))
    return out.reshape(db, width), scores.reshape(db, n_pages, page), score_new


def _count_all(mask):
    c = jnp.sum(mask.astype(jnp.int32), axis=1, keepdims=True)
    return jnp.sum(c, axis=0, keepdims=True)


def _sample_dsa_kernel(pt_ref, sc_ref, scn_ref, qb_ref, kp_ref, vp_ref, kn_ref, vn_ref, o_ref,
                       key_ref, keyn_ref, t_ref, cut_ref, m_ref, l_ref, acc_ref,
                       *, n_pages, page, n_heads, n_kv, dh, topk):
    p = pl.program_id(1)
    rep = n_heads // n_kv
    scale = dh ** -0.5
    past = n_pages * page

    @pl.when(p == 0)
    def _():
        m_ref[...] = jnp.full(m_ref.shape, -jnp.inf, F32)
        l_ref[...] = jnp.zeros(l_ref.shape, F32)
        acc_ref[...] = jnp.zeros(acc_ref.shape, F32)
        key_ref[...] = _order_key(sc_ref[0])
        keyn_ref[...] = _order_key(scn_ref[0])
        key_new = keyn_ref[:, 0:1]

        def cnt(t):
            return _count_all(key_ref[...] >= t) + (key_new >= t).astype(jnp.int32)

        t0 = jnp.full((1, 1), INT_MIN, jnp.int32)
        zero = jnp.zeros((1, 1), jnp.int32)
        t0 = jnp.where(cnt(zero) >= topk, zero, t0)

        def body(it, t):
            cand = t | (jnp.int32(1) << (30 - it))
            return jnp.where(cnt(cand) >= topk, cand, t)

        t = lax.fori_loop(0, 31, body, t0)
        t_ref[...] = t
        key = key_ref[...]
        n_gt = _count_all(key > t) + (key_new > t).astype(jnp.int32)
        need = topk - n_gt
        pos = (lax.broadcasted_iota(jnp.int32, key.shape, 0) * page
               + lax.broadcasted_iota(jnp.int32, key.shape, 1))
        n_bits = int(math.ceil(math.log2(past + 1)))

        def body2(it, x):
            cand = x | (jnp.int32(1) << (n_bits - 1 - it))
            c = _count_all((key_ref[...] == t) & (pos < cand)) + ((key_new == t) & (past < cand)).astype(jnp.int32)
            return jnp.where(c < need, cand, x)

        cut_ref[...] = lax.fori_loop(0, n_bits, body2, jnp.zeros((1, 1), jnp.int32))

    t = t_ref[...]
    cut = cut_ref[...]
    q = qb_ref[0]
    cols = page * n_kv
    kv_shift = n_kv.bit_length() - 1
    rep_shift = rep.bit_length() - 1

    key_row = key_ref[pl.ds(p, 1), :]
    pos_row = p * page + lax.broadcasted_iota(jnp.int32, (1, page), 1)
    sel = (key_row > t) | ((key_row == t) & (pos_row <= cut))
    e_t = lax.broadcasted_iota(jnp.int32, (page, cols), 0)
    e_c = lax.broadcasted_iota(jnp.int32, (page, cols), 1)
    spread = jnp.where((e_c >> kv_shift) == e_t, 1.0, 0.0).astype(BF16)
    sel_rows = jnp.broadcast_to(jnp.where(sel, 1.0, 0.0), (n_heads, page)).astype(BF16)
    sel_cols = _dot(sel_rows, spread) > 0.5
    r_id = lax.broadcasted_iota(jnp.int32, (n_heads, cols), 0)
    c_id = lax.broadcasted_iota(jnp.int32, (n_heads, cols), 1)
    keep = sel_cols & ((c_id & (n_kv - 1)) == (r_id >> rep_shift))
    s = _dot_nt(q.astype(BF16), kp_ref[0, 0].astype(BF16)) * scale
    s = jnp.where(keep, s, -jnp.inf)
    m_old = m_ref[...]
    m_new = jnp.maximum(m_old, jnp.max(s, axis=1, keepdims=True))
    m_safe = jnp.where(m_new == -jnp.inf, 0.0, m_new)
    alpha = jnp.exp(m_old - m_safe)
    pr = jnp.exp(s - m_safe)
    l_ref[...] = alpha * l_ref[...] + jnp.sum(pr, axis=1, keepdims=True)
    acc_ref[...] = alpha * acc_ref[...] + _dot(pr.astype(BF16), vp_ref[0, 0].astype(BF16))
    m_ref[...] = m_new

    @pl.when(p == n_pages - 1)
    def _():
        key_new = keyn_ref[:, 0:1]
        sel_new = (key_new > t) | ((key_new == t) & (past <= cut))
        h_id = lax.broadcasted_iota(jnp.int32, (n_heads, dh), 0) >> rep_shift
        kn = jnp.zeros((n_heads, dh), F32)
        vn = jnp.zeros((n_heads, dh), F32)
        for g in range(n_kv):
            kn = jnp.where(h_id == g, kn_ref[0, g:g + 1, :], kn)
            vn = jnp.where(h_id == g, vn_ref[0, g:g + 1, :], vn)
        s_new = jnp.sum(q.astype(F32) * kn, axis=1, keepdims=True) * scale
        s_new = jnp.where(sel_new, s_new, -jnp.inf)
        m_o = m_ref[...]
        m_n = jnp.maximum(m_o, s_new)
        m_s = jnp.where(m_n == -jnp.inf, 0.0, m_n)
        a = jnp.exp(m_o - m_s)
        p_new = jnp.exp(s_new - m_s)
        l_fin = a * l_ref[...] + p_new
        o_ref[0] = ((a * acc_ref[...] + p_new * vn) / l_fin).astype(o_ref.dtype)


def sample_dsa(page_table, layer, cache_k, cache_v, scores, score_new, qb, k_new, v_new,
               *, n_heads, n_kv, dh, topk):
    db, n_pages = page_table.shape
    cols = cache_k.shape[2]
    page = cols // n_kv
    rep = n_heads // n_kv
    assert n_kv & (n_kv - 1) == 0 and rep & (rep - 1) == 0
    pt = page_table.reshape(-1)

    def pool_map(b, p, pt_ref):
        return (layer, pt_ref[b * n_pages + p], 0, 0)

    row3 = lambda b, p, pt_ref: (b, 0, 0)
    grid_spec = pltpu.PrefetchScalarGridSpec(
        num_scalar_prefetch=1,
        grid=(db, n_pages),
        in_specs=[
            pl.BlockSpec((1, n_pages, page), row3),
            pl.BlockSpec((1, 1, LANES), row3),
            pl.BlockSpec((1, n_heads, dh), row3),
            pl.BlockSpec((1, 1, cols, dh), pool_map),
            pl.BlockSpec((1, 1, cols, dh), pool_map),
            pl.BlockSpec((1, n_kv, dh), row3),
            pl.BlockSpec((1, n_kv, dh), row3),
        ],
        out_specs=pl.BlockSpec((1, n_heads, dh), row3),
        scratch_shapes=[pltpu.VMEM((n_pages, page), jnp.int32), pltpu.VMEM((1, LANES), jnp.int32),
                        pltpu.VMEM((1, 1), jnp.int32), pltpu.VMEM((1, 1), jnp.int32),
                        pltpu.VMEM((n_heads, 1), F32), pltpu.VMEM((n_heads, 1), F32),
                        pltpu.VMEM((n_heads, dh), F32)],
    )
    out = pl.pallas_call(
        functools.partial(_sample_dsa_kernel, n_pages=n_pages, page=page, n_heads=n_heads, n_kv=n_kv, dh=dh,
                          topk=topk),
        grid_spec=grid_spec,
        out_shape=jax.ShapeDtypeStruct((db, n_heads, dh), F32),
        compiler_params=_params("parallel", "arbitrary"),
        name="sample_dsa",
    )(pt, scores, score_new, qb.reshape(db, n_heads, dh), cache_k, cache_v,
      k_new.reshape(db, n_kv, dh), v_new.reshape(db, n_kv, dh))
    return out.reshape(db, n_heads * dh)


def kernel(x_prompt, x_sample, cache_a_k, cache_a_v, cache_b_k, cache_b_v, cache_idx_k, page_table, w_in, lambda_q, lambda_k, subln_g, w_branch_a, w_branch_b, w_out, norm_mix_g, norm_ffn_g, w_dense_gate, w_dense_up, w_dense_down, w_router, w_exp_gate, w_exp_up, w_exp_down, norm_final_g):
    batch, seq, d_model = x_prompt.shape
    db, dec_seq, _ = x_sample.shape
    assert dec_seq == 1, "one new token per sample row"
    depth, n_pool, page, n_ha, a_width = cache_a_k.shape
    assert page == PAGE_SIZE
    dqk = a_width // 2
    dv = cache_a_v.shape[4]
    n_kv, dh = cache_b_k.shape[3:]
    d_idx = cache_idx_k.shape[3]
    n_hb = w_branch_b.shape[1] // dh
    n_in = w_in.shape[2]
    wa, wva, wqb, wkb = n_ha * 2 * dqk, n_ha * dv, n_hb * dh, n_kv * dh
    n_idx = (n_in - (2 * wa + wva + wqb + 2 * wkb + d_idx + 2 * d_model)) // (d_idx + 1)
    assert dv == 2 * dqk == LANES and dh == LANES and 2 * d_idx == LANES
    n_pages = page_table.shape[1]
    past_len = n_pages * page
    topk_p = min(TOPK_MAX, seq // 4)
    topk_s = min(TOPK_MAX, (past_len + dec_seq) // 4)
    n_exp = w_router.shape[2]

    offs = [0]
    for wdt in (wa, wa, wva, wqb, wkb, wkb, n_idx * d_idx, d_idx, n_idx, d_model, d_model):
        offs.append(offs[-1] + wdt)
    o_qa, o_ka, o_va, o_qb, o_kb, o_vb, o_qi, o_ki, o_wi, o_ga, o_gb, _ = offs

    pos_p = jnp.arange(seq, dtype=F32)
    pos_s = jnp.full((db,), float(past_len), F32)
    tabs = {}
    for name, pos in (("p", pos_p), ("s", pos_s)):
        tabs[name, "a"] = rope_table(pos, dqk)
        tabs[name, "b"] = rope_table(pos, dh)
        tabs[name, "i"] = rope_table(pos, d_idx, valid_lanes=d_idx)

    ck_b = cache_b_k.reshape(depth, n_pool, page * n_kv, dh)
    cv_b = cache_b_v.reshape(depth, n_pool, page * n_kv, dh)
    ci_t = jnp.swapaxes(cache_idx_k, 2, 3)

    xp = x_prompt.reshape(batch * seq, d_model)
    xs = x_sample.reshape(db * dec_seq, d_model)
    rows = {"p": [[] for _ in range(5)], "s": [[] for _ in range(5)]}

    for l in range(depth):
        lam_init = 0.8 - 0.6 * math.exp(-0.3 * l)
        lq, lk = lambda_q[l], lambda_k[l]
        mix_w = {}
        for grp, dt in (("p", BF16), ("s", F32)):
            wl = w_in[l].astype(dt)
            mix_w[grp] = dict(
                wl=wl, kiwi=jnp.pad(wl[:, o_ki:o_ga], ((0, 0), (0, LANES - d_idx - n_idx))),
                pa=w_branch_a[l].astype(dt), pb=w_branch_b[l].astype(dt), o=w_out[l].astype(dt))

        def in_proj(x, grp):
            act = BF16 if grp == "p" else F32
            wl = mix_w[grp]["wl"]
            xn = rmsnorm(x, norm_mix_g[l], act)
            ta, ha = tabs[grp, "a"]
            tb, hb = tabs[grp, "b"]
            ti, hi = tabs[grp, "i"]

            def kv_pair(w, tab=None, half=0):
                outs = project(xn, w, [F32] if act == F32 else [F32, BF16], tab, half)
                return outs[0], outs[-1]

            z = {}
            z["qa"], = project(xn, wl[:, o_qa:o_ka], [act], ta, ha)
            z["ka32"], z["ka"] = kv_pair(wl[:, o_ka:o_va], ta, ha)
            z["va32"], z["va"] = kv_pair(wl[:, o_va:o_qb])
            z["qb"], = project(xn, wl[:, o_qb:o_kb], [act], tb, hb)
            z["kb32"], z["kb"] = kv_pair(wl[:, o_kb:o_vb], tb, hb)
            z["vb32"], z["vb"] = kv_pair(wl[:, o_vb:o_qi])
            z["qi"], = project(xn, wl[:, o_qi:o_ki], [act], ta, ha)
            z["kiwi"], = project(xn, mix_w[grp]["kiwi"], [F32], ti, hi)
            z["ga"], = project(xn, wl[:, o_ga:o_gb], [F32])
            z["gb"], = project(xn, wl[:, o_gb:], [F32])
            return z

        def mix_out(x, z, oa, ob, grp):
            w = mix_w[grp]
            u = merge_branches(oa, ob, w["pa"], w["pb"], z["ga"], z["gb"])
            return matmul_residual(u, w["o"], x)

        zp = in_proj(xp, "p")
        oa = diff_attention_prompt(zp["qa"], zp["ka"], zp["va"], lq, lk, subln_g[l], batch=batch, seq=seq,
                                   n_heads=n_ha, dqk=dqk, lam_init=lam_init)
        ob = dsa_prompt(zp["qi"], zp["kiwi"], zp["qb"], zp["kb"], zp["vb"], batch=batch, seq=seq, n_idx=n_idx,
                        d_idx=d_idx, n_heads=n_hb, n_kv=n_kv, dh=dh, topk=topk_p)
        xp = mix_out(xp, zp, oa, ob, "p")

        zs = in_proj(xs, "s")
        ki_s = zs["kiwi"][:, :d_idx]
        wi_s = zs["kiwi"][:, d_idx:d_idx + n_idx]
        oa, scores, score_new = sample_attention(
            page_table, l, cache_a_k, cache_a_v, ci_t, zs["qa"], zs["qi"], wi_s, zs["ka32"], zs["va32"], ki_s,
            lq, lk, subln_g[l], n_heads=n_ha, dqk=dqk, lam_init=lam_init, n_idx=n_idx, d_idx=d_idx)
        ob = sample_dsa(page_table, l, ck_b, cv_b, scores, score_new, zs["qb"], zs["kb32"], zs["vb32"],
                        n_heads=n_hb, n_kv=n_kv, dh=dh, topk=topk_s)
        xs = mix_out(xs, zs, oa, ob, "s")

        for grp, z, bt, tt in (("p", zp, batch, seq), ("s", zs, db, dec_seq)):
            rows[grp][0].append(z["ka32"].reshape(bt, tt, n_ha, 2 * dqk))
            rows[grp][1].append(z["va32"].reshape(bt, tt, n_ha, dv))
            rows[grp][2].append(z["kb32"].reshape(bt, tt, n_kv, dh))
            rows[grp][3].append(z["vb32"].reshape(bt, tt, n_kv, dh))
            rows[grp][4].append(z["kiwi"][:, :d_idx].reshape(bt, tt, d_idx))

        i = l // 2
        if l % 2 == 0:
            hp = rmsnorm(xp, norm_ffn_g[l], BF16)
            hs = rmsnorm(xs, norm_ffn_g[l], F32)
            wg, wu, wd = (w_dense_gate[i].astype(BF16), w_dense_up[i].astype(BF16), w_dense_down[i].astype(BF16))
            xp = matmul_residual(gate_up(hp, wg, wu), wd, xp)
            xs = matmul_residual(gate_up(hs, w_dense_gate[i], w_dense_up[i]), w_dense_down[i], xs)
        else:
            xp, xs = moe_ffn(xp, xs, norm_ffn_g[l], w_router[i], w_exp_gate[i].astype(BF16),
                             w_exp_up[i].astype(BF16), w_exp_down[i].astype(BF16))

    y_prompt = rmsnorm(xp, norm_final_g, F32).reshape(batch, seq, d_model)
    y_sample = rmsnorm(xs, norm_final_g, F32).reshape(db, dec_seq, d_model)
    outs_p = [jnp.stack(r, axis=0) for r in rows["p"]]
    outs_s = [jnp.stack(r, axis=0) for r in rows["s"]]
    return (y_prompt, y_sample, *outs_p, *outs_s)
```

```python
import functools
import math

import jax
import jax.numpy as jnp
from jax import lax
from jax.experimental import pallas as pl
from jax.experimental.pallas import tpu as pltpu

LANES = 128
VMEM_LIMIT_BYTES = 56 * 1024 * 1024
ROPE_THETA = 500000.0
ROPE_FRAC = 4
RMS_EPS = 1e-6
TOPK_MAX = 256
PAGE_SIZE = 128
TOP_K_EXPERTS = 2
INT_MIN = -2 ** 31
LOG2_E = math.log2(math.e)
F32 = jnp.float32
BF16 = jnp.bfloat16


def _pick(n, pref, mult):
    if n <= pref:
        return n
    t = (pref // mult) * mult
    while t >= mult:
        if n % t == 0:
            return t
        t -= mult
    return n


def _params(*sem):
    return pltpu.CompilerParams(dimension_semantics=sem, vmem_limit_bytes=VMEM_LIMIT_BYTES)


def _precision(a, b):
    return lax.Precision.HIGHEST if a.dtype == F32 and b.dtype == F32 else None


def _dot(a, b):
    return jnp.dot(a, b, preferred_element_type=F32, precision=_precision(a, b))


def _dot_nt(a, b):
    return lax.dot_general(a, b, (((1,), (1,)), ((), ())), preferred_element_type=F32,
                           precision=_precision(a, b))


def _rms(x, g):
    return x * lax.rsqrt(jnp.mean(x * x, axis=-1, keepdims=True) + RMS_EPS) * g


def _rmsnorm_kernel(x_ref, g_ref, o_ref):
    o_ref[...] = _rms(x_ref[...], g_ref[...]).astype(o_ref.dtype)


def rmsnorm(x, g, out_dtype):
    m, d = x.shape
    tm = _pick(m, 512, 8)
    return pl.pallas_call(
        _rmsnorm_kernel,
        grid=(m // tm,),
        in_specs=[pl.BlockSpec((tm, d), lambda i: (i, 0)), pl.BlockSpec((1, d), lambda i: (0, 0))],
        out_specs=pl.BlockSpec((tm, d), lambda i: (i, 0)),
        out_shape=jax.ShapeDtypeStruct((m, d), out_dtype),
        compiler_params=_params("parallel"),
        name="rmsnorm",
    )(x, g.reshape(1, d))


def _proj_kernel(*refs, half, tn):
    if half:
        x_ref, w_ref, tab_ref, *o_refs = refs
    else:
        x_ref, w_ref, *o_refs = refs
    z = _dot(x_ref[...], w_ref[...])
    if not half:
        for o in o_refs:
            o[...] = z.astype(o.dtype)
        return
    c, s_lo, s_hi = tab_ref[0], tab_ref[1], tab_ref[2]
    for g in range(tn // LANES):
        zg = z[:, g * LANES:(g + 1) * LANES]
        r = zg * c + pltpu.roll(zg, LANES - half, 1) * s_lo + pltpu.roll(zg, half, 1) * s_hi
        for o in o_refs:
            o[:, g * LANES:(g + 1) * LANES] = r.astype(o.dtype)


def project(xn, w, out_dtypes, tab=None, half=0, tn_pref=512):
    m, k = xn.shape
    n = w.shape[1]
    p_rows = m if tab is None else tab.shape[1]
    assert m % p_rows == 0
    tm = _pick(p_rows, 1024, 8)
    tn = _pick(n, tn_pref, LANES)
    in_specs = [pl.BlockSpec((tm, k), lambda i, j: (i, 0)), pl.BlockSpec((k, tn), lambda i, j: (0, j))]
    args = [xn, w]
    if tab is not None:
        nblk = p_rows // tm
        in_specs.append(pl.BlockSpec((3, tm, LANES), lambda i, j: (0, i % nblk, 0)))
        args.append(tab)
    outs = pl.pallas_call(
        functools.partial(_proj_kernel, half=half if tab is not None else 0, tn=tn),
        grid=(m // tm, n // tn),
        in_specs=in_specs,
        out_specs=[pl.BlockSpec((tm, tn), lambda i, j: (i, j)) for _ in out_dtypes],
        out_shape=[jax.ShapeDtypeStruct((m, n), dt) for dt in out_dtypes],
        compiler_params=_params("parallel", "arbitrary"),
        name="in_proj",
    )(*args)
    return outs


def rope_table(pos, head_dim, valid_lanes=LANES):
    rot = head_dim // ROPE_FRAC
    half = rot // 2
    inv_freq = jnp.power(ROPE_THETA, -jnp.arange(half, dtype=F32) * 2.0 / rot)
    ang = pos[:, None] * inv_freq[None, :]
    cos, sin = jnp.cos(ang), jnp.sin(ang)
    n = pos.shape[0]
    ones = jnp.ones((n, head_dim - rot), F32)
    zeros_h = jnp.zeros((n, half), F32)
    zeros_r = jnp.zeros((n, head_dim - rot), F32)
    c = jnp.concatenate([cos, cos, ones], axis=1)
    s_lo = jnp.concatenate([-sin, zeros_h, zeros_r], axis=1)
    s_hi = jnp.concatenate([zeros_h, sin, zeros_r], axis=1)
    reps = LANES // head_dim
    tabs = [jnp.tile(t, (1, reps)) for t in (c, s_lo, s_hi)]
    if valid_lanes < LANES:
        lane = jnp.arange(LANES)[None, :]
        tabs = [jnp.where(lane < valid_lanes, tabs[0], 1.0), jnp.where(lane < valid_lanes, tabs[1], 0.0),
                jnp.where(lane < valid_lanes, tabs[2], 0.0)]
    return jnp.stack(tabs, axis=0), half


def _lambda(lq_ref, lk_ref, lam_init):
    lq = lq_ref[...].astype(F32)
    lk = lk_ref[...].astype(F32)
    prod = lq * lk
    return (jnp.exp(jnp.sum(prod[0:1, :], axis=1, keepdims=True))
            - jnp.exp(jnp.sum(prod[1:2, :], axis=1, keepdims=True)) + lam_init)


def _diffattn_kernel(lq_ref, lk_ref, g_ref, q_ref, k_ref, vt_ref, o_ref, m_ref, l_ref, acc_ref,
                     *, tq, dqk, lam_init):
    i = pl.program_id(2)
    c_exp = dqk ** -0.5 * LOG2_E
    q = q_ref[...].astype(F32)
    lane = lax.broadcasted_iota(jnp.int32, q.shape, 1)
    qs = jnp.concatenate([jnp.where(lane < dqk, q, 0.0), jnp.where(lane >= dqk, q, 0.0)], axis=0)
    qs = qs.astype(q_ref.dtype)
    m_ref[...] = jnp.full(m_ref.shape, -jnp.inf, F32)
    l_ref[...] = jnp.zeros(l_ref.shape, F32)
    acc_ref[...] = jnp.zeros(acc_ref.shape, F32)

    def step(j, masked):
        start = pl.multiple_of(j * tq, tq)
        kt = k_ref[pl.ds(start, tq), :]
        vt = vt_ref[:, pl.ds(start, tq)]
        s = _dot_nt(kt, qs) * c_exp
        if masked:
            key = lax.broadcasted_iota(jnp.int32, s.shape, 0)
            qry = lax.broadcasted_iota(jnp.int32, s.shape, 1) & (tq - 1)
            s = jnp.where(key <= qry, s, -jnp.inf)
        m_old = m_ref[...]
        m_new = jnp.maximum(m_old, jnp.max(s, axis=0, keepdims=True))
        alpha = jnp.exp2(m_old - m_new)
        p = jnp.exp2(s - m_new)
        l_ref[...] = alpha * l_ref[...] + jnp.sum(p, axis=0, keepdims=True)
        acc_ref[...] = alpha * acc_ref[...] + _dot(vt, p.astype(vt.dtype))
        m_ref[...] = m_new

    def body(j, carry):
        step(j, False)
        return carry

    lax.fori_loop(0, i, body, 0)
    step(i, True)

    lam = _lambda(lq_ref, lk_ref, lam_init)
    o_all = acc_ref[...] / l_ref[...]
    o = o_all[:, :tq] - lam * o_all[:, tq:]
    inv = lax.rsqrt(jnp.mean(o * o, axis=0, keepdims=True) + RMS_EPS)
    y = o * inv * g_ref[...] * (1.0 - lam_init)
    o_ref[...] = y.T.astype(o_ref.dtype)


def diff_attention_prompt(q, k, v_t, lq, lk, g, *, batch, seq, n_heads, dqk, lam_init):
    m, width = q.shape
    hd = width // n_heads
    tq = _pick(seq, 512, LANES)
    assert tq & (tq - 1) == 0
    nq = seq // tq
    return pl.pallas_call(
        functools.partial(_diffattn_kernel, tq=tq, dqk=dqk, lam_init=lam_init),
        grid=(batch, n_heads, nq),
        in_specs=[
            pl.BlockSpec(lq.shape, lambda b, h, i: (0, 0)),
            pl.BlockSpec(lk.shape, lambda b, h, i: (0, 0)),
            pl.BlockSpec((hd, 1), lambda b, h, i: (0, 0)),
            pl.BlockSpec((tq, hd), lambda b, h, i: (b * nq + i, h)),
            pl.BlockSpec((seq, hd), lambda b, h, i: (b, h)),
            pl.BlockSpec((hd, seq), lambda b, h, i: (h, b)),
        ],
        out_specs=pl.BlockSpec((tq, hd), lambda b, h, i: (b * nq + i, h)),
        out_shape=jax.ShapeDtypeStruct((m, width), BF16),
        scratch_shapes=[pltpu.VMEM((1, 2 * tq), F32), pltpu.VMEM((1, 2 * tq), F32), pltpu.VMEM((hd, 2 * tq), F32)],
        compiler_params=_params("parallel", "parallel", "arbitrary"),
        name="diff_attn_prompt",
    )(lq, lk, g.reshape(hd, 1), q, k, v_t)


def _order_key(score):
    score = jnp.where(score == 0.0, 0.0, score)
    bits = lax.bitcast_convert_type(score, jnp.int32)
    return jnp.where(bits < 0, bits ^ jnp.int32(0x7FFFFFFF), bits)


def _count(mask):
    return jnp.sum(mask.astype(jnp.int32), axis=1, keepdims=True)


def _kth_largest_key(load_keys, rows, k, count_fn):
    t0 = jnp.full((rows, 1), INT_MIN, jnp.int32)
    zero = jnp.zeros((rows, 1), jnp.int32)
    t0 = jnp.where(count_fn(load_keys() >= zero) >= k, zero, t0)

    def body(it, t):
        cand = t | (jnp.int32(1) << (30 - it))
        return jnp.where(count_fn(load_keys() >= cand) >= k, cand, t)

    return lax.fori_loop(0, 31, body, t0)


def _tie_cut(load_eq_pos, rows, need, n_bits, count_fn):
    def body(it, x):
        cand = x | (jnp.int32(1) << (n_bits - 1 - it))
        eq, pos = load_eq_pos()
        return jnp.where(count_fn(eq & (pos < cand)) < need, cand, x)

    return lax.fori_loop(0, n_bits, body, jnp.zeros((rows, 1), jnp.int32))


def _dsa_prompt_kernel(qi_ref, kiwi_k_ref, kiwi_q_ref, qb_ref, kb_ref, vb_ref, o_ref,
                       kdup_ref, key_ref, bias_ref, cut_ref,
                       *, tq, seq, extents, n_idx, d_idx, n_heads, n_kv, dh, topk):
    i = pl.program_id(1)

    @pl.when(i == 0)
    def _():
        kf = kiwi_k_ref[...]
        lane = lax.broadcasted_iota(jnp.int32, kf.shape, 1)
        klo = jnp.where(lane < d_idx, kf, 0.0)
        kdup_ref[...] = (klo + pltpu.roll(klo, d_idx, 1)).astype(kdup_ref.dtype)

    wscale = n_idx ** -0.5 * d_idx ** -0.5
    scale = dh ** -0.5
    rep = n_heads // n_kv

    def run(ncol):
        kd = kdup_ref[:ncol, :]
        w_all = kiwi_q_ref[...]
        score = jnp.zeros((tq, ncol), F32)
        for h in range(n_idx):
            grp = qi_ref[:, (h // 2) * LANES:(h // 2 + 1) * LANES].astype(F32)
            lane = lax.broadcasted_iota(jnp.int32, grp.shape, 1)
            keep = (lane < d_idx) if h % 2 == 0 else (lane >= d_idx)
            qh = jnp.where(keep, grp, 0.0).astype(qi_ref.dtype)
            w = w_all[:, d_idx + h:d_idx + h + 1] * wscale
            score = score + jnp.maximum(_dot_nt(qh, kd), 0.0) * w

        row = i * tq + lax.broadcasted_iota(jnp.int32, (tq, ncol), 0)
        col = lax.broadcasted_iota(jnp.int32, (tq, ncol), 1)
        causal = col <= row
        key_ref[:, :ncol] = jnp.where(causal, _order_key(score), INT_MIN)

        t = _kth_largest_key(lambda: key_ref[:, :ncol], tq, topk, _count)
        key = key_ref[:, :ncol]
        eq = key == t
        need = topk - _count(key > t)
        tie = (_count(eq) > need) & (t > INT_MIN)
        cut_ref[...] = jnp.full((tq, 1), ncol, jnp.int32)

        @pl.when(jnp.max(tie.astype(jnp.int32)) > 0)
        def _():
            def load():
                return key_ref[:, :ncol] == t, lax.broadcasted_iota(jnp.int32, (tq, ncol), 1)
            cut_ref[...] = _tie_cut(load, tq, need, int(math.log2(seq)), _count)

        sel = ((key > t) | (eq & (col <= cut_ref[...]))) & causal
        bias_ref[:, :ncol] = jnp.where(sel, 0.0, -jnp.inf)

        for h in range(n_heads):
            g = h // rep
            s = (_dot_nt(qb_ref[:, h * dh:(h + 1) * dh], kb_ref[:ncol, g * dh:(g + 1) * dh]) * scale
                 + bias_ref[:, :ncol])
            p = jnp.exp(s - jnp.max(s, axis=1, keepdims=True))
            l = jnp.sum(p, axis=1, keepdims=True)
            o = _dot(p.astype(vb_ref.dtype), vb_ref[:ncol, g * dh:(g + 1) * dh]) / l
            o_ref[:, h * dh:(h + 1) * dh] = o.astype(o_ref.dtype)

    lo = 0
    for ncol in extents:
        @pl.when(((i + 1) * tq > lo) & ((i + 1) * tq <= ncol))
        def _(ncol=ncol):
            run(ncol)
        lo = ncol


def dsa_prompt(qi, kiwi, qb, kb, vb, *, batch, seq, n_idx, d_idx, n_heads, n_kv, dh, topk):
    m = qi.shape[0]
    assert 2 * d_idx == LANES and seq & (seq - 1) == 0
    tq = _pick(seq, 256, LANES)
    nq = seq // tq
    n_ext = min(4, nq)
    extents = tuple(seq * (c + 1) // n_ext for c in range(n_ext))
    return pl.pallas_call(
        functools.partial(_dsa_prompt_kernel, tq=tq, seq=seq, extents=extents, n_idx=n_idx, d_idx=d_idx,
                          n_heads=n_heads,
                          n_kv=n_kv, dh=dh, topk=topk),
        grid=(batch, nq),
        in_specs=[
            pl.BlockSpec((tq, qi.shape[1]), lambda b, i: (b * nq + i, 0)),
            pl.BlockSpec((seq, LANES), lambda b, i: (b, 0)),
            pl.BlockSpec((tq, LANES), lambda b, i: (b * nq + i, 0)),
            pl.BlockSpec((tq, qb.shape[1]), lambda b, i: (b * nq + i, 0)),
            pl.BlockSpec((seq, kb.shape[1]), lambda b, i: (b, 0)),
            pl.BlockSpec((seq, vb.shape[1]), lambda b, i: (b, 0)),
        ],
        out_specs=pl.BlockSpec((tq, qb.shape[1]), lambda b, i: (b * nq + i, 0)),
        out_shape=jax.ShapeDtypeStruct((m, qb.shape[1]), BF16),
        scratch_shapes=[pltpu.VMEM((seq, LANES), BF16), pltpu.VMEM((tq, seq), jnp.int32),
                        pltpu.VMEM((tq, seq), F32), pltpu.VMEM((tq, 1), jnp.int32)],
        compiler_params=_params("parallel", "arbitrary"),
        name="dsa_prompt",
    )(qi, kiwi, kiwi, qb, kb, vb)


def _merge_kernel(oa_ref, ob_ref, wa_ref, wb_ref, ga_ref, gb_ref, o_ref):
    ya = _dot(oa_ref[...], wa_ref[...])
    yb = _dot(ob_ref[...], wb_ref[...])
    u = jax.nn.sigmoid(ga_ref[...]) * ya + jax.nn.sigmoid(gb_ref[...]) * yb
    o_ref[...] = u.astype(o_ref.dtype)


def merge_branches(oa, ob, w_pa, w_pb, ga, gb):
    m, ka = oa.shape
    kb = ob.shape[1]
    n = w_pa.shape[1]
    tm = _pick(m, 1024, 8)
    tn = _pick(n, 512, LANES)
    return pl.pallas_call(
        _merge_kernel,
        grid=(m // tm, n // tn),
        in_specs=[
            pl.BlockSpec((tm, ka), lambda i, j: (i, 0)),
            pl.BlockSpec((tm, kb), lambda i, j: (i, 0)),
            pl.BlockSpec((ka, tn), lambda i, j: (0, j)),
            pl.BlockSpec((kb, tn), lambda i, j: (0, j)),
            pl.BlockSpec((tm, tn), lambda i, j: (i, j)),
            pl.BlockSpec((tm, tn), lambda i, j: (i, j)),
        ],
        out_specs=pl.BlockSpec((tm, tn), lambda i, j: (i, j)),
        out_shape=jax.ShapeDtypeStruct((m, n), oa.dtype),
        compiler_params=_params("parallel", "arbitrary"),
        name="merge_branches",
    )(oa, ob, w_pa, w_pb, ga, gb)


def _matmul_res_kernel(a_ref, w_ref, r_ref, o_ref):
    o_ref[...] = r_ref[...] + _dot(a_ref[...], w_ref[...])


def matmul_residual(a, w, res):
    m, k = a.shape
    n = w.shape[1]
    tm = _pick(m, 512, 8)
    tn = _pick(n, 512, LANES)
    return pl.pallas_call(
        _matmul_res_kernel,
        grid=(m // tm, n // tn),
        in_specs=[
            pl.BlockSpec((tm, k), lambda i, j: (i, 0)),
            pl.BlockSpec((k, tn), lambda i, j: (0, j)),
            pl.BlockSpec((tm, tn), lambda i, j: (i, j)),
        ],
        out_specs=pl.BlockSpec((tm, tn), lambda i, j: (i, j)),
        out_shape=jax.ShapeDtypeStruct((m, n), F32),
        compiler_params=_params("parallel", "arbitrary"),
        name="matmul_residual",
    )(a, w, res)


def _gateup_kernel(x_ref, wg_ref, wu_ref, o_ref):
    x = x_ref[...]
    gate = _dot(x, wg_ref[...])
    up = _dot(x, wu_ref[...])
    o_ref[...] = (jax.nn.silu(gate) * up).astype(o_ref.dtype)


def gate_up(xn, wg, wu):
    m, k = xn.shape
    n = wg.shape[1]
    tm = _pick(m, 1024, 8)
    tn = _pick(n, 512, LANES)
    return pl.pallas_call(
        _gateup_kernel,
        grid=(m // tm, n // tn),
        in_specs=[
            pl.BlockSpec((tm, k), lambda i, j: (i, 0)),
            pl.BlockSpec((k, tn), lambda i, j: (0, j)),
            pl.BlockSpec((k, tn), lambda i, j: (0, j)),
        ],
        out_specs=pl.BlockSpec((tm, tn), lambda i, j: (i, j)),
        out_shape=jax.ShapeDtypeStruct((m, n), xn.dtype),
        compiler_params=_params("parallel", "arbitrary"),
        name="ffn_gate_up",
    )(xn, wg, wu)


def _router_kernel(x_ref, g_ref, wr_ref, base_ref, hn_ref, gate_ref, route_ref, cnt_ref, run_ref, *, n_exp):
    @pl.when(pl.program_id(0) == 0)
    def _():
        run_ref[...] = base_ref[...]

    hn = _rms(x_ref[...], g_ref[...])
    hn_ref[...] = hn
    tm = hn.shape[0]
    lane = lax.broadcasted_iota(jnp.int32, (tm, LANES), 1)
    logits = jnp.full((tm, LANES), -jnp.inf, F32)
    for e in range(n_exp):
        le = jnp.sum(hn * wr_ref[e:e + 1, :], axis=1, keepdims=True)
        logits = jnp.where(lane == e, le, logits)
    v1 = jnp.max(logits, axis=1, keepdims=True)
    i1 = jnp.min(jnp.where(logits == v1, lane, LANES), axis=1, keepdims=True)
    rest = jnp.where(lane == i1, -jnp.inf, logits)
    v2 = jnp.max(rest, axis=1, keepdims=True)
    i2 = jnp.min(jnp.where(rest == v2, lane, LANES), axis=1, keepdims=True)
    e2 = jnp.exp(v2 - v1)
    den = 1.0 + e2
    gate_ref[...] = jnp.where(lane == 0, 1.0 / den, jnp.where(lane == 1, e2 / den, 0.0))

    onehot = jnp.where((lane == i1) | (lane == i2), 1.0, 0.0)
    r = lax.broadcasted_iota(jnp.int32, (tm, tm), 0)
    c = lax.broadcasted_iota(jnp.int32, (tm, tm), 1)
    before = _dot(jnp.where(c < r, 1.0, 0.0).astype(BF16), onehot.astype(BF16)) + run_ref[...]
    r1 = jnp.sum(jnp.where(lane == i1, before, 0.0), axis=1, keepdims=True).astype(jnp.int32)
    r2 = jnp.sum(jnp.where(lane == i2, before, 0.0), axis=1, keepdims=True).astype(jnp.int32)
    route_ref[...] = jnp.where(lane == 0, i1, jnp.where(lane == 1, i2, jnp.where(lane == 2, r1,
                               jnp.where(lane == 3, r2, 0))))
    run_ref[...] = run_ref[...] + jnp.sum(onehot, axis=0, keepdims=True)
    cnt_ref[...] = run_ref[...]


def moe_router(x, g, w_router, base_counts):
    m, d = x.shape
    n_exp = w_router.shape[1]
    assert TOP_K_EXPERTS == 2 and n_exp <= LANES
    tm = _pick(m, 256, 8)
    row = lambda i: (i, 0)
    fixed = lambda i: (0, 0)
    return pl.pallas_call(
        functools.partial(_router_kernel, n_exp=n_exp),
        grid=(m // tm,),
        in_specs=[
            pl.BlockSpec((tm, d), row),
            pl.BlockSpec((1, d), fixed),
            pl.BlockSpec((n_exp, d), fixed),
            pl.BlockSpec((1, LANES), fixed),
        ],
        out_specs=[pl.BlockSpec((tm, d), row), pl.BlockSpec((tm, LANES), row), pl.BlockSpec((tm, LANES), row),
                   pl.BlockSpec((1, LANES), fixed)],
        out_shape=[jax.ShapeDtypeStruct((m, d), F32), jax.ShapeDtypeStruct((m, LANES), F32),
                   jax.ShapeDtypeStruct((m, LANES), jnp.int32), jax.ShapeDtypeStruct((1, LANES), F32)],
        scratch_shapes=[pltpu.VMEM((1, LANES), F32)],
        compiler_params=_params("arbitrary"),
        name="moe_router",
    )(x, g.reshape(1, d), w_router.T, base_counts)


def _row_copy(src_ref, src_row, dst_ref, dst_row, sem):
    return pltpu.make_async_copy(src_ref.at[pl.ds(src_row, 1)], dst_ref.at[pl.ds(dst_row, 1)], sem)


def _dispatch_kernel(pos1_ref, pos2_ref, x_ref, xs_in_ref, xs_ref, sem, *, tm):
    del xs_in_ref
    base = pl.program_id(0) * tm

    def start(r, carry):
        _row_copy(x_ref, r, xs_ref, pos1_ref[base + r], sem).start()
        _row_copy(x_ref, r, xs_ref, pos2_ref[base + r], sem).start()
        return carry

    def wait(r, carry):
        _row_copy(x_ref, 0, xs_ref, 0, sem).wait()
        _row_copy(x_ref, 0, xs_ref, 0, sem).wait()
        return carry

    lax.fori_loop(0, tm, start, 0)
    lax.fori_loop(0, tm, wait, 0)


def moe_dispatch(hn, pos1, pos2, x_sorted):
    m, d = hn.shape
    tm = _pick(m, 256, 8)
    grid_spec = pltpu.PrefetchScalarGridSpec(
        num_scalar_prefetch=2,
        grid=(m // tm,),
        in_specs=[pl.BlockSpec((tm, d), lambda i, p1, p2: (i, 0)), pl.BlockSpec(memory_space=pl.ANY)],
        out_specs=pl.BlockSpec(memory_space=pl.ANY),
        scratch_shapes=[pltpu.SemaphoreType.DMA(())],
    )
    return pl.pallas_call(
        functools.partial(_dispatch_kernel, tm=tm),
        grid_spec=grid_spec,
        out_shape=jax.ShapeDtypeStruct(x_sorted.shape, x_sorted.dtype),
        input_output_aliases={3: 0},
        compiler_params=_params("arbitrary"),
        name="moe_dispatch",
    )(pos1, pos2, hn, x_sorted)


def _combine_kernel(pos1_ref, pos2_ref, x_ref, gate_ref, ys_ref, o_ref, buf_ref, sem, *, tm):
    base = pl.program_id(0) * tm

    def start(r, carry):
        _row_copy(ys_ref, pos1_ref[base + r], buf_ref.at[0], r, sem).start()
        _row_copy(ys_ref, pos2_ref[base + r], buf_ref.at[1], r, sem).start()
        return carry

    def wait(r, carry):
        _row_copy(ys_ref, 0, buf_ref.at[0], 0, sem).wait()
        _row_copy(ys_ref, 0, buf_ref.at[1], 0, sem).wait()
        return carry

    lax.fori_loop(0, tm, start, 0)
    lax.fori_loop(0, tm, wait, 0)
    gate = gate_ref[...]
    o_ref[...] = x_ref[...] + gate[:, 0:1] * buf_ref[0] + gate[:, 1:2] * buf_ref[1]


def moe_combine(x, gates, pos1, pos2, y_sorted):
    m, d = x.shape
    tm = _pick(m, 256, 8)
    row = lambda i, p1, p2: (i, 0)
    grid_spec = pltpu.PrefetchScalarGridSpec(
        num_scalar_prefetch=2,
        grid=(m // tm,),
        in_specs=[pl.BlockSpec((tm, d), row), pl.BlockSpec((tm, LANES), row), pl.BlockSpec(memory_space=pl.ANY)],
        out_specs=pl.BlockSpec((tm, d), row),
        scratch_shapes=[pltpu.VMEM((2, tm, d), F32), pltpu.SemaphoreType.DMA(())],
    )
    return pl.pallas_call(
        functools.partial(_combine_kernel, tm=tm),
        grid_spec=grid_spec,
        out_shape=jax.ShapeDtypeStruct((m, d), F32),
        compiler_params=_params("arbitrary"),
        name="moe_combine",
    )(pos1, pos2, x, gates, y_sorted)


def _expert_gateup_kernel(te_ref, nu_ref, x_ref, wg_ref, wu_ref, o_ref):
    @pl.when(pl.program_id(0) < nu_ref[0])
    def _():
        x = x_ref[...].astype(BF16)
        gate = _dot(x, wg_ref[0])
        up = _dot(x, wu_ref[0])
        o_ref[...] = (jax.nn.silu(gate) * up).astype(o_ref.dtype)

    @pl.when(pl.program_id(0) >= nu_ref[0])
    def _():
        o_ref[...] = jnp.zeros(o_ref.shape, o_ref.dtype)


def expert_gate_up(x_sorted, wg, wu, tile_expert, n_used, tm):
    p_rows, d = x_sorted.shape
    ff = wg.shape[2]
    tn = _pick(ff, 512, LANES)
    nj = ff // tn

    def w_map(i, j, te, nu):
        return (te[i], 0, jnp.where(i < nu[0], j, nj - 1))

    grid_spec = pltpu.PrefetchScalarGridSpec(
        num_scalar_prefetch=2,
        grid=(p_rows // tm, nj),
        in_specs=[
            pl.BlockSpec((tm, d), lambda i, j, te, nu: (jnp.minimum(i, nu[0] - 1), 0)),
            pl.BlockSpec((1, d, tn), w_map),
            pl.BlockSpec((1, d, tn), w_map),
        ],
        out_specs=pl.BlockSpec((tm, tn), lambda i, j, te, nu: (i, j)),
    )
    return pl.pallas_call(
        _expert_gateup_kernel,
        grid_spec=grid_spec,
        out_shape=jax.ShapeDtypeStruct((p_rows, ff), BF16),
        compiler_params=_params("parallel", "arbitrary"),
        name="expert_gate_up",
    )(tile_expert, n_used, x_sorted, wg, wu)


def _expert_down_kernel(te_ref, nu_ref, h_ref, wd_ref, o_ref):
    @pl.when(pl.program_id(0) < nu_ref[0])
    def _():
        o_ref[...] = _dot(h_ref[...], wd_ref[0])

    @pl.when(pl.program_id(0) >= nu_ref[0])
    def _():
        o_ref[...] = jnp.zeros(o_ref.shape, o_ref.dtype)


def expert_down(h_sorted, wd, tile_expert, n_used, tm):
    p_rows, ff = h_sorted.shape
    d = wd.shape[2]
    tn = _pick(d, 512, LANES)
    nj = d // tn
    grid_spec = pltpu.PrefetchScalarGridSpec(
        num_scalar_prefetch=2,
        grid=(p_rows // tm, nj),
        in_specs=[
            pl.BlockSpec((tm, ff), lambda i, j, te, nu: (jnp.minimum(i, nu[0] - 1), 0)),
            pl.BlockSpec((1, ff, tn), lambda i, j, te, nu: (te[i], 0, jnp.where(i < nu[0], j, nj - 1))),
        ],
        out_specs=pl.BlockSpec((tm, tn), lambda i, j, te, nu: (i, j)),
    )
    return pl.pallas_call(
        _expert_down_kernel,
        grid_spec=grid_spec,
        out_shape=jax.ShapeDtypeStruct((p_rows, d), F32),
        compiler_params=_params("parallel", "arbitrary"),
        name="expert_down",
    )(tile_expert, n_used, h_sorted, wd)


MOE_ROW_TILE = 512
SAMPLE_ATTN_PAGES = 8
SAMPLE_DSA_PAGES = 16


def moe_ffn(xp, xs, g, w_router, wg, wu, wd):
    n_exp = w_router.shape[1]
    d = xp.shape[1]
    tm = MOE_ROW_TILE
    zeros = jnp.zeros((1, LANES), F32)
    hn_p, gate_p, route_p, cnt_p = moe_router(xp, g, w_router, zeros)
    hn_s, gate_s, route_s, cnt = moe_router(xs, g, w_router, cnt_p)

    counts = cnt[0, :n_exp].astype(jnp.int32)
    sizes = (counts + tm - 1) // tm * tm
    ends = jnp.cumsum(sizes)
    starts = ends - sizes
    n_assign = TOP_K_EXPERTS * (xp.shape[0] + xs.shape[0])
    n_tiles = (n_assign + n_exp * (tm - 1) + tm - 1) // tm
    tile_expert = jnp.minimum(jnp.sum(jnp.arange(n_tiles)[:, None] * tm >= ends[None, :], axis=1), n_exp - 1)
    tile_expert = tile_expert.astype(jnp.int32)
    n_used = (ends[-1:] // tm).astype(jnp.int32)

    def slots(route):
        return (jnp.take(starts, route[:, 0]) + route[:, 2], jnp.take(starts, route[:, 1]) + route[:, 3])

    p1_p, p2_p = slots(route_p)
    p1_s, p2_s = slots(route_s)
    x_sorted = jnp.zeros((n_tiles * tm, d), F32)
    x_sorted = moe_dispatch(hn_p, p1_p, p2_p, x_sorted)
    x_sorted = moe_dispatch(hn_s, p1_s, p2_s, x_sorted)
    h_sorted = expert_gate_up(x_sorted, wg, wu, tile_expert, n_used, tm)
    y_sorted = expert_down(h_sorted, wd, tile_expert, n_used, tm)
    return (moe_combine(xp, gate_p, p1_p, p2_p, y_sorted), moe_combine(xs, gate_s, p1_s, p2_s, y_sorted))


def _sample_attn_kernel(pt_ref, lq_ref, lk_ref, g_ref, qa_ref, qi_ref, wi_ref, *rest,
                        n_steps, group, n_heads, dqk, lam_init, n_idx, d_idx):
    kp_refs, vp_refs, ip_refs = rest[:group], rest[group:2 * group], rest[2 * group:3 * group]
    kn_ref, vn_ref, in_ref, o_ref, sc_ref, scn_ref, m_ref, l_ref, acc_ref = rest[3 * group:]
    p = pl.program_id(1)
    hd = 2 * dqk
    page = kp_refs[0].shape[2]
    cols = page * n_heads
    scale = dqk ** -0.5

    q = qa_ref[0].astype(F32)
    lane = lax.broadcasted_iota(jnp.int32, q.shape, 1)
    qs = jnp.concatenate([jnp.where(lane < dqk, q, 0.0), jnp.where(lane >= dqk, q, 0.0)], axis=0)

    @pl.when(p == 0)
    def _():
        m_ref[...] = jnp.full(m_ref.shape, -jnp.inf, F32)
        l_ref[...] = jnp.zeros(l_ref.shape, F32)
        acc_ref[...] = jnp.zeros(acc_ref.shape, F32)

    k2 = jnp.concatenate([r[0, 0].reshape(cols, hd).astype(BF16) for r in kp_refs], axis=0)
    v2 = jnp.concatenate([r[0, 0].reshape(cols, hd).astype(BF16) for r in vp_refs], axis=0)
    s = _dot_nt(qs.astype(BF16), k2) * scale
    r_id = lax.broadcasted_iota(jnp.int32, s.shape, 0) & (n_heads - 1)
    c_id = lax.broadcasted_iota(jnp.int32, s.shape, 1) & (n_heads - 1)
    s = jnp.where(r_id == c_id, s, -jnp.inf)
    m_old = m_ref[...]
    m_new = jnp.maximum(m_old, jnp.max(s, axis=1, keepdims=True))
    alpha = jnp.exp(m_old - m_new)
    pr = jnp.exp(s - m_new)
    l_ref[...] = alpha * l_ref[...] + jnp.sum(pr, axis=1, keepdims=True)
    acc_ref[...] = alpha * acc_ref[...] + _dot(pr.astype(BF16), v2)
    m_ref[...] = m_new

    wscale = n_idx ** -0.5 * d_idx ** -0.5
    qi = qi_ref[0]
    w = wi_ref[0] * wscale
    for k, ip_ref in enumerate(ip_refs):
        si = jnp.maximum(_dot(qi, ip_ref[0, 0]), 0.0) * w
        sc_ref[0, k:k + 1, :] = jnp.sum(si, axis=0, keepdims=True)

    @pl.when(p == n_steps - 1)
    def _():
        kn = jnp.concatenate([kn_ref[0], kn_ref[0]], axis=0)
        vn = jnp.concatenate([vn_ref[0], vn_ref[0]], axis=0)
        s_new = jnp.sum(qs * kn, axis=1, keepdims=True) * scale
        m_o = m_ref[...]
        m_n = jnp.maximum(m_o, s_new)
        a = jnp.exp(m_o - m_n)
        p_new = jnp.exp(s_new - m_n)
        l_fin = a * l_ref[...] + p_new
        o_all = (a * acc_ref[...] + p_new * vn) / l_fin
        lam = _lambda(lq_ref, lk_ref, lam_init)
        o = o_all[:n_heads] - lam * o_all[n_heads:]
        o_ref[0] = (_rms(o, g_ref[...]) * (1.0 - lam_init)).astype(o_ref.dtype)
        kin = in_ref[0]
        s_in = jnp.maximum(jnp.sum(qi.astype(F32) * kin, axis=1, keepdims=True), 0.0) * w
        scn_ref[0] = jnp.broadcast_to(jnp.sum(s_in, axis=0, keepdims=True), (1, LANES))


def sample_attention(page_table, layer, cache_k, cache_v, cache_i_t, qa, qi, wi, k_new, v_new, i_new, lq, lk, g,
                     *, n_heads, dqk, lam_init, n_idx, d_idx):
    db, n_pages = page_table.shape
    width = qa.shape[1]
    page = cache_k.shape[2]
    hd = 2 * dqk
    rows = 2 * n_heads
    assert n_heads & (n_heads - 1) == 0
    pt = page_table.reshape(-1)
    group = SAMPLE_ATTN_PAGES if n_pages % SAMPLE_ATTN_PAGES == 0 else 1
    n_steps = n_pages // group

    def pool5(k):
        return lambda b, p, pt_ref: (layer, pt_ref[b * n_pages + p * group + k], 0, 0, 0)

    def pool4(k):
        return lambda b, p, pt_ref: (layer, pt_ref[b * n_pages + p * group + k], 0, 0)

    row3 = lambda b, p, pt_ref: (b, 0, 0)
    const2 = lambda b, p, pt_ref: (0, 0)
    grid_spec = pltpu.PrefetchScalarGridSpec(
        num_scalar_prefetch=1,
        grid=(db, n_steps),
        in_specs=[
            pl.BlockSpec(lq.shape, const2),
            pl.BlockSpec(lk.shape, const2),
            pl.BlockSpec((1, hd), const2),
            pl.BlockSpec((1, n_heads, hd), row3),
            pl.BlockSpec((1, n_idx, d_idx), row3),
            pl.BlockSpec((1, n_idx, 1), row3),
            *[pl.BlockSpec((1, 1, page, n_heads, hd), pool5(k)) for k in range(group)],
            *[pl.BlockSpec((1, 1, page, n_heads, hd), pool5(k)) for k in range(group)],
            *[pl.BlockSpec((1, 1, d_idx, page), pool4(k)) for k in range(group)],
            pl.BlockSpec((1, n_heads, hd), row3),
            pl.BlockSpec((1, n_heads, hd), row3),
            pl.BlockSpec((1, 1, d_idx), row3),
        ],
        out_specs=[
            pl.BlockSpec((1, n_heads, hd), row3),
            pl.BlockSpec((1, group, page), lambda b, p, pt_ref: (b * n_steps + p, 0, 0)),
            pl.BlockSpec((1, 1, LANES), row3),
        ],
        scratch_shapes=[pltpu.VMEM((rows, 1), F32), pltpu.VMEM((rows, 1), F32), pltpu.VMEM((rows, hd), F32)],
    )
    out, scores, score_new = pl.pallas_call(
        functools.partial(_sample_attn_kernel, n_steps=n_steps, group=group, n_heads=n_heads, dqk=dqk,
                          lam_init=lam_init, n_idx=n_idx, d_idx=d_idx),
        grid_spec=grid_spec,
        out_shape=[jax.ShapeDtypeStruct((db, n_heads, hd), F32),
                   jax.ShapeDtypeStruct((db * n_steps, group, page), F32),
                   jax.ShapeDtypeStruct((db, 1, LANES), F32)],
        compiler_params=_params("parallel", "arbitrary"),
        name="sample_attention",
    )(pt, lq, lk, g.reshape(1, hd), qa.reshape(db, n_heads, hd), qi.reshape(db, n_idx, d_idx),
      wi.reshape(db, n_idx, 1), *([cache_k] * group), *([cache_v] * group), *([cache_i_t] * group),
      k_new.reshape(db, n_heads, hd), v_new.reshape(db, n_heads, hd), i_new.reshape(db, 1, d_idx))
    return out.reshape(db, width), scores.reshape(db, n_pages, page), score_new


def _count_all(mask):
    c = jnp.sum(mask.astype(jnp.int32), axis=1, keepdims=True)
    return jnp.sum(c, axis=0, keepdims=True)


def _sample_dsa_kernel(pt_ref, sc_ref, scn_ref, qb_ref, *rest, n_pages, group, page, n_heads, n_kv, dh, topk):
    kp_refs, vp_refs = rest[:group], rest[group:2 * group]
    kn_ref, vn_ref, o_ref, key_ref, keyn_ref, t_ref, cut_ref, selx_ref, m_ref, l_ref, acc_ref = rest[2 * group:]
    p = pl.program_id(1)
    rep = n_heads // n_kv
    scale = dh ** -0.5
    past = n_pages * page
    cols = page * n_kv
    kv_shift = n_kv.bit_length() - 1
    rep_shift = rep.bit_length() - 1

    @pl.when(p == 0)
    def _():
        m_ref[...] = jnp.full(m_ref.shape, -jnp.inf, F32)
        l_ref[...] = jnp.zeros(l_ref.shape, F32)
        acc_ref[...] = jnp.zeros(acc_ref.shape, F32)
        key_ref[...] = _order_key(sc_ref[0])
        keyn_ref[...] = _order_key(scn_ref[0])
        key_new = keyn_ref[:, 0:1]

        def cnt(t):
            return _count_all(key_ref[...] >= t) + (key_new >= t).astype(jnp.int32)

        t0 = jnp.full((1, 1), INT_MIN, jnp.int32)
        zero = jnp.zeros((1, 1), jnp.int32)
        t0 = jnp.where(cnt(zero) >= topk, zero, t0)

        def body(it, t):
            cand = t | (jnp.int32(1) << (30 - it))
            return jnp.where(cnt(cand) >= topk, cand, t)

        t = lax.fori_loop(0, 31, body, t0)
        t_ref[...] = t
        key = key_ref[...]
        n_gt = _count_all(key > t) + (key_new > t).astype(jnp.int32)
        need = topk - n_gt
        pos = (lax.broadcasted_iota(jnp.int32, key.shape, 0) * page
               + lax.broadcasted_iota(jnp.int32, key.shape, 1))
        n_bits = int(math.ceil(math.log2(past + 1)))

        def body2(it, x):
            cand = x | (jnp.int32(1) << (n_bits - 1 - it))
            c = _count_all((key_ref[...] == t) & (pos < cand)) + ((key_new == t) & (past < cand)).astype(jnp.int32)
            return jnp.where(c < need, cand, x)

        cut = lax.fori_loop(0, n_bits, body2, jnp.zeros((1, 1), jnp.int32))
        cut_ref[...] = cut
        sel = (key > t) | ((key == t) & (pos <= cut))
        e_t = lax.broadcasted_iota(jnp.int32, (page, cols), 0)
        e_c = lax.broadcasted_iota(jnp.int32, (page, cols), 1)
        spread = jnp.where((e_c >> kv_shift) == e_t, 1.0, 0.0).astype(BF16)
        selx_ref[...] = _dot(jnp.where(sel, 1.0, 0.0).astype(BF16), spread)

    t = t_ref[...]
    cut = cut_ref[...]
    q = qb_ref[0]

    k2 = jnp.concatenate([r[0, 0].astype(BF16) for r in kp_refs], axis=0)
    v2 = jnp.concatenate([r[0, 0].astype(BF16) for r in vp_refs], axis=0)
    sel_cols = jnp.concatenate(
        [jnp.broadcast_to(selx_ref[pl.ds(p * group + k, 1), :], (n_heads, cols)) for k in range(group)], axis=1)
    r_id = lax.broadcasted_iota(jnp.int32, (n_heads, group * cols), 0)
    c_id = lax.broadcasted_iota(jnp.int32, (n_heads, group * cols), 1)
    keep = (sel_cols > 0.5) & ((c_id & (n_kv - 1)) == (r_id >> rep_shift))
    s = _dot_nt(q.astype(BF16), k2) * scale
    s = jnp.where(keep, s, -jnp.inf)
    m_old = m_ref[...]
    m_new = jnp.maximum(m_old, jnp.max(s, axis=1, keepdims=True))
    m_safe = jnp.where(m_new == -jnp.inf, 0.0, m_new)
    alpha = jnp.exp(m_old - m_safe)
    pr = jnp.exp(s - m_safe)
    l_ref[...] = alpha * l_ref[...] + jnp.sum(pr, axis=1, keepdims=True)
    acc_ref[...] = alpha * acc_ref[...] + _dot(pr.astype(BF16), v2)
    m_ref[...] = m_new

    @pl.when(p == n_pages // group - 1)
    def _():
        key_new = keyn_ref[:, 0:1]
        sel_new = (key_new > t) | ((key_new == t) & (past <= cut))
        h_id = lax.broadcasted_iota(jnp.int32, (n_heads, dh), 0) >> rep_shift
        kn = jnp.zeros((n_heads, dh), F32)
        vn = jnp.zeros((n_heads, dh), F32)
        for g in range(n_kv):
            kn = jnp.where(h_id == g, kn_ref[0, g:g + 1, :], kn)
            vn = jnp.where(h_id == g, vn_ref[0, g:g + 1, :], vn)
        s_new = jnp.sum(q.astype(F32) * kn, axis=1, keepdims=True) * scale
        s_new = jnp.where(sel_new, s_new, -jnp.inf)
        m_o = m_ref[...]
        m_n = jnp.maximum(m_o, s_new)
        m_s = jnp.where(m_n == -jnp.inf, 0.0, m_n)
        a = jnp.exp(m_o - m_s)
        p_new = jnp.exp(s_new - m_s)
        l_fin = a * l_ref[...] + p_new
        o_ref[0] = ((a * acc_ref[...] + p_new * vn) / l_fin).astype(o_ref.dtype)


def sample_dsa(page_table, layer, cache_k, cache_v, scores, score_new, qb, k_new, v_new,
               *, n_heads, n_kv, dh, topk):
    db, n_pages = page_table.shape
    cols = cache_k.shape[2]
    page = cols // n_kv
    rep = n_heads // n_kv
    assert n_kv & (n_kv - 1) == 0 and rep & (rep - 1) == 0
    pt = page_table.reshape(-1)

    group = SAMPLE_DSA_PAGES if n_pages % SAMPLE_DSA_PAGES == 0 else 1

    def pool_map(k):
        return lambda b, p, pt_ref: (layer, pt_ref[b * n_pages + p * group + k], 0, 0)

    row3 = lambda b, p, pt_ref: (b, 0, 0)
    grid_spec = pltpu.PrefetchScalarGridSpec(
        num_scalar_prefetch=1,
        grid=(db, n_pages // group),
        in_specs=[
            pl.BlockSpec((1, n_pages, page), row3),
            pl.BlockSpec((1, 1, LANES), row3),
            pl.BlockSpec((1, n_heads, dh), row3),
            *[pl.BlockSpec((1, 1, cols, dh), pool_map(k)) for k in range(group)],
            *[pl.BlockSpec((1, 1, cols, dh), pool_map(k)) for k in range(group)],
            pl.BlockSpec((1, n_kv, dh), row3),
            pl.BlockSpec((1, n_kv, dh), row3),
        ],
        out_specs=pl.BlockSpec((1, n_heads, dh), row3),
        scratch_shapes=[pltpu.VMEM((n_pages, page), jnp.int32), pltpu.VMEM((1, LANES), jnp.int32),
                        pltpu.VMEM((1, 1), jnp.int32), pltpu.VMEM((1, 1), jnp.int32),
                        pltpu.VMEM((n_pages, cols), F32),
                        pltpu.VMEM((n_heads, 1), F32), pltpu.VMEM((n_heads, 1), F32),
                        pltpu.VMEM((n_heads, dh), F32)],
    )
    out = pl.pallas_call(
        functools.partial(_sample_dsa_kernel, n_pages=n_pages, group=group, page=page, n_heads=n_heads,
                          n_kv=n_kv, dh=dh, topk=topk),
        grid_spec=grid_spec,
        out_shape=jax.ShapeDtypeStruct((db, n_heads, dh), F32),
        compiler_params=_params("parallel", "arbitrary"),
        name="sample_dsa",
    )(pt, scores, score_new, qb.reshape(db, n_heads, dh), *([cache_k] * group), *([cache_v] * group),
      k_new.reshape(db, n_kv, dh), v_new.reshape(db, n_kv, dh))
    return out.reshape(db, n_heads * dh)


def kernel(x_prompt, x_sample, cache_a_k, cache_a_v, cache_b_k, cache_b_v, cache_idx_k, page_table, w_in, lambda_q, lambda_k, subln_g, w_branch_a, w_branch_b, w_out, norm_mix_g, norm_ffn_g, w_dense_gate, w_dense_up, w_dense_down, w_router, w_exp_gate, w_exp_up, w_exp_down, norm_final_g):
    batch, seq, d_model = x_prompt.shape
    db, dec_seq, _ = x_sample.shape
    assert dec_seq == 1, "one new token per sample row"
    depth, n_pool, page, n_ha, a_width = cache_a_k.shape
    assert page == PAGE_SIZE
    dqk = a_width // 2
    dv = cache_a_v.shape[4]
    n_kv, dh = cache_b_k.shape[3:]
    d_idx = cache_idx_k.shape[3]
    n_hb = w_branch_b.shape[1] // dh
    n_in = w_in.shape[2]
    wa, wva, wqb, wkb = n_ha * 2 * dqk, n_ha * dv, n_hb * dh, n_kv * dh
    n_idx = (n_in - (2 * wa + wva + wqb + 2 * wkb + d_idx + 2 * d_model)) // (d_idx + 1)
    assert dv == 2 * dqk == LANES and dh == LANES and 2 * d_idx == LANES
    n_pages = page_table.shape[1]
    past_len = n_pages * page
    topk_p = min(TOPK_MAX, seq // 4)
    topk_s = min(TOPK_MAX, (past_len + dec_seq) // 4)
    n_exp = w_router.shape[2]

    offs = [0]
    for wdt in (wa, wa, wva, wqb, wkb, wkb, n_idx * d_idx, d_idx, n_idx, d_model, d_model):
        offs.append(offs[-1] + wdt)
    o_qa, o_ka, o_va, o_qb, o_kb, o_vb, o_qi, o_ki, o_wi, o_ga, o_gb, _ = offs

    pos_p = jnp.arange(seq, dtype=F32)
    pos_s = jnp.full((db,), float(past_len), F32)
    tabs = {}
    for name, pos in (("p", pos_p), ("s", pos_s)):
        tabs[name, "a"] = rope_table(pos, dqk)
        tabs[name, "b"] = rope_table(pos, dh)
        tabs[name, "i"] = rope_table(pos, d_idx, valid_lanes=d_idx)

    ck_b = cache_b_k.reshape(depth, n_pool, page * n_kv, dh)
    cv_b = cache_b_v.reshape(depth, n_pool, page * n_kv, dh)
    ci_t = jnp.swapaxes(cache_idx_k, 2, 3)

    xp = x_prompt.reshape(batch * seq, d_model)
    xs = x_sample.reshape(db * dec_seq, d_model)
    rows = {"p": [[] for _ in range(5)], "s": [[] for _ in range(5)]}

    for l in range(depth):
        lam_init = 0.8 - 0.6 * math.exp(-0.3 * l)
        lq, lk = lambda_q[l], lambda_k[l]
        mix_w = {}
        for grp, dt in (("p", BF16), ("s", F32)):
            wl = w_in[l].astype(dt)
            mix_w[grp] = dict(
                wl=wl, kiwi=jnp.pad(wl[:, o_ki:o_ga], ((0, 0), (0, LANES - d_idx - n_idx))),
                pa=w_branch_a[l].astype(dt), pb=w_branch_b[l].astype(dt), o=w_out[l].astype(dt))

        def in_proj(x, grp):
            act = BF16 if grp == "p" else F32
            wl = mix_w[grp]["wl"]
            xn = rmsnorm(x, norm_mix_g[l], act)
            ta, ha = tabs[grp, "a"]
            tb, hb = tabs[grp, "b"]
            ti, hi = tabs[grp, "i"]

            def kv_pair(w, tab=None, half=0):
                outs = project(xn, w, [F32] if act == F32 else [F32, BF16], tab, half)
                return outs[0], outs[-1]

            z = {}
            z["qa"], = project(xn, wl[:, o_qa:o_ka], [act], ta, ha)
            z["ka32"], z["ka"] = kv_pair(wl[:, o_ka:o_va], ta, ha)
            z["va32"], z["va"] = kv_pair(wl[:, o_va:o_qb])
            z["qb"], = project(xn, wl[:, o_qb:o_kb], [act], tb, hb)
            z["kb32"], z["kb"] = kv_pair(wl[:, o_kb:o_vb], tb, hb)
            z["vb32"], z["vb"] = kv_pair(wl[:, o_vb:o_qi])
            z["qi"], = project(xn, wl[:, o_qi:o_ki], [act], ta, ha)
            z["kiwi"], = project(xn, mix_w[grp]["kiwi"], [F32], ti, hi)
            z["ga"], = project(xn, wl[:, o_ga:o_gb], [F32])
            z["gb"], = project(xn, wl[:, o_gb:], [F32])
            return z

        def mix_out(x, z, oa, ob, grp):
            w = mix_w[grp]
            u = merge_branches(oa, ob, w["pa"], w["pb"], z["ga"], z["gb"])
            return matmul_residual(u, w["o"], x)

        zp = in_proj(xp, "p")
        oa = diff_attention_prompt(zp["qa"], zp["ka"], zp["va"].T, lq, lk, subln_g[l], batch=batch, seq=seq,
                                   n_heads=n_ha, dqk=dqk, lam_init=lam_init)
        ob = dsa_prompt(zp["qi"], zp["kiwi"], zp["qb"], zp["kb"], zp["vb"], batch=batch, seq=seq, n_idx=n_idx,
                        d_idx=d_idx, n_heads=n_hb, n_kv=n_kv, dh=dh, topk=topk_p)
        xp = mix_out(xp, zp, oa, ob, "p")

        zs = in_proj(xs, "s")
        ki_s = zs["kiwi"][:, :d_idx]
        wi_s = zs["kiwi"][:, d_idx:d_idx + n_idx]
        oa, scores, score_new = sample_attention(
            page_table, l, cache_a_k, cache_a_v, ci_t, zs["qa"], zs["qi"], wi_s, zs["ka32"], zs["va32"], ki_s,
            lq, lk, subln_g[l], n_heads=n_ha, dqk=dqk, lam_init=lam_init, n_idx=n_idx, d_idx=d_idx)
        ob = sample_dsa(page_table, l, ck_b, cv_b, scores, score_new, zs["qb"], zs["kb32"], zs["vb32"],
                        n_heads=n_hb, n_kv=n_kv, dh=dh, topk=topk_s)
        xs = mix_out(xs, zs, oa, ob, "s")

        for grp, z, bt, tt in (("p", zp, batch, seq), ("s", zs, db, dec_seq)):
            rows[grp][0].append(z["ka32"].reshape(bt, tt, n_ha, 2 * dqk))
            rows[grp][1].append(z["va32"].reshape(bt, tt, n_ha, dv))
            rows[grp][2].append(z["kb32"].reshape(bt, tt, n_kv, dh))
            rows[grp][3].append(z["vb32"].reshape(bt, tt, n_kv, dh))
            rows[grp][4].append(z["kiwi"][:, :d_idx].reshape(bt, tt, d_idx))

        i = l // 2
        if l % 2 == 0:
            hp = rmsnorm(xp, norm_ffn_g[l], BF16)
            hs = rmsnorm(xs, norm_ffn_g[l], F32)
            wg, wu, wd = (w_dense_gate[i].astype(BF16), w_dense_up[i].astype(BF16), w_dense_down[i].astype(BF16))
            xp = matmul_residual(gate_up(hp, wg, wu), wd, xp)
            xs = matmul_residual(gate_up(hs, w_dense_gate[i], w_dense_up[i]), w_dense_down[i], xs)
        else:
            xp, xs = moe_ffn(xp, xs, norm_ffn_g[l], w_router[i], w_exp_gate[i].astype(BF16),
                             w_exp_up[i].astype(BF16), w_exp_down[i].astype(BF16))

    y_prompt = rmsnorm(xp, norm_final_g, F32).reshape(batch, seq, d_model)
    y_sample = rmsnorm(xs, norm_final_g, F32).reshape(db, dec_seq, d_model)
    outs_p = [jnp.stack(r, axis=0) for r in rows["p"]]
    outs_s = [jnp.stack(r, axis=0) for r in rows["s"]]
    return (y_prompt, y_sample, *outs_p, *outs_s)
```

```python
import functools
import math

import jax
import jax.numpy as jnp
from jax import lax
from jax.experimental import pallas as pl
from jax.experimental.pallas import tpu as pltpu

LANES = 128
VMEM_LIMIT_BYTES = 56 * 1024 * 1024
ROPE_THETA = 500000.0
ROPE_FRAC = 4
RMS_EPS = 1e-6
TOPK_MAX = 256
PAGE_SIZE = 128
TOP_K_EXPERTS = 2
INT_MIN = -2 ** 31
LOG2_E = math.log2(math.e)
F32 = jnp.float32
BF16 = jnp.bfloat16


def _pick(n, pref, mult):
    if n <= pref:
        return n
    t = (pref // mult) * mult
    while t >= mult:
        if n % t == 0:
            return t
        t -= mult
    return n


def _params(*sem):
    return pltpu.CompilerParams(dimension_semantics=sem, vmem_limit_bytes=VMEM_LIMIT_BYTES)


def _precision(a, b):
    return lax.Precision.HIGHEST if a.dtype == F32 and b.dtype == F32 else None


def _dot(a, b):
    return jnp.dot(a, b, preferred_element_type=F32, precision=_precision(a, b))


def _dot_nt(a, b):
    return lax.dot_general(a, b, (((1,), (1,)), ((), ())), preferred_element_type=F32,
                           precision=_precision(a, b))


def _rms(x, g):
    return x * lax.rsqrt(jnp.mean(x * x, axis=-1, keepdims=True) + RMS_EPS) * g


def _rmsnorm_kernel(x_ref, g_ref, o_ref):
    o_ref[...] = _rms(x_ref[...], g_ref[...]).astype(o_ref.dtype)


def rmsnorm(x, g, out_dtype):
    m, d = x.shape
    tm = _pick(m, 512, 8)
    return pl.pallas_call(
        _rmsnorm_kernel,
        grid=(m // tm,),
        in_specs=[pl.BlockSpec((tm, d), lambda i: (i, 0)), pl.BlockSpec((1, d), lambda i: (0, 0))],
        out_specs=pl.BlockSpec((tm, d), lambda i: (i, 0)),
        out_shape=jax.ShapeDtypeStruct((m, d), out_dtype),
        compiler_params=_params("parallel"),
        name="rmsnorm",
    )(x, g.reshape(1, d))


def _proj_kernel(*refs, half, tn):
    if half:
        x_ref, w_ref, tab_ref, *o_refs = refs
    else:
        x_ref, w_ref, *o_refs = refs
    z = _dot(x_ref[...], w_ref[...])
    if not half:
        for o in o_refs:
            o[...] = z.astype(o.dtype)
        return
    c, s_lo, s_hi = tab_ref[0], tab_ref[1], tab_ref[2]
    for g in range(tn // LANES):
        zg = z[:, g * LANES:(g + 1) * LANES]
        r = zg * c + pltpu.roll(zg, LANES - half, 1) * s_lo + pltpu.roll(zg, half, 1) * s_hi
        for o in o_refs:
            o[:, g * LANES:(g + 1) * LANES] = r.astype(o.dtype)


def project(xn, w, out_dtypes, tab=None, half=0, tn_pref=512):
    m, k = xn.shape
    n = w.shape[1]
    p_rows = m if tab is None else tab.shape[1]
    assert m % p_rows == 0
    tm = _pick(p_rows, 1024, 8)
    tn = _pick(n, tn_pref, LANES)
    in_specs = [pl.BlockSpec((tm, k), lambda i, j: (i, 0)), pl.BlockSpec((k, tn), lambda i, j: (0, j))]
    args = [xn, w]
    if tab is not None:
        nblk = p_rows // tm
        in_specs.append(pl.BlockSpec((3, tm, LANES), lambda i, j: (0, i % nblk, 0)))
        args.append(tab)
    outs = pl.pallas_call(
        functools.partial(_proj_kernel, half=half if tab is not None else 0, tn=tn),
        grid=(m // tm, n // tn),
        in_specs=in_specs,
        out_specs=[pl.BlockSpec((tm, tn), lambda i, j: (i, j)) for _ in out_dtypes],
        out_shape=[jax.ShapeDtypeStruct((m, n), dt) for dt in out_dtypes],
        compiler_params=_params("parallel", "arbitrary"),
        name="in_proj",
    )(*args)
    return outs


def rope_table(pos, head_dim, valid_lanes=LANES):
    rot = head_dim // ROPE_FRAC
    half = rot // 2
    inv_freq = jnp.power(ROPE_THETA, -jnp.arange(half, dtype=F32) * 2.0 / rot)
    ang = pos[:, None] * inv_freq[None, :]
    cos, sin = jnp.cos(ang), jnp.sin(ang)
    n = pos.shape[0]
    ones = jnp.ones((n, head_dim - rot), F32)
    zeros_h = jnp.zeros((n, half), F32)
    zeros_r = jnp.zeros((n, head_dim - rot), F32)
    c = jnp.concatenate([cos, cos, ones], axis=1)
    s_lo = jnp.concatenate([-sin, zeros_h, zeros_r], axis=1)
    s_hi = jnp.concatenate([zeros_h, sin, zeros_r], axis=1)
    reps = LANES // head_dim
    tabs = [jnp.tile(t, (1, reps)) for t in (c, s_lo, s_hi)]
    if valid_lanes < LANES:
        lane = jnp.arange(LANES)[None, :]
        tabs = [jnp.where(lane < valid_lanes, tabs[0], 1.0), jnp.where(lane < valid_lanes, tabs[1], 0.0),
                jnp.where(lane < valid_lanes, tabs[2], 0.0)]
    return jnp.stack(tabs, axis=0), half


def _lambda(lq_ref, lk_ref, lam_init):
    lq = lq_ref[...].astype(F32)
    lk = lk_ref[...].astype(F32)
    prod = lq * lk
    return (jnp.exp(jnp.sum(prod[0:1, :], axis=1, keepdims=True))
            - jnp.exp(jnp.sum(prod[1:2, :], axis=1, keepdims=True)) + lam_init)


def _diffattn_kernel(lq_ref, lk_ref, g_ref, q_ref, k_ref, vt_ref, o_ref, m_ref, l_ref, acc_ref,
                     *, tq, dqk, lam_init):
    i = pl.program_id(2)
    c_exp = dqk ** -0.5 * LOG2_E
    q = q_ref[...].astype(F32)
    lane = lax.broadcasted_iota(jnp.int32, q.shape, 1)
    qs = jnp.concatenate([jnp.where(lane < dqk, q, 0.0), jnp.where(lane >= dqk, q, 0.0)], axis=0)
    qs = qs.astype(q_ref.dtype)
    m_ref[...] = jnp.full(m_ref.shape, -jnp.inf, F32)
    l_ref[...] = jnp.zeros(l_ref.shape, F32)
    acc_ref[...] = jnp.zeros(acc_ref.shape, F32)

    def step(j, masked):
        start = pl.multiple_of(j * tq, tq)
        kt = k_ref[pl.ds(start, tq), :]
        vt = vt_ref[:, pl.ds(start, tq)]
        s = _dot_nt(kt, qs) * c_exp
        if masked:
            key = lax.broadcasted_iota(jnp.int32, s.shape, 0)
            qry = lax.broadcasted_iota(jnp.int32, s.shape, 1) & (tq - 1)
            s = jnp.where(key <= qry, s, -jnp.inf)
        m_old = m_ref[...]
        m_new = jnp.maximum(m_old, jnp.max(s, axis=0, keepdims=True))
        alpha = jnp.exp2(m_old - m_new)
        p = jnp.exp2(s - m_new)
        l_ref[...] = alpha * l_ref[...] + jnp.sum(p, axis=0, keepdims=True)
        acc_ref[...] = alpha * acc_ref[...] + _dot(vt, p.astype(vt.dtype))
        m_ref[...] = m_new

    def body(j, carry):
        step(j, False)
        return carry

    lax.fori_loop(0, i, body, 0)
    step(i, True)

    lam = _lambda(lq_ref, lk_ref, lam_init)
    o_all = acc_ref[...] / l_ref[...]
    o = o_all[:, :tq] - lam * o_all[:, tq:]
    inv = lax.rsqrt(jnp.mean(o * o, axis=0, keepdims=True) + RMS_EPS)
    y = o * inv * g_ref[...] * (1.0 - lam_init)
    o_ref[...] = y.T.astype(o_ref.dtype)


def diff_attention_prompt(q, k, v_t, lq, lk, g, *, batch, seq, n_heads, dqk, lam_init):
    m, width = q.shape
    hd = width // n_heads
    tq = _pick(seq, 512, LANES)
    assert tq & (tq - 1) == 0
    nq = seq // tq
    return pl.pallas_call(
        functools.partial(_diffattn_kernel, tq=tq, dqk=dqk, lam_init=lam_init),
        grid=(batch, n_heads, nq),
        in_specs=[
            pl.BlockSpec(lq.shape, lambda b, h, i: (0, 0)),
            pl.BlockSpec(lk.shape, lambda b, h, i: (0, 0)),
            pl.BlockSpec((hd, 1), lambda b, h, i: (0, 0)),
            pl.BlockSpec((tq, hd), lambda b, h, i: (b * nq + i, h)),
            pl.BlockSpec((seq, hd), lambda b, h, i: (b, h)),
            pl.BlockSpec((hd, seq), lambda b, h, i: (h, b)),
        ],
        out_specs=pl.BlockSpec((tq, hd), lambda b, h, i: (b * nq + i, h)),
        out_shape=jax.ShapeDtypeStruct((m, width), BF16),
        scratch_shapes=[pltpu.VMEM((1, 2 * tq), F32), pltpu.VMEM((1, 2 * tq), F32), pltpu.VMEM((hd, 2 * tq), F32)],
        compiler_params=_params("parallel", "parallel", "arbitrary"),
        name="diff_attn_prompt",
    )(lq, lk, g.reshape(hd, 1), q, k, v_t)


def _order_key(score):
    score = jnp.where(score == 0.0, 0.0, score)
    bits = lax.bitcast_convert_type(score, jnp.int32)
    return jnp.where(bits < 0, bits ^ jnp.int32(0x7FFFFFFF), bits)


def _count_keys(mask):
    return jnp.sum(mask.astype(jnp.int32), axis=0, keepdims=True)


def _kth_largest_key(load_keys, shape, k, count_fn):
    t0 = jnp.full(shape, INT_MIN, jnp.int32)
    zero = jnp.zeros(shape, jnp.int32)
    t0 = jnp.where(count_fn(load_keys() >= zero) >= k, zero, t0)

    def body(it, t):
        cand = t | (jnp.int32(1) << (30 - it))
        return jnp.where(count_fn(load_keys() >= cand) >= k, cand, t)

    return lax.fori_loop(0, 31, body, t0)


def _tie_cut(load_eq_pos, shape, need, n_bits, count_fn):
    def body(it, x):
        cand = x | (jnp.int32(1) << (n_bits - 1 - it))
        eq, pos = load_eq_pos()
        return jnp.where(count_fn(eq & (pos < cand)) < need, cand, x)

    return lax.fori_loop(0, n_bits, body, jnp.zeros(shape, jnp.int32))


def _dsa_prompt_kernel(qi_ref, kiwi_k_ref, kiwi_q_ref, qb_ref, kb_ref, vbt_ref, o_ref,
                       kdup_ref, key_ref, bias_ref, cut_ref,
                       *, tq, seq, extents, n_idx, d_idx, n_heads, n_kv, dh, topk):
    i = pl.program_id(1)

    @pl.when(i == 0)
    def _():
        kf = kiwi_k_ref[...]
        lane = lax.broadcasted_iota(jnp.int32, kf.shape, 1)
        klo = jnp.where(lane < d_idx, kf, 0.0)
        kdup_ref[...] = (klo + pltpu.roll(klo, d_idx, 1)).astype(kdup_ref.dtype)

    wscale = n_idx ** -0.5 * d_idx ** -0.5
    c_exp = dh ** -0.5 * LOG2_E
    rep = n_heads // n_kv

    def run(ncol):
        kd = kdup_ref[:ncol, :]
        w_t = (kiwi_q_ref[...] * wscale).T
        score = jnp.zeros((ncol, tq), F32)
        for h in range(n_idx):
            grp = qi_ref[:, (h // 2) * LANES:(h // 2 + 1) * LANES].astype(F32)
            lane = lax.broadcasted_iota(jnp.int32, grp.shape, 1)
            keep = (lane < d_idx) if h % 2 == 0 else (lane >= d_idx)
            qh = jnp.where(keep, grp, 0.0).astype(qi_ref.dtype)
            score = score + jnp.maximum(_dot_nt(kd, qh), 0.0) * w_t[d_idx + h:d_idx + h + 1, :]

        kpos = lax.broadcasted_iota(jnp.int32, (ncol, tq), 0)
        qpos = i * tq + lax.broadcasted_iota(jnp.int32, (ncol, tq), 1)
        causal = kpos <= qpos
        key_ref[:ncol, :] = jnp.where(causal, _order_key(score), INT_MIN)

        t = _kth_largest_key(lambda: key_ref[:ncol, :], (1, tq), topk, _count_keys)
        key = key_ref[:ncol, :]
        eq = key == t
        need = topk - _count_keys(key > t)
        tie = (_count_keys(eq) > need) & (t > INT_MIN)
        cut_ref[...] = jnp.full((1, tq), ncol, jnp.int32)

        @pl.when(jnp.max(tie.astype(jnp.int32)) > 0)
        def _():
            def load():
                return key_ref[:ncol, :] == t, lax.broadcasted_iota(jnp.int32, (ncol, tq), 0)
            cut_ref[...] = _tie_cut(load, (1, tq), need, int(math.log2(seq)), _count_keys)

        sel = ((key > t) | (eq & (kpos <= cut_ref[...]))) & causal
        bias_ref[:ncol, :] = jnp.where(sel, 0.0, -jnp.inf)

        for h in range(n_heads):
            g = h // rep
            s = (_dot_nt(kb_ref[:ncol, g * dh:(g + 1) * dh], qb_ref[:, h * dh:(h + 1) * dh]) * c_exp
                 + bias_ref[:ncol, :])
            p = jnp.exp2(s - jnp.max(s, axis=0, keepdims=True))
            l = jnp.sum(p, axis=0, keepdims=True)
            o_t = _dot(vbt_ref[g * dh:(g + 1) * dh, :ncol], p.astype(vbt_ref.dtype)) / l
            o_ref[:, h * dh:(h + 1) * dh] = o_t.T.astype(o_ref.dtype)

    lo = 0
    for ncol in extents:
        @pl.when(((i + 1) * tq > lo) & ((i + 1) * tq <= ncol))
        def _(ncol=ncol):
            run(ncol)
        lo = ncol


def dsa_prompt(qi, kiwi, qb, kb, vb_t, *, batch, seq, n_idx, d_idx, n_heads, n_kv, dh, topk):
    m = qi.shape[0]
    assert 2 * d_idx == LANES and seq & (seq - 1) == 0
    tq = _pick(seq, 256, LANES)
    nq = seq // tq
    n_ext = min(4, nq)
    extents = tuple(seq * (c + 1) // n_ext for c in range(n_ext))
    return pl.pallas_call(
        functools.partial(_dsa_prompt_kernel, tq=tq, seq=seq, extents=extents, n_idx=n_idx, d_idx=d_idx,
                          n_heads=n_heads,
                          n_kv=n_kv, dh=dh, topk=topk),
        grid=(batch, nq),
        in_specs=[
            pl.BlockSpec((tq, qi.shape[1]), lambda b, i: (b * nq + i, 0)),
            pl.BlockSpec((seq, LANES), lambda b, i: (b, 0)),
            pl.BlockSpec((tq, LANES), lambda b, i: (b * nq + i, 0)),
            pl.BlockSpec((tq, qb.shape[1]), lambda b, i: (b * nq + i, 0)),
            pl.BlockSpec((seq, kb.shape[1]), lambda b, i: (b, 0)),
            pl.BlockSpec((vb_t.shape[0], seq), lambda b, i: (0, b)),
        ],
        out_specs=pl.BlockSpec((tq, qb.shape[1]), lambda b, i: (b * nq + i, 0)),
        out_shape=jax.ShapeDtypeStruct((m, qb.shape[1]), BF16),
        scratch_shapes=[pltpu.VMEM((seq, LANES), BF16), pltpu.VMEM((seq, tq), jnp.int32),
                        pltpu.VMEM((seq, tq), F32), pltpu.VMEM((1, tq), jnp.int32)],
        compiler_params=_params("parallel", "arbitrary"),
        name="dsa_prompt",
    )(qi, kiwi, kiwi, qb, kb, vb_t)


def _merge_kernel(oa_ref, ob_ref, wa_ref, wb_ref, ga_ref, gb_ref, o_ref):
    ya = _dot(oa_ref[...], wa_ref[...])
    yb = _dot(ob_ref[...], wb_ref[...])
    u = jax.nn.sigmoid(ga_ref[...]) * ya + jax.nn.sigmoid(gb_ref[...]) * yb
    o_ref[...] = u.astype(o_ref.dtype)


def merge_branches(oa, ob, w_pa, w_pb, ga, gb):
    m, ka = oa.shape
    kb = ob.shape[1]
    n = w_pa.shape[1]
    tm = _pick(m, 1024, 8)
    tn = _pick(n, 512, LANES)
    return pl.pallas_call(
        _merge_kernel,
        grid=(m // tm, n // tn),
        in_specs=[
            pl.BlockSpec((tm, ka), lambda i, j: (i, 0)),
            pl.BlockSpec((tm, kb), lambda i, j: (i, 0)),
            pl.BlockSpec((ka, tn), lambda i, j: (0, j)),
            pl.BlockSpec((kb, tn), lambda i, j: (0, j)),
            pl.BlockSpec((tm, tn), lambda i, j: (i, j)),
            pl.BlockSpec((tm, tn), lambda i, j: (i, j)),
        ],
        out_specs=pl.BlockSpec((tm, tn), lambda i, j: (i, j)),
        out_shape=jax.ShapeDtypeStruct((m, n), oa.dtype),
        compiler_params=_params("parallel", "arbitrary"),
        name="merge_branches",
    )(oa, ob, w_pa, w_pb, ga, gb)


def _matmul_res_kernel(a_ref, w_ref, r_ref, o_ref):
    o_ref[...] = r_ref[...] + _dot(a_ref[...], w_ref[...])


def matmul_residual(a, w, res):
    m, k = a.shape
    n = w.shape[1]
    tm = _pick(m, 1024, 8)
    tn = _pick(n, 512, LANES)
    return pl.pallas_call(
        _matmul_res_kernel,
        grid=(m // tm, n // tn),
        in_specs=[
            pl.BlockSpec((tm, k), lambda i, j: (i, 0)),
            pl.BlockSpec((k, tn), lambda i, j: (0, j)),
            pl.BlockSpec((tm, tn), lambda i, j: (i, j)),
        ],
        out_specs=pl.BlockSpec((tm, tn), lambda i, j: (i, j)),
        out_shape=jax.ShapeDtypeStruct((m, n), F32),
        compiler_params=_params("parallel", "arbitrary"),
        name="matmul_residual",
    )(a, w, res)


def _gateup_kernel(x_ref, wg_ref, wu_ref, o_ref):
    x = x_ref[...]
    gate = _dot(x, wg_ref[...].astype(x.dtype))
    up = _dot(x, wu_ref[...].astype(x.dtype))
    o_ref[...] = (jax.nn.silu(gate) * up).astype(o_ref.dtype)


def gate_up(xn, wg, wu):
    m, k = xn.shape
    n = wg.shape[1]
    tm = _pick(m, 1024, 8)
    tn = _pick(n, 512, LANES)
    return pl.pallas_call(
        _gateup_kernel,
        grid=(n // tn, m // tm),
        in_specs=[
            pl.BlockSpec((tm, k), lambda j, i: (i, 0)),
            pl.BlockSpec((k, tn), lambda j, i: (0, j)),
            pl.BlockSpec((k, tn), lambda j, i: (0, j)),
        ],
        out_specs=pl.BlockSpec((tm, tn), lambda j, i: (i, j)),
        out_shape=jax.ShapeDtypeStruct((m, n), xn.dtype),
        compiler_params=_params("parallel", "arbitrary"),
        name="ffn_gate_up",
    )(xn, wg, wu)


def _router_kernel(x_ref, g_ref, wr_ref, base_ref, hn_ref, gate_ref, route_ref, cnt_ref, run_ref, *, n_exp):
    @pl.when(pl.program_id(0) == 0)
    def _():
        run_ref[...] = base_ref[...]

    hn = _rms(x_ref[...], g_ref[...])
    hn_ref[...] = hn
    tm = hn.shape[0]
    lane = lax.broadcasted_iota(jnp.int32, (tm, LANES), 1)
    logits = jnp.full((tm, LANES), -jnp.inf, F32)
    for e in range(n_exp):
        le = jnp.sum(hn * wr_ref[e:e + 1, :], axis=1, keepdims=True)
        logits = jnp.where(lane == e, le, logits)
    v1 = jnp.max(logits, axis=1, keepdims=True)
    i1 = jnp.min(jnp.where(logits == v1, lane, LANES), axis=1, keepdims=True)
    rest = jnp.where(lane == i1, -jnp.inf, logits)
    v2 = jnp.max(rest, axis=1, keepdims=True)
    i2 = jnp.min(jnp.where(rest == v2, lane, LANES), axis=1, keepdims=True)
    e2 = jnp.exp(v2 - v1)
    den = 1.0 + e2
    gate_ref[...] = jnp.where(lane == 0, 1.0 / den, jnp.where(lane == 1, e2 / den, 0.0))

    onehot = jnp.where((lane == i1) | (lane == i2), 1.0, 0.0)
    r = lax.broadcasted_iota(jnp.int32, (tm, tm), 0)
    c = lax.broadcasted_iota(jnp.int32, (tm, tm), 1)
    before = _dot(jnp.where(c < r, 1.0, 0.0).astype(BF16), onehot.astype(BF16)) + run_ref[...]
    r1 = jnp.sum(jnp.where(lane == i1, before, 0.0), axis=1, keepdims=True).astype(jnp.int32)
    r2 = jnp.sum(jnp.where(lane == i2, before, 0.0), axis=1, keepdims=True).astype(jnp.int32)
    route_ref[...] = jnp.where(lane == 0, i1, jnp.where(lane == 1, i2, jnp.where(lane == 2, r1,
                               jnp.where(lane == 3, r2, 0))))
    run_ref[...] = run_ref[...] + jnp.sum(onehot, axis=0, keepdims=True)
    cnt_ref[...] = run_ref[...]


def moe_router(x, g, w_router, base_counts):
    m, d = x.shape
    n_exp = w_router.shape[1]
    assert TOP_K_EXPERTS == 2 and n_exp <= LANES
    tm = _pick(m, 256, 8)
    row = lambda i: (i, 0)
    fixed = lambda i: (0, 0)
    return pl.pallas_call(
        functools.partial(_router_kernel, n_exp=n_exp),
        grid=(m // tm,),
        in_specs=[
            pl.BlockSpec((tm, d), row),
            pl.BlockSpec((1, d), fixed),
            pl.BlockSpec((n_exp, d), fixed),
            pl.BlockSpec((1, LANES), fixed),
        ],
        out_specs=[pl.BlockSpec((tm, d), row), pl.BlockSpec((tm, LANES), row), pl.BlockSpec((tm, LANES), row),
                   pl.BlockSpec((1, LANES), fixed)],
        out_shape=[jax.ShapeDtypeStruct((m, d), F32), jax.ShapeDtypeStruct((m, LANES), F32),
                   jax.ShapeDtypeStruct((m, LANES), jnp.int32), jax.ShapeDtypeStruct((1, LANES), F32)],
        scratch_shapes=[pltpu.VMEM((1, LANES), F32)],
        compiler_params=_params("arbitrary"),
        name="moe_router",
    )(x, g.reshape(1, d), w_router.T, base_counts)


def _row_copy(src_ref, src_row, dst_ref, dst_row, sem):
    return pltpu.make_async_copy(src_ref.at[pl.ds(src_row, 1)], dst_ref.at[pl.ds(dst_row, 1)], sem)


def _dispatch_kernel(pos1_ref, pos2_ref, x_ref, xs_in_ref, xs_ref, sem, *, tm):
    del xs_in_ref
    base = pl.program_id(0) * tm

    def start(r, carry):
        _row_copy(x_ref, r, xs_ref, pos1_ref[base + r], sem).start()
        _row_copy(x_ref, r, xs_ref, pos2_ref[base + r], sem).start()
        return carry

    def wait(r, carry):
        _row_copy(x_ref, 0, xs_ref, 0, sem).wait()
        _row_copy(x_ref, 0, xs_ref, 0, sem).wait()
        return carry

    lax.fori_loop(0, tm, start, 0)
    lax.fori_loop(0, tm, wait, 0)


def moe_dispatch(hn, pos1, pos2, x_sorted):
    m, d = hn.shape
    tm = _pick(m, 256, 8)
    grid_spec = pltpu.PrefetchScalarGridSpec(
        num_scalar_prefetch=2,
        grid=(m // tm,),
        in_specs=[pl.BlockSpec((tm, d), lambda i, p1, p2: (i, 0)), pl.BlockSpec(memory_space=pl.ANY)],
        out_specs=pl.BlockSpec(memory_space=pl.ANY),
        scratch_shapes=[pltpu.SemaphoreType.DMA(())],
    )
    return pl.pallas_call(
        functools.partial(_dispatch_kernel, tm=tm),
        grid_spec=grid_spec,
        out_shape=jax.ShapeDtypeStruct(x_sorted.shape, x_sorted.dtype),
        input_output_aliases={3: 0},
        compiler_params=_params("arbitrary"),
        name="moe_dispatch",
    )(pos1, pos2, hn, x_sorted)


def _combine_kernel(pos1_ref, pos2_ref, x_ref, gate_ref, ys_ref, o_ref, buf_ref, sem, *, tm):
    base = pl.program_id(0) * tm

    def start(r, carry):
        _row_copy(ys_ref, pos1_ref[base + r], buf_ref.at[0], r, sem).start()
        _row_copy(ys_ref, pos2_ref[base + r], buf_ref.at[1], r, sem).start()
        return carry

    def wait(r, carry):
        _row_copy(ys_ref, 0, buf_ref.at[0], 0, sem).wait()
        _row_copy(ys_ref, 0, buf_ref.at[1], 0, sem).wait()
        return carry

    lax.fori_loop(0, tm, start, 0)
    lax.fori_loop(0, tm, wait, 0)
    gate = gate_ref[...]
    o_ref[...] = x_ref[...] + gate[:, 0:1] * buf_ref[0] + gate[:, 1:2] * buf_ref[1]


def moe_combine(x, gates, pos1, pos2, y_sorted):
    m, d = x.shape
    tm = _pick(m, 256, 8)
    row = lambda i, p1, p2: (i, 0)
    grid_spec = pltpu.PrefetchScalarGridSpec(
        num_scalar_prefetch=2,
        grid=(m // tm,),
        in_specs=[pl.BlockSpec((tm, d), row), pl.BlockSpec((tm, LANES), row), pl.BlockSpec(memory_space=pl.ANY)],
        out_specs=pl.BlockSpec((tm, d), row),
        scratch_shapes=[pltpu.VMEM((2, tm, d), F32), pltpu.SemaphoreType.DMA(())],
    )
    return pl.pallas_call(
        functools.partial(_combine_kernel, tm=tm),
        grid_spec=grid_spec,
        out_shape=jax.ShapeDtypeStruct((m, d), F32),
        compiler_params=_params("arbitrary"),
        name="moe_combine",
    )(pos1, pos2, x, gates, y_sorted)


def _expert_gateup_kernel(te_ref, nu_ref, x_ref, wg_ref, wu_ref, o_ref):
    @pl.when(pl.program_id(1) < nu_ref[0])
    def _():
        x = x_ref[...]
        gate = _dot(x, wg_ref[0].astype(BF16))
        up = _dot(x, wu_ref[0].astype(BF16))
        o_ref[...] = (jax.nn.silu(gate) * up).astype(o_ref.dtype)

    @pl.when(pl.program_id(1) >= nu_ref[0])
    def _():
        o_ref[...] = jnp.zeros(o_ref.shape, o_ref.dtype)


def expert_gate_up(x_sorted, wg, wu, tile_expert, n_used, tm):
    p_rows, d = x_sorted.shape
    ff = wg.shape[2]
    tn = _pick(ff, 512, LANES)

    def w_map(j, i, te, nu):
        return (te[i], 0, j)

    grid_spec = pltpu.PrefetchScalarGridSpec(
        num_scalar_prefetch=2,
        grid=(ff // tn, p_rows // tm),
        in_specs=[
            pl.BlockSpec((tm, d), lambda j, i, te, nu: (jnp.minimum(i, nu[0] - 1), 0)),
            pl.BlockSpec((1, d, tn), w_map),
            pl.BlockSpec((1, d, tn), w_map),
        ],
        out_specs=pl.BlockSpec((tm, tn), lambda j, i, te, nu: (i, j)),
    )
    return pl.pallas_call(
        _expert_gateup_kernel,
        grid_spec=grid_spec,
        out_shape=jax.ShapeDtypeStruct((p_rows, ff), BF16),
        compiler_params=_params("parallel", "arbitrary"),
        name="expert_gate_up",
    )(tile_expert, n_used, x_sorted, wg, wu)


def _expert_down_kernel(te_ref, nu_ref, h_ref, wd_ref, o_ref):
    @pl.when(pl.program_id(0) < nu_ref[0])
    def _():
        o_ref[...] = _dot(h_ref[...], wd_ref[0])

    @pl.when(pl.program_id(0) >= nu_ref[0])
    def _():
        o_ref[...] = jnp.zeros(o_ref.shape, o_ref.dtype)


def expert_down(h_sorted, wd, tile_expert, n_used, tm):
    p_rows, ff = h_sorted.shape
    d = wd.shape[2]
    tn = _pick(d, 512, LANES)
    nj = d // tn
    grid_spec = pltpu.PrefetchScalarGridSpec(
        num_scalar_prefetch=2,
        grid=(p_rows // tm, nj),
        in_specs=[
            pl.BlockSpec((tm, ff), lambda i, j, te, nu: (jnp.minimum(i, nu[0] - 1), 0)),
            pl.BlockSpec((1, ff, tn), lambda i, j, te, nu: (te[i], 0, jnp.where(i < nu[0], j, nj - 1))),
        ],
        out_specs=pl.BlockSpec((tm, tn), lambda i, j, te, nu: (i, j)),
    )
    return pl.pallas_call(
        _expert_down_kernel,
        grid_spec=grid_spec,
        out_shape=jax.ShapeDtypeStruct((p_rows, d), F32),
        compiler_params=_params("parallel", "arbitrary"),
        name="expert_down",
    )(tile_expert, n_used, h_sorted, wd)


MOE_ROW_TILE = 512
SAMPLE_ATTN_PAGES = 8
SAMPLE_DSA_PAGES = 16


def moe_ffn(xp, xs, g, w_router, wg, wu, wd):
    n_exp = w_router.shape[1]
    d = xp.shape[1]
    tm = MOE_ROW_TILE
    zeros = jnp.zeros((1, LANES), F32)
    hn_p, gate_p, route_p, cnt_p = moe_router(xp, g, w_router, zeros)
    hn_s, gate_s, route_s, cnt = moe_router(xs, g, w_router, cnt_p)

    counts = cnt[0, :n_exp].astype(jnp.int32)
    sizes = (counts + tm - 1) // tm * tm
    ends = jnp.cumsum(sizes)
    starts = ends - sizes
    n_assign = TOP_K_EXPERTS * (xp.shape[0] + xs.shape[0])
    n_tiles = (n_assign + n_exp * (tm - 1) + tm - 1) // tm
    tile_expert = jnp.minimum(jnp.sum(jnp.arange(n_tiles)[:, None] * tm >= ends[None, :], axis=1), n_exp - 1)
    tile_expert = tile_expert.astype(jnp.int32)
    n_used = (ends[-1:] // tm).astype(jnp.int32)

    def slots(route):
        return (jnp.take(starts, route[:, 0]) + route[:, 2], jnp.take(starts, route[:, 1]) + route[:, 3])

    p1_p, p2_p = slots(route_p)
    p1_s, p2_s = slots(route_s)
    x_sorted = jnp.zeros((n_tiles * tm, d), F32)
    x_sorted = moe_dispatch(hn_p, p1_p, p2_p, x_sorted)
    x_sorted = moe_dispatch(hn_s, p1_s, p2_s, x_sorted)
    h_sorted = expert_gate_up(x_sorted.astype(BF16), wg, wu, tile_expert, n_used, tm)
    y_sorted = expert_down(h_sorted, wd.astype(BF16), tile_expert, n_used, tm)
    return (moe_combine(xp, gate_p, p1_p, p2_p, y_sorted), moe_combine(xs, gate_s, p1_s, p2_s, y_sorted))


def _sample_attn_kernel(pt_ref, lq_ref, lk_ref, g_ref, qa_ref, qi_ref, wi_ref, *rest,
                        n_steps, group, n_heads, dqk, lam_init, n_idx, d_idx):
    kp_refs, vp_refs, ip_refs = rest[:group], rest[group:2 * group], rest[2 * group:3 * group]
    kn_ref, vn_ref, in_ref, o_ref, sc_ref, scn_ref, m_ref, l_ref, acc_ref = rest[3 * group:]
    p = pl.program_id(1)
    hd = 2 * dqk
    page = kp_refs[0].shape[2]
    cols = page * n_heads
    scale = dqk ** -0.5

    q = qa_ref[0].astype(F32)
    lane = lax.broadcasted_iota(jnp.int32, q.shape, 1)
    qs = jnp.concatenate([jnp.where(lane < dqk, q, 0.0), jnp.where(lane >= dqk, q, 0.0)], axis=0)

    @pl.when(p == 0)
    def _():
        m_ref[...] = jnp.full(m_ref.shape, -jnp.inf, F32)
        l_ref[...] = jnp.zeros(l_ref.shape, F32)
        acc_ref[...] = jnp.zeros(acc_ref.shape, F32)

    k2 = jnp.concatenate([r[0, 0].reshape(cols, hd).astype(BF16) for r in kp_refs], axis=0)
    v2 = jnp.concatenate([r[0, 0].reshape(cols, hd).astype(BF16) for r in vp_refs], axis=0)
    s = _dot_nt(qs.astype(BF16), k2) * scale
    r_id = lax.broadcasted_iota(jnp.int32, s.shape, 0) & (n_heads - 1)
    c_id = lax.broadcasted_iota(jnp.int32, s.shape, 1) & (n_heads - 1)
    s = jnp.where(r_id == c_id, s, -jnp.inf)
    m_old = m_ref[...]
    m_new = jnp.maximum(m_old, jnp.max(s, axis=1, keepdims=True))
    alpha = jnp.exp(m_old - m_new)
    pr = jnp.exp(s - m_new)
    l_ref[...] = alpha * l_ref[...] + jnp.sum(pr, axis=1, keepdims=True)
    acc_ref[...] = alpha * acc_ref[...] + _dot(pr.astype(BF16), v2)
    m_ref[...] = m_new

    wscale = n_idx ** -0.5 * d_idx ** -0.5
    qi = qi_ref[0]
    w = wi_ref[0] * wscale
    for k, ip_ref in enumerate(ip_refs):
        si = jnp.maximum(_dot(qi, ip_ref[0, 0]), 0.0) * w
        sc_ref[0, k:k + 1, :] = jnp.sum(si, axis=0, keepdims=True)

    @pl.when(p == n_steps - 1)
    def _():
        kn = jnp.concatenate([kn_ref[0], kn_ref[0]], axis=0)
        vn = jnp.concatenate([vn_ref[0], vn_ref[0]], axis=0)
        s_new = jnp.sum(qs * kn, axis=1, keepdims=True) * scale
        m_o = m_ref[...]
        m_n = jnp.maximum(m_o, s_new)
        a = jnp.exp(m_o - m_n)
        p_new = jnp.exp(s_new - m_n)
        l_fin = a * l_ref[...] + p_new
        o_all = (a * acc_ref[...] + p_new * vn) / l_fin
        lam = _lambda(lq_ref, lk_ref, lam_init)
        o = o_all[:n_heads] - lam * o_all[n_heads:]
        o_ref[0] = (_rms(o, g_ref[...]) * (1.0 - lam_init)).astype(o_ref.dtype)
        kin = in_ref[0]
        s_in = jnp.maximum(jnp.sum(qi.astype(F32) * kin, axis=1, keepdims=True), 0.0) * w
        scn_ref[0] = jnp.broadcast_to(jnp.sum(s_in, axis=0, keepdims=True), (1, LANES))


def sample_attention(page_table, layer, cache_k, cache_v, cache_i_t, qa, qi, wi, k_new, v_new, i_new, lq, lk, g,
                     *, n_heads, dqk, lam_init, n_idx, d_idx):
    db, n_pages = page_table.shape
    width = qa.shape[1]
    page = cache_k.shape[2]
    hd = 2 * dqk
    rows = 2 * n_heads
    assert n_heads & (n_heads - 1) == 0
    pt = page_table.reshape(-1)
    group = SAMPLE_ATTN_PAGES if n_pages % SAMPLE_ATTN_PAGES == 0 else 1
    n_steps = n_pages // group

    def pool5(k):
        return lambda b, p, pt_ref: (layer, pt_ref[b * n_pages + p * group + k], 0, 0, 0)

    def pool4(k):
        return lambda b, p, pt_ref: (layer, pt_ref[b * n_pages + p * group + k], 0, 0)

    row3 = lambda b, p, pt_ref: (b, 0, 0)
    const2 = lambda b, p, pt_ref: (0, 0)
    grid_spec = pltpu.PrefetchScalarGridSpec(
        num_scalar_prefetch=1,
        grid=(db, n_steps),
        in_specs=[
            pl.BlockSpec(lq.shape, const2),
            pl.BlockSpec(lk.shape, const2),
            pl.BlockSpec((1, hd), const2),
            pl.BlockSpec((1, n_heads, hd), row3),
            pl.BlockSpec((1, n_idx, d_idx), row3),
            pl.BlockSpec((1, n_idx, 1), row3),
            *[pl.BlockSpec((1, 1, page, n_heads, hd), pool5(k)) for k in range(group)],
            *[pl.BlockSpec((1, 1, page, n_heads, hd), pool5(k)) for k in range(group)],
            *[pl.BlockSpec((1, 1, d_idx, page), pool4(k)) for k in range(group)],
            pl.BlockSpec((1, n_heads, hd), row3),
            pl.BlockSpec((1, n_heads, hd), row3),
            pl.BlockSpec((1, 1, d_idx), row3),
        ],
        out_specs=[
            pl.BlockSpec((1, n_heads, hd), row3),
            pl.BlockSpec((1, group, page), lambda b, p, pt_ref: (b * n_steps + p, 0, 0)),
            pl.BlockSpec((1, 1, LANES), row3),
        ],
        scratch_shapes=[pltpu.VMEM((rows, 1), F32), pltpu.VMEM((rows, 1), F32), pltpu.VMEM((rows, hd), F32)],
    )
    out, scores, score_new = pl.pallas_call(
        functools.partial(_sample_attn_kernel, n_steps=n_steps, group=group, n_heads=n_heads, dqk=dqk,
                          lam_init=lam_init, n_idx=n_idx, d_idx=d_idx),
        grid_spec=grid_spec,
        out_shape=[jax.ShapeDtypeStruct((db, n_heads, hd), F32),
                   jax.ShapeDtypeStruct((db * n_steps, group, page), F32),
                   jax.ShapeDtypeStruct((db, 1, LANES), F32)],
        compiler_params=_params("parallel", "arbitrary"),
        name="sample_attention",
    )(pt, lq, lk, g.reshape(1, hd), qa.reshape(db, n_heads, hd), qi.reshape(db, n_idx, d_idx),
      wi.reshape(db, n_idx, 1), *([cache_k] * group), *([cache_v] * group), *([cache_i_t] * group),
      k_new.reshape(db, n_heads, hd), v_new.reshape(db, n_heads, hd), i_new.reshape(db, 1, d_idx))
    return out.reshape(db, width), scores.reshape(db, n_pages, page), score_new


def _count_all(mask):
    c = jnp.sum(mask.astype(jnp.int32), axis=1, keepdims=True)
    return jnp.sum(c, axis=0, keepdims=True)


def _sample_dsa_kernel(pt_ref, sc_ref, scn_ref, qb_ref, *rest, n_pages, group, page, n_heads, n_kv, dh, topk):
    kp_refs, vp_refs = rest[:group], rest[group:2 * group]
    kn_ref, vn_ref, o_ref, key_ref, keyn_ref, t_ref, cut_ref, selx_ref, m_ref, l_ref, acc_ref = rest[2 * group:]
    p = pl.program_id(1)
    rep = n_heads // n_kv
    scale = dh ** -0.5
    past = n_pages * page
    cols = page * n_kv
    kv_shift = n_kv.bit_length() - 1
    rep_shift = rep.bit_length() - 1

    @pl.when(p == 0)
    def _():
        m_ref[...] = jnp.full(m_ref.shape, -jnp.inf, F32)
        l_ref[...] = jnp.zeros(l_ref.shape, F32)
        acc_ref[...] = jnp.zeros(acc_ref.shape, F32)
        key_ref[...] = _order_key(sc_ref[0])
        keyn_ref[...] = _order_key(scn_ref[0])
        key_new = keyn_ref[:, 0:1]

        def cnt(t):
            return _count_all(key_ref[...] >= t) + (key_new >= t).astype(jnp.int32)

        t0 = jnp.full((1, 1), INT_MIN, jnp.int32)
        zero = jnp.zeros((1, 1), jnp.int32)
        t0 = jnp.where(cnt(zero) >= topk, zero, t0)

        def body(it, t):
            cand = t | (jnp.int32(1) << (30 - it))
            return jnp.where(cnt(cand) >= topk, cand, t)

        t = lax.fori_loop(0, 31, body, t0)
        t_ref[...] = t
        key = key_ref[...]
        n_gt = _count_all(key > t) + (key_new > t).astype(jnp.int32)
        need = topk - n_gt
        pos = (lax.broadcasted_iota(jnp.int32, key.shape, 0) * page
               + lax.broadcasted_iota(jnp.int32, key.shape, 1))
        n_bits = int(math.ceil(math.log2(past + 1)))

        def body2(it, x):
            cand = x | (jnp.int32(1) << (n_bits - 1 - it))
            c = _count_all((key_ref[...] == t) & (pos < cand)) + ((key_new == t) & (past < cand)).astype(jnp.int32)
            return jnp.where(c < need, cand, x)

        cut = lax.fori_loop(0, n_bits, body2, jnp.zeros((1, 1), jnp.int32))
        cut_ref[...] = cut
        sel = (key > t) | ((key == t) & (pos <= cut))
        e_t = lax.broadcasted_iota(jnp.int32, (page, cols), 0)
        e_c = lax.broadcasted_iota(jnp.int32, (page, cols), 1)
        spread = jnp.where((e_c >> kv_shift) == e_t, 1.0, 0.0).astype(BF16)
        selx_ref[...] = _dot(jnp.where(sel, 1.0, 0.0).astype(BF16), spread)

    t = t_ref[...]
    cut = cut_ref[...]
    q = qb_ref[0]

    k2 = jnp.concatenate([r[0, 0].astype(BF16) for r in kp_refs], axis=0)
    v2 = jnp.concatenate([r[0, 0].astype(BF16) for r in vp_refs], axis=0)
    sel_cols = jnp.concatenate(
        [jnp.broadcast_to(selx_ref[pl.ds(p * group + k, 1), :], (n_heads, cols)) for k in range(group)], axis=1)
    r_id = lax.broadcasted_iota(jnp.int32, (n_heads, group * cols), 0)
    c_id = lax.broadcasted_iota(jnp.int32, (n_heads, group * cols), 1)
    keep = (sel_cols > 0.5) & ((c_id & (n_kv - 1)) == (r_id >> rep_shift))
    s = _dot_nt(q.astype(BF16), k2) * scale
    s = jnp.where(keep, s, -jnp.inf)
    m_old = m_ref[...]
    m_new = jnp.maximum(m_old, jnp.max(s, axis=1, keepdims=True))
    m_safe = jnp.where(m_new == -jnp.inf, 0.0, m_new)
    alpha = jnp.exp(m_old - m_safe)
    pr = jnp.exp(s - m_safe)
    l_ref[...] = alpha * l_ref[...] + jnp.sum(pr, axis=1, keepdims=True)
    acc_ref[...] = alpha * acc_ref[...] + _dot(pr.astype(BF16), v2)
    m_ref[...] = m_new

    @pl.when(p == n_pages // group - 1)
    def _():
        key_new = keyn_ref[:, 0:1]
        sel_new = (key_new > t) | ((key_new == t) & (past <= cut))
        h_id = lax.broadcasted_iota(jnp.int32, (n_heads, dh), 0) >> rep_shift
        kn = jnp.zeros((n_heads, dh), F32)
        vn = jnp.zeros((n_heads, dh), F32)
        for g in range(n_kv):
            kn = jnp.where(h_id == g, kn_ref[0, g:g + 1, :], kn)
            vn = jnp.where(h_id == g, vn_ref[0, g:g + 1, :], vn)
        s_new = jnp.sum(q.astype(F32) * kn, axis=1, keepdims=True) * scale
        s_new = jnp.where(sel_new, s_new, -jnp.inf)
        m_o = m_ref[...]
        m_n = jnp.maximum(m_o, s_new)
        m_s = jnp.where(m_n == -jnp.inf, 0.0, m_n)
        a = jnp.exp(m_o - m_s)
        p_new = jnp.exp(s_new - m_s)
        l_fin = a * l_ref[...] + p_new
        o_ref[0] = ((a * acc_ref[...] + p_new * vn) / l_fin).astype(o_ref.dtype)


def sample_dsa(page_table, layer, cache_k, cache_v, scores, score_new, qb, k_new, v_new,
               *, n_heads, n_kv, dh, topk):
    db, n_pages = page_table.shape
    cols = cache_k.shape[2]
    page = cols // n_kv
    rep = n_heads // n_kv
    assert n_kv & (n_kv - 1) == 0 and rep & (rep - 1) == 0
    pt = page_table.reshape(-1)

    group = SAMPLE_DSA_PAGES if n_pages % SAMPLE_DSA_PAGES == 0 else 1

    def pool_map(k):
        return lambda b, p, pt_ref: (layer, pt_ref[b * n_pages + p * group + k], 0, 0)

    row3 = lambda b, p, pt_ref: (b, 0, 0)
    grid_spec = pltpu.PrefetchScalarGridSpec(
        num_scalar_prefetch=1,
        grid=(db, n_pages // group),
        in_specs=[
            pl.BlockSpec((1, n_pages, page), row3),
            pl.BlockSpec((1, 1, LANES), row3),
            pl.BlockSpec((1, n_heads, dh), row3),
            *[pl.BlockSpec((1, 1, cols, dh), pool_map(k)) for k in range(group)],
            *[pl.BlockSpec((1, 1, cols, dh), pool_map(k)) for k in range(group)],
            pl.BlockSpec((1, n_kv, dh), row3),
            pl.BlockSpec((1, n_kv, dh), row3),
        ],
        out_specs=pl.BlockSpec((1, n_heads, dh), row3),
        scratch_shapes=[pltpu.VMEM((n_pages, page), jnp.int32), pltpu.VMEM((1, LANES), jnp.int32),
                        pltpu.VMEM((1, 1), jnp.int32), pltpu.VMEM((1, 1), jnp.int32),
                        pltpu.VMEM((n_pages, cols), F32),
                        pltpu.VMEM((n_heads, 1), F32), pltpu.VMEM((n_heads, 1), F32),
                        pltpu.VMEM((n_heads, dh), F32)],
    )
    out = pl.pallas_call(
        functools.partial(_sample_dsa_kernel, n_pages=n_pages, group=group, page=page, n_heads=n_heads,
                          n_kv=n_kv, dh=dh, topk=topk),
        grid_spec=grid_spec,
        out_shape=jax.ShapeDtypeStruct((db, n_heads, dh), F32),
        compiler_params=_params("parallel", "arbitrary"),
        name="sample_dsa",
    )(pt, scores, score_new, qb.reshape(db, n_heads, dh), *([cache_k] * group), *([cache_v] * group),
      k_new.reshape(db, n_kv, dh), v_new.reshape(db, n_kv, dh))
    return out.reshape(db, n_heads * dh)


def kernel(x_prompt, x_sample, cache_a_k, cache_a_v, cache_b_k, cache_b_v, cache_idx_k, page_table, w_in, lambda_q, lambda_k, subln_g, w_branch_a, w_branch_b, w_out, norm_mix_g, norm_ffn_g, w_dense_gate, w_dense_up, w_dense_down, w_router, w_exp_gate, w_exp_up, w_exp_down, norm_final_g):
    batch, seq, d_model = x_prompt.shape
    db, dec_seq, _ = x_sample.shape
    assert dec_seq == 1, "one new token per sample row"
    depth, n_pool, page, n_ha, a_width = cache_a_k.shape
    assert page == PAGE_SIZE
    dqk = a_width // 2
    dv = cache_a_v.shape[4]
    n_kv, dh = cache_b_k.shape[3:]
    d_idx = cache_idx_k.shape[3]
    n_hb = w_branch_b.shape[1] // dh
    n_in = w_in.shape[2]
    wa, wva, wqb, wkb = n_ha * 2 * dqk, n_ha * dv, n_hb * dh, n_kv * dh
    n_idx = (n_in - (2 * wa + wva + wqb + 2 * wkb + d_idx + 2 * d_model)) // (d_idx + 1)
    assert dv == 2 * dqk == LANES and dh == LANES and 2 * d_idx == LANES
    n_pages = page_table.shape[1]
    past_len = n_pages * page
    topk_p = min(TOPK_MAX, seq // 4)
    topk_s = min(TOPK_MAX, (past_len + dec_seq) // 4)
    n_exp = w_router.shape[2]

    offs = [0]
    for wdt in (wa, wa, wva, wqb, wkb, wkb, n_idx * d_idx, d_idx, n_idx, d_model, d_model):
        offs.append(offs[-1] + wdt)
    o_qa, o_ka, o_va, o_qb, o_kb, o_vb, o_qi, o_ki, o_wi, o_ga, o_gb, _ = offs

    pos_p = jnp.arange(seq, dtype=F32)
    pos_s = jnp.full((db,), float(past_len), F32)
    tabs = {}
    for name, pos in (("p", pos_p), ("s", pos_s)):
        tabs[name, "a"] = rope_table(pos, dqk)
        tabs[name, "b"] = rope_table(pos, dh)
        tabs[name, "i"] = rope_table(pos, d_idx, valid_lanes=d_idx)

    ck_b = cache_b_k.reshape(depth, n_pool, page * n_kv, dh)
    cv_b = cache_b_v.reshape(depth, n_pool, page * n_kv, dh)
    ci_t = jnp.swapaxes(cache_idx_k, 2, 3)

    xp = x_prompt.reshape(batch * seq, d_model)
    xs = x_sample.reshape(db * dec_seq, d_model)
    rows = {"p": [[] for _ in range(5)], "s": [[] for _ in range(5)]}

    for l in range(depth):
        lam_init = 0.8 - 0.6 * math.exp(-0.3 * l)
        lq, lk = lambda_q[l], lambda_k[l]
        mix_w = {}
        for grp, dt in (("p", BF16), ("s", F32)):
            wl = w_in[l].astype(dt)
            mix_w[grp] = dict(
                wl=wl, kiwi=jnp.pad(wl[:, o_ki:o_ga], ((0, 0), (0, LANES - d_idx - n_idx))),
                pa=w_branch_a[l].astype(dt), pb=w_branch_b[l].astype(dt), o=w_out[l].astype(dt))

        def in_proj(x, grp):
            act = BF16 if grp == "p" else F32
            wl = mix_w[grp]["wl"]
            xn = rmsnorm(x, norm_mix_g[l], act)
            ta, ha = tabs[grp, "a"]
            tb, hb = tabs[grp, "b"]
            ti, hi = tabs[grp, "i"]

            def kv_pair(w, tab=None, half=0):
                outs = project(xn, w, [F32] if act == F32 else [F32, BF16], tab, half)
                return outs[0], outs[-1]

            z = {}
            z["qa"], = project(xn, wl[:, o_qa:o_ka], [act], ta, ha)
            z["ka32"], z["ka"] = kv_pair(wl[:, o_ka:o_va], ta, ha)
            z["va32"], z["va"] = kv_pair(wl[:, o_va:o_qb])
            z["qb"], = project(xn, wl[:, o_qb:o_kb], [act], tb, hb)
            z["kb32"], z["kb"] = kv_pair(wl[:, o_kb:o_vb], tb, hb)
            z["vb32"], z["vb"] = kv_pair(wl[:, o_vb:o_qi])
            z["qi"], = project(xn, wl[:, o_qi:o_ki], [act], ta, ha)
            z["kiwi"], = project(xn, mix_w[grp]["kiwi"], [F32], ti, hi)
            z["ga"], = project(xn, wl[:, o_ga:o_gb], [F32])
            z["gb"], = project(xn, wl[:, o_gb:], [F32])
            return z

        def mix_out(x, z, oa, ob, grp):
            w = mix_w[grp]
            u = merge_branches(oa, ob, w["pa"], w["pb"], z["ga"], z["gb"])
            return matmul_residual(u, w["o"], x)

        zp = in_proj(xp, "p")
        oa = diff_attention_prompt(zp["qa"], zp["ka"], zp["va"].T, lq, lk, subln_g[l], batch=batch, seq=seq,
                                   n_heads=n_ha, dqk=dqk, lam_init=lam_init)
        ob = dsa_prompt(zp["qi"], zp["kiwi"], zp["qb"], zp["kb"], zp["vb"].T, batch=batch, seq=seq, n_idx=n_idx,
                        d_idx=d_idx, n_heads=n_hb, n_kv=n_kv, dh=dh, topk=topk_p)
        xp = mix_out(xp, zp, oa, ob, "p")

        zs = in_proj(xs, "s")
        ki_s = zs["kiwi"][:, :d_idx]
        wi_s = zs["kiwi"][:, d_idx:d_idx + n_idx]
        oa, scores, score_new = sample_attention(
            page_table, l, cache_a_k, cache_a_v, ci_t, zs["qa"], zs["qi"], wi_s, zs["ka32"], zs["va32"], ki_s,
            lq, lk, subln_g[l], n_heads=n_ha, dqk=dqk, lam_init=lam_init, n_idx=n_idx, d_idx=d_idx)
        ob = sample_dsa(page_table, l, ck_b, cv_b, scores, score_new, zs["qb"], zs["kb32"], zs["vb32"],
                        n_heads=n_hb, n_kv=n_kv, dh=dh, topk=topk_s)
        xs = mix_out(xs, zs, oa, ob, "s")

        for grp, z, bt, tt in (("p", zp, batch, seq), ("s", zs, db, dec_seq)):
            rows[grp][0].append(z["ka32"].reshape(bt, tt, n_ha, 2 * dqk))
            rows[grp][1].append(z["va32"].reshape(bt, tt, n_ha, dv))
            rows[grp][2].append(z["kb32"].reshape(bt, tt, n_kv, dh))
            rows[grp][3].append(z["vb32"].reshape(bt, tt, n_kv, dh))
            rows[grp][4].append(z["kiwi"][:, :d_idx].reshape(bt, tt, d_idx))

        i = l // 2
        if l % 2 == 0:
            hp = rmsnorm(xp, norm_ffn_g[l], BF16)
            hs = rmsnorm(xs, norm_ffn_g[l], F32)
            xp = matmul_residual(gate_up(hp, w_dense_gate[i], w_dense_up[i]), w_dense_down[i].astype(BF16), xp)
            xs = matmul_residual(gate_up(hs, w_dense_gate[i], w_dense_up[i]), w_dense_down[i], xs)
        else:
            xp, xs = moe_ffn(xp, xs, norm_ffn_g[l], w_router[i], w_exp_gate[i], w_exp_up[i], w_exp_down[i])

    y_prompt = rmsnorm(xp, norm_final_g, F32).reshape(batch, seq, d_model)
    y_sample = rmsnorm(xs, norm_final_g, F32).reshape(db, dec_seq, d_model)
    outs_p = [jnp.stack(r, axis=0) for r in rows["p"]]
    outs_s = [jnp.stack(r, axis=0) for r in rows["s"]]
    return (y_prompt, y_sample, *outs_p, *outs_s)
```

```python
import functools
import math

import jax
import jax.numpy as jnp
from jax import lax
from jax.experimental import pallas as pl
from jax.experimental.pallas import tpu as pltpu

LANES = 128
VMEM_LIMIT_BYTES = 56 * 1024 * 1024
ROPE_THETA = 500000.0
ROPE_FRAC = 4
RMS_EPS = 1e-6
TOPK_MAX = 256
PAGE_SIZE = 128
TOP_K_EXPERTS = 2
INT_MIN = -2 ** 31
LOG2_E = math.log2(math.e)
F32 = jnp.float32
BF16 = jnp.bfloat16


def _pick(n, pref, mult):
    if n <= pref:
        return n
    t = (pref // mult) * mult
    while t >= mult:
        if n % t == 0:
            return t
        t -= mult
    return n


def _params(*sem):
    return pltpu.CompilerParams(dimension_semantics=sem, vmem_limit_bytes=VMEM_LIMIT_BYTES)


def _precision(a, b):
    return lax.Precision.HIGHEST if a.dtype == F32 and b.dtype == F32 else None


def _dot(a, b):
    return jnp.dot(a, b, preferred_element_type=F32, precision=_precision(a, b))


def _dot_nt(a, b):
    return lax.dot_general(a, b, (((1,), (1,)), ((), ())), preferred_element_type=F32,
                           precision=_precision(a, b))


def _rms(x, g):
    return x * lax.rsqrt(jnp.mean(x * x, axis=-1, keepdims=True) + RMS_EPS) * g


def _rmsnorm_kernel(x_ref, g_ref, o_ref):
    o_ref[...] = _rms(x_ref[...], g_ref[...]).astype(o_ref.dtype)


def rmsnorm(x, g, out_dtype):
    m, d = x.shape
    tm = _pick(m, 512, 8)
    return pl.pallas_call(
        _rmsnorm_kernel,
        grid=(m // tm,),
        in_specs=[pl.BlockSpec((tm, d), lambda i: (i, 0)), pl.BlockSpec((1, d), lambda i: (0, 0))],
        out_specs=pl.BlockSpec((tm, d), lambda i: (i, 0)),
        out_shape=jax.ShapeDtypeStruct((m, d), out_dtype),
        compiler_params=_params("parallel"),
        name="rmsnorm",
    )(x, g.reshape(1, d))


def _proj_kernel(*refs, half, tn):
    if half:
        x_ref, w_ref, tab_ref, *o_refs = refs
    else:
        x_ref, w_ref, *o_refs = refs
    z = _dot(x_ref[...], w_ref[...])
    if not half:
        for o in o_refs:
            o[...] = z.astype(o.dtype)
        return
    c, s_lo, s_hi = tab_ref[0], tab_ref[1], tab_ref[2]
    for g in range(tn // LANES):
        zg = z[:, g * LANES:(g + 1) * LANES]
        r = zg * c + pltpu.roll(zg, LANES - half, 1) * s_lo + pltpu.roll(zg, half, 1) * s_hi
        for o in o_refs:
            o[:, g * LANES:(g + 1) * LANES] = r.astype(o.dtype)


def project(xn, w, out_dtypes, tab=None, half=0, tn_pref=1024):
    m, k = xn.shape
    n = w.shape[1]
    p_rows = m if tab is None else tab.shape[1]
    assert m % p_rows == 0
    tm = _pick(p_rows, 1024, 8)
    tn = _pick(n, tn_pref, LANES)
    in_specs = [pl.BlockSpec((tm, k), lambda i, j: (i, 0)), pl.BlockSpec((k, tn), lambda i, j: (0, j))]
    args = [xn, w]
    if tab is not None:
        nblk = p_rows // tm
        in_specs.append(pl.BlockSpec((3, tm, LANES), lambda i, j: (0, i % nblk, 0)))
        args.append(tab)
    outs = pl.pallas_call(
        functools.partial(_proj_kernel, half=half if tab is not None else 0, tn=tn),
        grid=(m // tm, n // tn),
        in_specs=in_specs,
        out_specs=[pl.BlockSpec((tm, tn), lambda i, j: (i, j)) for _ in out_dtypes],
        out_shape=[jax.ShapeDtypeStruct((m, n), dt) for dt in out_dtypes],
        compiler_params=_params("parallel", "arbitrary"),
        name="in_proj",
    )(*args)
    return outs


def rope_table(pos, head_dim, valid_lanes=LANES):
    rot = head_dim // ROPE_FRAC
    half = rot // 2
    inv_freq = jnp.power(ROPE_THETA, -jnp.arange(half, dtype=F32) * 2.0 / rot)
    ang = pos[:, None] * inv_freq[None, :]
    cos, sin = jnp.cos(ang), jnp.sin(ang)
    n = pos.shape[0]
    ones = jnp.ones((n, head_dim - rot), F32)
    zeros_h = jnp.zeros((n, half), F32)
    zeros_r = jnp.zeros((n, head_dim - rot), F32)
    c = jnp.concatenate([cos, cos, ones], axis=1)
    s_lo = jnp.concatenate([-sin, zeros_h, zeros_r], axis=1)
    s_hi = jnp.concatenate([zeros_h, sin, zeros_r], axis=1)
    reps = LANES // head_dim
    tabs = [jnp.tile(t, (1, reps)) for t in (c, s_lo, s_hi)]
    if valid_lanes < LANES:
        lane = jnp.arange(LANES)[None, :]
        tabs = [jnp.where(lane < valid_lanes, tabs[0], 1.0), jnp.where(lane < valid_lanes, tabs[1], 0.0),
                jnp.where(lane < valid_lanes, tabs[2], 0.0)]
    return jnp.stack(tabs, axis=0), half


def _lambda(lq_ref, lk_ref, lam_init):
    lq = lq_ref[...].astype(F32)
    lk = lk_ref[...].astype(F32)
    prod = lq * lk
    return (jnp.exp(jnp.sum(prod[0:1, :], axis=1, keepdims=True))
            - jnp.exp(jnp.sum(prod[1:2, :], axis=1, keepdims=True)) + lam_init)


def _diffattn_kernel(lq_ref, lk_ref, g_ref, q_ref, k_ref, vt_ref, o_ref, m_ref, l_ref, acc_ref,
                     *, tq, dqk, lam_init):
    i = pl.program_id(2)
    c_exp = dqk ** -0.5 * LOG2_E
    q = q_ref[...].astype(F32)
    lane = lax.broadcasted_iota(jnp.int32, q.shape, 1)
    qs = jnp.concatenate([jnp.where(lane < dqk, q, 0.0), jnp.where(lane >= dqk, q, 0.0)], axis=0)
    qs = qs.astype(q_ref.dtype)
    m_ref[...] = jnp.full(m_ref.shape, -jnp.inf, F32)
    l_ref[...] = jnp.zeros(l_ref.shape, F32)
    acc_ref[...] = jnp.zeros(acc_ref.shape, F32)

    def step(j, masked):
        start = pl.multiple_of(j * tq, tq)
        kt = k_ref[pl.ds(start, tq), :]
        vt = vt_ref[:, pl.ds(start, tq)]
        s = _dot_nt(kt, qs) * c_exp
        if masked:
            key = lax.broadcasted_iota(jnp.int32, s.shape, 0)
            qry = lax.broadcasted_iota(jnp.int32, s.shape, 1) & (tq - 1)
            s = jnp.where(key <= qry, s, -jnp.inf)
        m_old = m_ref[...]
        m_new = jnp.maximum(m_old, jnp.max(s, axis=0, keepdims=True))
        alpha = jnp.exp2(m_old - m_new)
        p = jnp.exp2(s - m_new)
        l_ref[...] = alpha * l_ref[...] + jnp.sum(p, axis=0, keepdims=True)
        acc_ref[...] = alpha * acc_ref[...] + _dot(vt, p.astype(vt.dtype))
        m_ref[...] = m_new

    def body(j, carry):
        step(j, False)
        return carry

    lax.fori_loop(0, i, body, 0)
    step(i, True)

    lam = _lambda(lq_ref, lk_ref, lam_init)
    o_all = acc_ref[...] / l_ref[...]
    o = o_all[:, :tq] - lam * o_all[:, tq:]
    inv = lax.rsqrt(jnp.mean(o * o, axis=0, keepdims=True) + RMS_EPS)
    y = o * inv * g_ref[...] * (1.0 - lam_init)
    o_ref[...] = y.T.astype(o_ref.dtype)


def diff_attention_prompt(q, k, v_t, lq, lk, g, *, batch, seq, n_heads, dqk, lam_init):
    m, width = q.shape
    hd = width // n_heads
    tq = _pick(seq, 512, LANES)
    assert tq & (tq - 1) == 0
    nq = seq // tq
    return pl.pallas_call(
        functools.partial(_diffattn_kernel, tq=tq, dqk=dqk, lam_init=lam_init),
        grid=(batch, n_heads, nq),
        in_specs=[
            pl.BlockSpec(lq.shape, lambda b, h, i: (0, 0)),
            pl.BlockSpec(lk.shape, lambda b, h, i: (0, 0)),
            pl.BlockSpec((hd, 1), lambda b, h, i: (0, 0)),
            pl.BlockSpec((tq, hd), lambda b, h, i: (b * nq + i, h)),
            pl.BlockSpec((seq, hd), lambda b, h, i: (b, h)),
            pl.BlockSpec((hd, seq), lambda b, h, i: (h, b)),
        ],
        out_specs=pl.BlockSpec((tq, hd), lambda b, h, i: (b * nq + i, h)),
        out_shape=jax.ShapeDtypeStruct((m, width), BF16),
        scratch_shapes=[pltpu.VMEM((1, 2 * tq), F32), pltpu.VMEM((1, 2 * tq), F32), pltpu.VMEM((hd, 2 * tq), F32)],
        compiler_params=_params("parallel", "parallel", "arbitrary"),
        name="diff_attn_prompt",
    )(lq, lk, g.reshape(hd, 1), q, k, v_t)


def _order_key(score):
    score = jnp.where(score == 0.0, 0.0, score)
    bits = lax.bitcast_convert_type(score, jnp.int32)
    return jnp.where(bits < 0, bits ^ jnp.int32(0x7FFFFFFF), bits)


def _count_rows(mask):
    return jnp.sum(mask.astype(jnp.int32), axis=1, keepdims=True)


def _kth_largest_key(load_keys, shape, k, count_fn):
    t0 = jnp.full(shape, INT_MIN, jnp.int32)
    zero = jnp.zeros(shape, jnp.int32)
    t0 = jnp.where(count_fn(load_keys() >= zero) >= k, zero, t0)

    def body(it, t):
        cand = t | (jnp.int32(1) << (30 - it))
        return jnp.where(count_fn(load_keys() >= cand) >= k, cand, t)

    return lax.fori_loop(0, 31, body, t0)


def _tie_cut(load_eq_pos, shape, need, n_bits, count_fn):
    def body(it, x):
        cand = x | (jnp.int32(1) << (n_bits - 1 - it))
        eq, pos = load_eq_pos()
        return jnp.where(count_fn(eq & (pos < cand)) < need, cand, x)

    return lax.fori_loop(0, n_bits, body, jnp.zeros(shape, jnp.int32))


def _dsa_prompt_kernel(qi_ref, kiwi_k_ref, kiwi_q_ref, qb_ref, kb_ref, vb_ref, o_ref,
                       kdup_ref, wb_ref, score_ref, key_ref, bias_ref, cut_ref,
                       *, tq, seq, extents, n_idx, d_idx, n_heads, n_kv, dh, topk):
    i = pl.program_id(1)

    @pl.when(i == 0)
    def _():
        kf = kiwi_k_ref[...]
        lane = lax.broadcasted_iota(jnp.int32, kf.shape, 1)
        klo = jnp.where(lane < d_idx, kf, 0.0)
        kdup_ref[...] = (klo + pltpu.roll(klo, d_idx, 1)).astype(kdup_ref.dtype)

    wscale = n_idx ** -0.5 * d_idx ** -0.5
    c_exp = dh ** -0.5 * LOG2_E
    rep_shift = (n_heads // n_kv).bit_length() - 1

    w_all = kiwi_q_ref[...] * wscale
    for h in range(n_idx):
        wb_ref[h] = jnp.broadcast_to(w_all[:, d_idx + h:d_idx + h + 1], (tq, LANES))

    def window(j, width):
        return pl.ds(pl.multiple_of(j * width, width), width)

    def run(ncol):
        score_ref[:, :ncol] = jnp.zeros((tq, ncol), F32)

        def head_pair(j, carry):
            grp = qi_ref[:, window(j, LANES)].astype(F32)
            lane = lax.broadcasted_iota(jnp.int32, grp.shape, 1)
            for half in range(2):
                keep = (lane < d_idx) if half == 0 else (lane >= d_idx)
                qh = jnp.where(keep, grp, 0.0).astype(qi_ref.dtype)
                w = wb_ref[2 * j + half][:, 0:1]
                score_ref[:, :ncol] += jnp.maximum(_dot_nt(qh, kdup_ref[:ncol, :]), 0.0) * w
            return carry

        lax.fori_loop(0, n_idx // 2, head_pair, 0)

        row = i * tq + lax.broadcasted_iota(jnp.int32, (tq, ncol), 0)
        col = lax.broadcasted_iota(jnp.int32, (tq, ncol), 1)
        causal = col <= row
        key_ref[:, :ncol] = jnp.where(causal, _order_key(score_ref[:, :ncol]), INT_MIN)

        t = _kth_largest_key(lambda: key_ref[:, :ncol], (tq, 1), topk, _count_rows)
        key = key_ref[:, :ncol]
        eq = key == t
        need = topk - _count_rows(key > t)
        tie = (_count_rows(eq) > need) & (t > INT_MIN)
        cut_ref[...] = jnp.full((tq, 1), ncol, jnp.int32)

        @pl.when(jnp.max(tie.astype(jnp.int32)) > 0)
        def _():
            def load():
                return key_ref[:, :ncol] == t, lax.broadcasted_iota(jnp.int32, (tq, ncol), 1)
            cut_ref[...] = _tie_cut(load, (tq, 1), need, int(math.log2(seq)), _count_rows)

        sel = ((key > t) | (eq & (col <= cut_ref[...]))) & causal
        bias_ref[:, :ncol] = jnp.where(sel, 0.0, -jnp.inf)

        def head(h, carry):
            g = h >> rep_shift
            s = _dot_nt(qb_ref[:, window(h, dh)], kb_ref[:ncol, window(g, dh)]) * c_exp + bias_ref[:, :ncol]
            p = jnp.exp2(s - jnp.max(s, axis=1, keepdims=True))
            l = jnp.sum(p, axis=1, keepdims=True)
            o = _dot(p.astype(vb_ref.dtype), vb_ref[:ncol, window(g, dh)]) / l
            o_ref[:, window(h, dh)] = o.astype(o_ref.dtype)
            return carry

        lax.fori_loop(0, n_heads, head, 0)

    lo = 0
    for ncol in extents:
        @pl.when(((i + 1) * tq > lo) & ((i + 1) * tq <= ncol))
        def _(ncol=ncol):
            run(ncol)
        lo = ncol


def dsa_prompt(qi, kiwi, qb, kb, vb, *, batch, seq, n_idx, d_idx, n_heads, n_kv, dh, topk):
    m = qi.shape[0]
    rep = n_heads // n_kv
    assert 2 * d_idx == LANES and dh == LANES and seq & (seq - 1) == 0 and rep & (rep - 1) == 0 and n_idx % 2 == 0
    tq = _pick(seq, 256, LANES)
    nq = seq // tq
    n_ext = min(4, nq)
    extents = tuple(seq * (c + 1) // n_ext for c in range(n_ext))
    return pl.pallas_call(
        functools.partial(_dsa_prompt_kernel, tq=tq, seq=seq, extents=extents, n_idx=n_idx, d_idx=d_idx,
                          n_heads=n_heads,
                          n_kv=n_kv, dh=dh, topk=topk),
        grid=(batch, nq),
        in_specs=[
            pl.BlockSpec((tq, qi.shape[1]), lambda b, i: (b * nq + i, 0)),
            pl.BlockSpec((seq, LANES), lambda b, i: (b, 0)),
            pl.BlockSpec((tq, LANES), lambda b, i: (b * nq + i, 0)),
            pl.BlockSpec((tq, qb.shape[1]), lambda b, i: (b * nq + i, 0)),
            pl.BlockSpec((seq, kb.shape[1]), lambda b, i: (b, 0)),
            pl.BlockSpec((seq, vb.shape[1]), lambda b, i: (b, 0)),
        ],
        out_specs=pl.BlockSpec((tq, qb.shape[1]), lambda b, i: (b * nq + i, 0)),
        out_shape=jax.ShapeDtypeStruct((m, qb.shape[1]), BF16),
        scratch_shapes=[pltpu.VMEM((seq, LANES), BF16), pltpu.VMEM((n_idx, tq, LANES), F32),
                        pltpu.VMEM((tq, seq), F32), pltpu.VMEM((tq, seq), jnp.int32),
                        pltpu.VMEM((tq, seq), F32), pltpu.VMEM((tq, 1), jnp.int32)],
        compiler_params=_params("parallel", "arbitrary"),
        name="dsa_prompt",
    )(qi, kiwi, kiwi, qb, kb, vb)


def _merge_kernel(oa_ref, ob_ref, wa_ref, wb_ref, ga_ref, gb_ref, o_ref):
    ya = _dot(oa_ref[...], wa_ref[...])
    yb = _dot(ob_ref[...], wb_ref[...])
    u = jax.nn.sigmoid(ga_ref[...]) * ya + jax.nn.sigmoid(gb_ref[...]) * yb
    o_ref[...] = u.astype(o_ref.dtype)


def merge_branches(oa, ob, w_pa, w_pb, ga, gb):
    m, ka = oa.shape
    kb = ob.shape[1]
    n = w_pa.shape[1]
    tm = _pick(m, 1024, 8)
    tn = _pick(n, 512, LANES)
    return pl.pallas_call(
        _merge_kernel,
        grid=(m // tm, n // tn),
        in_specs=[
            pl.BlockSpec((tm, ka), lambda i, j: (i, 0)),
            pl.BlockSpec((tm, kb), lambda i, j: (i, 0)),
            pl.BlockSpec((ka, tn), lambda i, j: (0, j)),
            pl.BlockSpec((kb, tn), lambda i, j: (0, j)),
            pl.BlockSpec((tm, tn), lambda i, j: (i, j)),
            pl.BlockSpec((tm, tn), lambda i, j: (i, j)),
        ],
        out_specs=pl.BlockSpec((tm, tn), lambda i, j: (i, j)),
        out_shape=jax.ShapeDtypeStruct((m, n), oa.dtype),
        compiler_params=_params("parallel", "arbitrary"),
        name="merge_branches",
    )(oa, ob, w_pa, w_pb, ga, gb)


def _matmul_res_kernel(a_ref, w_ref, r_ref, o_ref):
    o_ref[...] = r_ref[...] + _dot(a_ref[...], w_ref[...])


def matmul_residual(a, w, res):
    m, k = a.shape
    n = w.shape[1]
    tm = _pick(m, 1024, 8)
    tn = _pick(n, 512, LANES)
    return pl.pallas_call(
        _matmul_res_kernel,
        grid=(m // tm, n // tn),
        in_specs=[
            pl.BlockSpec((tm, k), lambda i, j: (i, 0)),
            pl.BlockSpec((k, tn), lambda i, j: (0, j)),
            pl.BlockSpec((tm, tn), lambda i, j: (i, j)),
        ],
        out_specs=pl.BlockSpec((tm, tn), lambda i, j: (i, j)),
        out_shape=jax.ShapeDtypeStruct((m, n), F32),
        compiler_params=_params("parallel", "arbitrary"),
        name="matmul_residual",
    )(a, w, res)


def _gateup_kernel(x_ref, wg_ref, wu_ref, o_ref, *w_cast):
    x = x_ref[...]
    if w_cast:
        wg_c, wu_c = w_cast

        @pl.when(pl.program_id(1) == 0)
        def _():
            wg_c[...] = wg_ref[...].astype(wg_c.dtype)
            wu_c[...] = wu_ref[...].astype(wu_c.dtype)

        wg_ref, wu_ref = wg_c, wu_c
    gate = _dot(x, wg_ref[...])
    up = _dot(x, wu_ref[...])
    o_ref[...] = (jax.nn.silu(gate) * up).astype(o_ref.dtype)


def gate_up(xn, wg, wu):
    m, k = xn.shape
    n = wg.shape[1]
    tm = _pick(m, 1024, 8)
    tn = _pick(n, 512, LANES)
    scratch = [] if wg.dtype == xn.dtype else [pltpu.VMEM((k, tn), xn.dtype), pltpu.VMEM((k, tn), xn.dtype)]
    return pl.pallas_call(
        _gateup_kernel,
        grid=(n // tn, m // tm),
        in_specs=[
            pl.BlockSpec((tm, k), lambda j, i: (i, 0)),
            pl.BlockSpec((k, tn), lambda j, i: (0, j)),
            pl.BlockSpec((k, tn), lambda j, i: (0, j)),
        ],
        out_specs=pl.BlockSpec((tm, tn), lambda j, i: (i, j)),
        out_shape=jax.ShapeDtypeStruct((m, n), xn.dtype),
        scratch_shapes=scratch,
        compiler_params=_params("parallel", "arbitrary"),
        name="ffn_gate_up",
    )(xn, wg, wu)


def _router_kernel(x_ref, g_ref, wr_ref, base_ref, hn_ref, gate_ref, route_ref, cnt_ref, run_ref, *, n_exp):
    @pl.when(pl.program_id(0) == 0)
    def _():
        run_ref[...] = base_ref[...]

    hn = _rms(x_ref[...], g_ref[...])
    hn_ref[...] = hn
    tm = hn.shape[0]
    lane = lax.broadcasted_iota(jnp.int32, (tm, LANES), 1)
    logits = jnp.full((tm, LANES), -jnp.inf, F32)
    for e in range(n_exp):
        le = jnp.sum(hn * wr_ref[e:e + 1, :], axis=1, keepdims=True)
        logits = jnp.where(lane == e, le, logits)
    v1 = jnp.max(logits, axis=1, keepdims=True)
    i1 = jnp.min(jnp.where(logits == v1, lane, LANES), axis=1, keepdims=True)
    rest = jnp.where(lane == i1, -jnp.inf, logits)
    v2 = jnp.max(rest, axis=1, keepdims=True)
    i2 = jnp.min(jnp.where(rest == v2, lane, LANES), axis=1, keepdims=True)
    e2 = jnp.exp(v2 - v1)
    den = 1.0 + e2
    gate_ref[...] = jnp.where(lane == 0, 1.0 / den, jnp.where(lane == 1, e2 / den, 0.0))

    onehot = jnp.where((lane == i1) | (lane == i2), 1.0, 0.0)
    r = lax.broadcasted_iota(jnp.int32, (tm, tm), 0)
    c = lax.broadcasted_iota(jnp.int32, (tm, tm), 1)
    before = _dot(jnp.where(c < r, 1.0, 0.0).astype(BF16), onehot.astype(BF16)) + run_ref[...]
    r1 = jnp.sum(jnp.where(lane == i1, before, 0.0), axis=1, keepdims=True).astype(jnp.int32)
    r2 = jnp.sum(jnp.where(lane == i2, before, 0.0), axis=1, keepdims=True).astype(jnp.int32)
    route_ref[...] = jnp.where(lane == 0, i1, jnp.where(lane == 1, i2, jnp.where(lane == 2, r1,
                               jnp.where(lane == 3, r2, 0))))
    run_ref[...] = run_ref[...] + jnp.sum(onehot, axis=0, keepdims=True)
    cnt_ref[...] = run_ref[...]


def moe_router(x, g, w_router, base_counts):
    m, d = x.shape
    n_exp = w_router.shape[1]
    assert TOP_K_EXPERTS == 2 and n_exp <= LANES
    tm = _pick(m, 256, 8)
    row = lambda i: (i, 0)
    fixed = lambda i: (0, 0)
    return pl.pallas_call(
        functools.partial(_router_kernel, n_exp=n_exp),
        grid=(m // tm,),
        in_specs=[
            pl.BlockSpec((tm, d), row),
            pl.BlockSpec((1, d), fixed),
            pl.BlockSpec((n_exp, d), fixed),
            pl.BlockSpec((1, LANES), fixed),
        ],
        out_specs=[pl.BlockSpec((tm, d), row), pl.BlockSpec((tm, LANES), row), pl.BlockSpec((tm, LANES), row),
                   pl.BlockSpec((1, LANES), fixed)],
        out_shape=[jax.ShapeDtypeStruct((m, d), F32), jax.ShapeDtypeStruct((m, LANES), F32),
                   jax.ShapeDtypeStruct((m, LANES), jnp.int32), jax.ShapeDtypeStruct((1, LANES), F32)],
        scratch_shapes=[pltpu.VMEM((1, LANES), F32)],
        compiler_params=_params("arbitrary"),
        name="moe_router",
    )(x, g.reshape(1, d), w_router.T, base_counts)


def _row_copy(src_ref, src_row, dst_ref, dst_row, sem):
    return pltpu.make_async_copy(src_ref.at[pl.ds(src_row, 1)], dst_ref.at[pl.ds(dst_row, 1)], sem)


def _dispatch_kernel(pos1_ref, pos2_ref, x_ref, xs_in_ref, xs_ref, sem, *, tm):
    del xs_in_ref
    base = pl.program_id(0) * tm

    def start(r, carry):
        _row_copy(x_ref, r, xs_ref, pos1_ref[base + r], sem).start()
        _row_copy(x_ref, r, xs_ref, pos2_ref[base + r], sem).start()
        return carry

    def wait(r, carry):
        _row_copy(x_ref, 0, xs_ref, 0, sem).wait()
        _row_copy(x_ref, 0, xs_ref, 0, sem).wait()
        return carry

    lax.fori_loop(0, tm, start, 0)
    lax.fori_loop(0, tm, wait, 0)


def moe_dispatch(hn, pos1, pos2, x_sorted):
    m, d = hn.shape
    tm = _pick(m, 256, 8)
    grid_spec = pltpu.PrefetchScalarGridSpec(
        num_scalar_prefetch=2,
        grid=(m // tm,),
        in_specs=[pl.BlockSpec((tm, d), lambda i, p1, p2: (i, 0)), pl.BlockSpec(memory_space=pl.ANY)],
        out_specs=pl.BlockSpec(memory_space=pl.ANY),
        scratch_shapes=[pltpu.SemaphoreType.DMA(())],
    )
    return pl.pallas_call(
        functools.partial(_dispatch_kernel, tm=tm),
        grid_spec=grid_spec,
        out_shape=jax.ShapeDtypeStruct(x_sorted.shape, x_sorted.dtype),
        input_output_aliases={3: 0},
        compiler_params=_params("arbitrary"),
        name="moe_dispatch",
    )(pos1, pos2, hn, x_sorted)


def _combine_kernel(pos1_ref, pos2_ref, x_ref, gate_ref, ys_ref, o_ref, buf_ref, sem, *, tm):
    base = pl.program_id(0) * tm

    def start(r, carry):
        _row_copy(ys_ref, pos1_ref[base + r], buf_ref.at[0], r, sem).start()
        _row_copy(ys_ref, pos2_ref[base + r], buf_ref.at[1], r, sem).start()
        return carry

    def wait(r, carry):
        _row_copy(ys_ref, 0, buf_ref.at[0], 0, sem).wait()
        _row_copy(ys_ref, 0, buf_ref.at[1], 0, sem).wait()
        return carry

    lax.fori_loop(0, tm, start, 0)
    lax.fori_loop(0, tm, wait, 0)
    gate = gate_ref[...]
    o_ref[...] = x_ref[...] + gate[:, 0:1] * buf_ref[0] + gate[:, 1:2] * buf_ref[1]


def moe_combine(x, gates, pos1, pos2, y_sorted):
    m, d = x.shape
    tm = _pick(m, 256, 8)
    row = lambda i, p1, p2: (i, 0)
    grid_spec = pltpu.PrefetchScalarGridSpec(
        num_scalar_prefetch=2,
        grid=(m // tm,),
        in_specs=[pl.BlockSpec((tm, d), row), pl.BlockSpec((tm, LANES), row), pl.BlockSpec(memory_space=pl.ANY)],
        out_specs=pl.BlockSpec((tm, d), row),
        scratch_shapes=[pltpu.VMEM((2, tm, d), F32), pltpu.SemaphoreType.DMA(())],
    )
    return pl.pallas_call(
        functools.partial(_combine_kernel, tm=tm),
        grid_spec=grid_spec,
        out_shape=jax.ShapeDtypeStruct((m, d), F32),
        compiler_params=_params("arbitrary"),
        name="moe_combine",
    )(pos1, pos2, x, gates, y_sorted)


def _expert_gateup_kernel(te_ref, nu_ref, x_ref, wg_ref, wu_ref, o_ref, wg_c, wu_c):
    i = pl.program_id(1)
    used = i < nu_ref[0]
    fresh = (i == 0) | (te_ref[i] != te_ref[jnp.maximum(i - 1, 0)])

    @pl.when(used & fresh)
    def _():
        wg_c[...] = wg_ref[0].astype(wg_c.dtype)
        wu_c[...] = wu_ref[0].astype(wu_c.dtype)

    @pl.when(used)
    def _():
        x = x_ref[...]
        gate = _dot(x, wg_c[...])
        up = _dot(x, wu_c[...])
        o_ref[...] = (jax.nn.silu(gate) * up).astype(o_ref.dtype)

    @pl.when(jnp.logical_not(used))
    def _():
        o_ref[...] = jnp.zeros(o_ref.shape, o_ref.dtype)


def expert_gate_up(x_sorted, wg, wu, tile_expert, n_used, tm):
    p_rows, d = x_sorted.shape
    ff = wg.shape[2]
    tn = _pick(ff, 512, LANES)

    def w_map(j, i, te, nu):
        return (te[i], 0, j)

    grid_spec = pltpu.PrefetchScalarGridSpec(
        num_scalar_prefetch=2,
        grid=(ff // tn, p_rows // tm),
        in_specs=[
            pl.BlockSpec((tm, d), lambda j, i, te, nu: (jnp.minimum(i, nu[0] - 1), 0)),
            pl.BlockSpec((1, d, tn), w_map),
            pl.BlockSpec((1, d, tn), w_map),
        ],
        out_specs=pl.BlockSpec((tm, tn), lambda j, i, te, nu: (i, j)),
        scratch_shapes=[pltpu.VMEM((d, tn), BF16), pltpu.VMEM((d, tn), BF16)],
    )
    return pl.pallas_call(
        _expert_gateup_kernel,
        grid_spec=grid_spec,
        out_shape=jax.ShapeDtypeStruct((p_rows, ff), BF16),
        compiler_params=_params("parallel", "arbitrary"),
        name="expert_gate_up",
    )(tile_expert, n_used, x_sorted, wg, wu)


def _expert_down_kernel(te_ref, nu_ref, h_ref, wd_ref, o_ref):
    @pl.when(pl.program_id(0) < nu_ref[0])
    def _():
        o_ref[...] = _dot(h_ref[...], wd_ref[0])

    @pl.when(pl.program_id(0) >= nu_ref[0])
    def _():
        o_ref[...] = jnp.zeros(o_ref.shape, o_ref.dtype)


def expert_down(h_sorted, wd, tile_expert, n_used, tm):
    p_rows, ff = h_sorted.shape
    d = wd.shape[2]
    tn = _pick(d, 512, LANES)
    nj = d // tn
    grid_spec = pltpu.PrefetchScalarGridSpec(
        num_scalar_prefetch=2,
        grid=(p_rows // tm, nj),
        in_specs=[
            pl.BlockSpec((tm, ff), lambda i, j, te, nu: (jnp.minimum(i, nu[0] - 1), 0)),
            pl.BlockSpec((1, ff, tn), lambda i, j, te, nu: (te[i], 0, jnp.where(i < nu[0], j, nj - 1))),
        ],
        out_specs=pl.BlockSpec((tm, tn), lambda i, j, te, nu: (i, j)),
    )
    return pl.pallas_call(
        _expert_down_kernel,
        grid_spec=grid_spec,
        out_shape=jax.ShapeDtypeStruct((p_rows, d), F32),
        compiler_params=_params("parallel", "arbitrary"),
        name="expert_down",
    )(tile_expert, n_used, h_sorted, wd)


MOE_ROW_TILE = 512
SAMPLE_ATTN_PAGES = 8
SAMPLE_DSA_PAGES = 16


def moe_ffn(xp, xs, g, w_router, wg, wu, wd):
    n_exp = w_router.shape[1]
    d = xp.shape[1]
    tm = MOE_ROW_TILE
    zeros = jnp.zeros((1, LANES), F32)
    hn_p, gate_p, route_p, cnt_p = moe_router(xp, g, w_router, zeros)
    hn_s, gate_s, route_s, cnt = moe_router(xs, g, w_router, cnt_p)

    counts = cnt[0, :n_exp].astype(jnp.int32)
    sizes = (counts + tm - 1) // tm * tm
    ends = jnp.cumsum(sizes)
    starts = ends - sizes
    n_assign = TOP_K_EXPERTS * (xp.shape[0] + xs.shape[0])
    n_tiles = (n_assign + n_exp * (tm - 1) + tm - 1) // tm
    tile_expert = jnp.minimum(jnp.sum(jnp.arange(n_tiles)[:, None] * tm >= ends[None, :], axis=1), n_exp - 1)
    tile_expert = tile_expert.astype(jnp.int32)
    n_used = (ends[-1:] // tm).astype(jnp.int32)

    def slots(route):
        return (jnp.take(starts, route[:, 0]) + route[:, 2], jnp.take(starts, route[:, 1]) + route[:, 3])

    p1_p, p2_p = slots(route_p)
    p1_s, p2_s = slots(route_s)
    x_sorted = jnp.zeros((n_tiles * tm, d), F32)
    x_sorted = moe_dispatch(hn_p, p1_p, p2_p, x_sorted)
    x_sorted = moe_dispatch(hn_s, p1_s, p2_s, x_sorted)
    h_sorted = expert_gate_up(x_sorted.astype(BF16), wg, wu, tile_expert, n_used, tm)
    y_sorted = expert_down(h_sorted, wd.astype(BF16), tile_expert, n_used, tm)
    return (moe_combine(xp, gate_p, p1_p, p2_p, y_sorted), moe_combine(xs, gate_s, p1_s, p2_s, y_sorted))


def _sample_attn_kernel(pt_ref, lq_ref, lk_ref, g_ref, qa_ref, qi_ref, wi_ref, *rest,
                        n_steps, group, n_heads, dqk, lam_init, n_idx, d_idx):
    kp_refs, vp_refs, ip_refs = rest[:group], rest[group:2 * group], rest[2 * group:3 * group]
    kn_ref, vn_ref, in_ref, o_ref, sc_ref, scn_ref, m_ref, l_ref, acc_ref = rest[3 * group:]
    p = pl.program_id(1)
    hd = 2 * dqk
    page = kp_refs[0].shape[2]
    cols = page * n_heads
    scale = dqk ** -0.5

    q = qa_ref[0].astype(F32)
    lane = lax.broadcasted_iota(jnp.int32, q.shape, 1)
    qs = jnp.concatenate([jnp.where(lane < dqk, q, 0.0), jnp.where(lane >= dqk, q, 0.0)], axis=0)

    @pl.when(p == 0)
    def _():
        m_ref[...] = jnp.full(m_ref.shape, -jnp.inf, F32)
        l_ref[...] = jnp.zeros(l_ref.shape, F32)
        acc_ref[...] = jnp.zeros(acc_ref.shape, F32)

    k2 = jnp.concatenate([r[0, 0].reshape(cols, hd).astype(BF16) for r in kp_refs], axis=0)
    v2 = jnp.concatenate([r[0, 0].reshape(cols, hd).astype(BF16) for r in vp_refs], axis=0)
    s = _dot_nt(qs.astype(BF16), k2) * scale
    r_id = lax.broadcasted_iota(jnp.int32, s.shape, 0) & (n_heads - 1)
    c_id = lax.broadcasted_iota(jnp.int32, s.shape, 1) & (n_heads - 1)
    s = jnp.where(r_id == c_id, s, -jnp.inf)
    m_old = m_ref[...]
    m_new = jnp.maximum(m_old, jnp.max(s, axis=1, keepdims=True))
    alpha = jnp.exp(m_old - m_new)
    pr = jnp.exp(s - m_new)
    l_ref[...] = alpha * l_ref[...] + jnp.sum(pr, axis=1, keepdims=True)
    acc_ref[...] = alpha * acc_ref[...] + _dot(pr.astype(BF16), v2)
    m_ref[...] = m_new

    wscale = n_idx ** -0.5 * d_idx ** -0.5
    qi = qi_ref[0]
    w = wi_ref[0] * wscale
    for k, ip_ref in enumerate(ip_refs):
        si = jnp.maximum(_dot(qi, ip_ref[0, 0]), 0.0) * w
        sc_ref[0, k:k + 1, :] = jnp.sum(si, axis=0, keepdims=True)

    @pl.when(p == n_steps - 1)
    def _():
        kn = jnp.concatenate([kn_ref[0], kn_ref[0]], axis=0)
        vn = jnp.concatenate([vn_ref[0], vn_ref[0]], axis=0)
        s_new = jnp.sum(qs * kn, axis=1, keepdims=True) * scale
        m_o = m_ref[...]
        m_n = jnp.maximum(m_o, s_new)
        a = jnp.exp(m_o - m_n)
        p_new = jnp.exp(s_new - m_n)
        l_fin = a * l_ref[...] + p_new
        o_all = (a * acc_ref[...] + p_new * vn) / l_fin
        lam = _lambda(lq_ref, lk_ref, lam_init)
        o = o_all[:n_heads] - lam * o_all[n_heads:]
        o_ref[0] = (_rms(o, g_ref[...]) * (1.0 - lam_init)).astype(o_ref.dtype)
        kin = in_ref[0]
        s_in = jnp.maximum(jnp.sum(qi.astype(F32) * kin, axis=1, keepdims=True), 0.0) * w
        scn_ref[0] = jnp.broadcast_to(jnp.sum(s_in, axis=0, keepdims=True), (1, LANES))


def sample_attention(page_table, layer, cache_k, cache_v, cache_i_t, qa, qi, wi, k_new, v_new, i_new, lq, lk, g,
                     *, n_heads, dqk, lam_init, n_idx, d_idx):
    db, n_pages = page_table.shape
    width = qa.shape[1]
    page = cache_k.shape[2]
    hd = 2 * dqk
    rows = 2 * n_heads
    assert n_heads & (n_heads - 1) == 0
    pt = page_table.reshape(-1)
    group = SAMPLE_ATTN_PAGES if n_pages % SAMPLE_ATTN_PAGES == 0 else 1
    n_steps = n_pages // group

    def pool5(k):
        return lambda b, p, pt_ref: (layer, pt_ref[b * n_pages + p * group + k], 0, 0, 0)

    def pool4(k):
        return lambda b, p, pt_ref: (layer, pt_ref[b * n_pages + p * group + k], 0, 0)

    row3 = lambda b, p, pt_ref: (b, 0, 0)
    const2 = lambda b, p, pt_ref: (0, 0)
    grid_spec = pltpu.PrefetchScalarGridSpec(
        num_scalar_prefetch=1,
        grid=(db, n_steps),
        in_specs=[
            pl.BlockSpec(lq.shape, const2),
            pl.BlockSpec(lk.shape, const2),
            pl.BlockSpec((1, hd), const2),
            pl.BlockSpec((1, n_heads, hd), row3),
            pl.BlockSpec((1, n_idx, d_idx), row3),
            pl.BlockSpec((1, n_idx, 1), row3),
            *[pl.BlockSpec((1, 1, page, n_heads, hd), pool5(k)) for k in range(group)],
            *[pl.BlockSpec((1, 1, page, n_heads, hd), pool5(k)) for k in range(group)],
            *[pl.BlockSpec((1, 1, d_idx, page), pool4(k)) for k in range(group)],
            pl.BlockSpec((1, n_heads, hd), row3),
            pl.BlockSpec((1, n_heads, hd), row3),
            pl.BlockSpec((1, 1, d_idx), row3),
        ],
        out_specs=[
            pl.BlockSpec((1, n_heads, hd), row3),
            pl.BlockSpec((1, group, page), lambda b, p, pt_ref: (b * n_steps + p, 0, 0)),
            pl.BlockSpec((1, 1, LANES), row3),
        ],
        scratch_shapes=[pltpu.VMEM((rows, 1), F32), pltpu.VMEM((rows, 1), F32), pltpu.VMEM((rows, hd), F32)],
    )
    out, scores, score_new = pl.pallas_call(
        functools.partial(_sample_attn_kernel, n_steps=n_steps, group=group, n_heads=n_heads, dqk=dqk,
                          lam_init=lam_init, n_idx=n_idx, d_idx=d_idx),
        grid_spec=grid_spec,
        out_shape=[jax.ShapeDtypeStruct((db, n_heads, hd), F32),
                   jax.ShapeDtypeStruct((db * n_steps, group, page), F32),
                   jax.ShapeDtypeStruct((db, 1, LANES), F32)],
        compiler_params=_params("parallel", "arbitrary"),
        name="sample_attention",
    )(pt, lq, lk, g.reshape(1, hd), qa.reshape(db, n_heads, hd), qi.reshape(db, n_idx, d_idx),
      wi.reshape(db, n_idx, 1), *([cache_k] * group), *([cache_v] * group), *([cache_i_t] * group),
      k_new.reshape(db, n_heads, hd), v_new.reshape(db, n_heads, hd), i_new.reshape(db, 1, d_idx))
    return out.reshape(db, width), scores.reshape(db, n_pages, page), score_new


def _count_all(mask):
    c = jnp.sum(mask.astype(jnp.int32), axis=1, keepdims=True)
    return jnp.sum(c, axis=0, keepdims=True)


def _sample_dsa_kernel(pt_ref, sc_ref, scn_ref, qb_ref, *rest, n_pages, group, page, n_heads, n_kv, dh, topk):
    kp_refs, vp_refs = rest[:group], rest[group:2 * group]
    kn_ref, vn_ref, o_ref, key_ref, keyn_ref, t_ref, cut_ref, selx_ref, m_ref, l_ref, acc_ref = rest[2 * group:]
    p = pl.program_id(1)
    rep = n_heads // n_kv
    scale = dh ** -0.5
    past = n_pages * page
    cols = page * n_kv
    kv_shift = n_kv.bit_length() - 1
    rep_shift = rep.bit_length() - 1

    @pl.when(p == 0)
    def _():
        m_ref[...] = jnp.full(m_ref.shape, -jnp.inf, F32)
        l_ref[...] = jnp.zeros(l_ref.shape, F32)
        acc_ref[...] = jnp.zeros(acc_ref.shape, F32)
        key_ref[...] = _order_key(sc_ref[0])
        keyn_ref[...] = _order_key(scn_ref[0])
        key_new = keyn_ref[:, 0:1]

        def cnt(t):
            return _count_all(key_ref[...] >= t) + (key_new >= t).astype(jnp.int32)

        t0 = jnp.full((1, 1), INT_MIN, jnp.int32)
        zero = jnp.zeros((1, 1), jnp.int32)
        t0 = jnp.where(cnt(zero) >= topk, zero, t0)

        def body(it, t):
            cand = t | (jnp.int32(1) << (30 - it))
            return jnp.where(cnt(cand) >= topk, cand, t)

        t = lax.fori_loop(0, 31, body, t0)
        t_ref[...] = t
        key = key_ref[...]
        n_gt = _count_all(key > t) + (key_new > t).astype(jnp.int32)
        need = topk - n_gt
        pos = (lax.broadcasted_iota(jnp.int32, key.shape, 0) * page
               + lax.broadcasted_iota(jnp.int32, key.shape, 1))
        n_bits = int(math.ceil(math.log2(past + 1)))

        def body2(it, x):
            cand = x | (jnp.int32(1) << (n_bits - 1 - it))
            c = _count_all((key_ref[...] == t) & (pos < cand)) + ((key_new == t) & (past < cand)).astype(jnp.int32)
            return jnp.where(c < need, cand, x)

        cut = lax.fori_loop(0, n_bits, body2, jnp.zeros((1, 1), jnp.int32))
        cut_ref[...] = cut
        sel = (key > t) | ((key == t) & (pos <= cut))
        e_t = lax.broadcasted_iota(jnp.int32, (page, cols), 0)
        e_c = lax.broadcasted_iota(jnp.int32, (page, cols), 1)
        spread = jnp.where((e_c >> kv_shift) == e_t, 1.0, 0.0).astype(BF16)
        selx_ref[...] = _dot(jnp.where(sel, 1.0, 0.0).astype(BF16), spread)

    t = t_ref[...]
    cut = cut_ref[...]
    q = qb_ref[0]

    k2 = jnp.concatenate([r[0, 0].astype(BF16) for r in kp_refs], axis=0)
    v2 = jnp.concatenate([r[0, 0].astype(BF16) for r in vp_refs], axis=0)
    sel_cols = jnp.concatenate(
        [jnp.broadcast_to(selx_ref[pl.ds(p * group + k, 1), :], (n_heads, cols)) for k in range(group)], axis=1)
    r_id = lax.broadcasted_iota(jnp.int32, (n_heads, group * cols), 0)
    c_id = lax.broadcasted_iota(jnp.int32, (n_heads, group * cols), 1)
    keep = (sel_cols > 0.5) & ((c_id & (n_kv - 1)) == (r_id >> rep_shift))
    s = _dot_nt(q.astype(BF16), k2) * scale
    s = jnp.where(keep, s, -jnp.inf)
    m_old = m_ref[...]
    m_new = jnp.maximum(m_old, jnp.max(s, axis=1, keepdims=True))
    m_safe = jnp.where(m_new == -jnp.inf, 0.0, m_new)
    alpha = jnp.exp(m_old - m_safe)
    pr = jnp.exp(s - m_safe)
    l_ref[...] = alpha * l_ref[...] + jnp.sum(pr, axis=1, keepdims=True)
    acc_ref[...] = alpha * acc_ref[...] + _dot(pr.astype(BF16), v2)
    m_ref[...] = m_new

    @pl.when(p == n_pages // group - 1)
    def _():
        key_new = keyn_ref[:, 0:1]
        sel_new = (key_new > t) | ((key_new == t) & (past <= cut))
        h_id = lax.broadcasted_iota(jnp.int32, (n_heads, dh), 0) >> rep_shift
        kn = jnp.zeros((n_heads, dh), F32)
        vn = jnp.zeros((n_heads, dh), F32)
        for g in range(n_kv):
            kn = jnp.where(h_id == g, kn_ref[0, g:g + 1, :], kn)
            vn = jnp.where(h_id == g, vn_ref[0, g:g + 1, :], vn)
        s_new = jnp.sum(q.astype(F32) * kn, axis=1, keepdims=True) * scale
        s_new = jnp.where(sel_new, s_new, -jnp.inf)
        m_o = m_ref[...]
        m_n = jnp.maximum(m_o, s_new)
        m_s = jnp.where(m_n == -jnp.inf, 0.0, m_n)
        a = jnp.exp(m_o - m_s)
        p_new = jnp.exp(s_new - m_s)
        l_fin = a * l_ref[...] + p_new
        o_ref[0] = ((a * acc_ref[...] + p_new * vn) / l_fin).astype(o_ref.dtype)


def sample_dsa(page_table, layer, cache_k, cache_v, scores, score_new, qb, k_new, v_new,
               *, n_heads, n_kv, dh, topk):
    db, n_pages = page_table.shape
    cols = cache_k.shape[2]
    page = cols // n_kv
    rep = n_heads // n_kv
    assert n_kv & (n_kv - 1) == 0 and rep & (rep - 1) == 0
    pt = page_table.reshape(-1)

    group = SAMPLE_DSA_PAGES if n_pages % SAMPLE_DSA_PAGES == 0 else 1

    def pool_map(k):
        return lambda b, p, pt_ref: (layer, pt_ref[b * n_pages + p * group + k], 0, 0)

    row3 = lambda b, p, pt_ref: (b, 0, 0)
    grid_spec = pltpu.PrefetchScalarGridSpec(
        num_scalar_prefetch=1,
        grid=(db, n_pages // group),
        in_specs=[
            pl.BlockSpec((1, n_pages, page), row3),
            pl.BlockSpec((1, 1, LANES), row3),
            pl.BlockSpec((1, n_heads, dh), row3),
            *[pl.BlockSpec((1, 1, cols, dh), pool_map(k)) for k in range(group)],
            *[pl.BlockSpec((1, 1, cols, dh), pool_map(k)) for k in range(group)],
            pl.BlockSpec((1, n_kv, dh), row3),
            pl.BlockSpec((1, n_kv, dh), row3),
        ],
        out_specs=pl.BlockSpec((1, n_heads, dh), row3),
        scratch_shapes=[pltpu.VMEM((n_pages, page), jnp.int32), pltpu.VMEM((1, LANES), jnp.int32),
                        pltpu.VMEM((1, 1), jnp.int32), pltpu.VMEM((1, 1), jnp.int32),
                        pltpu.VMEM((n_pages, cols), F32),
                        pltpu.VMEM((n_heads, 1), F32), pltpu.VMEM((n_heads, 1), F32),
                        pltpu.VMEM((n_heads, dh), F32)],
    )
    out = pl.pallas_call(
        functools.partial(_sample_dsa_kernel, n_pages=n_pages, group=group, page=page, n_heads=n_heads,
                          n_kv=n_kv, dh=dh, topk=topk),
        grid_spec=grid_spec,
        out_shape=jax.ShapeDtypeStruct((db, n_heads, dh), F32),
        compiler_params=_params("parallel", "arbitrary"),
        name="sample_dsa",
    )(pt, scores, score_new, qb.reshape(db, n_heads, dh), *([cache_k] * group), *([cache_v] * group),
      k_new.reshape(db, n_kv, dh), v_new.reshape(db, n_kv, dh))
    return out.reshape(db, n_heads * dh)


def kernel(x_prompt, x_sample, cache_a_k, cache_a_v, cache_b_k, cache_b_v, cache_idx_k, page_table, w_in, lambda_q, lambda_k, subln_g, w_branch_a, w_branch_b, w_out, norm_mix_g, norm_ffn_g, w_dense_gate, w_dense_up, w_dense_down, w_router, w_exp_gate, w_exp_up, w_exp_down, norm_final_g):
    batch, seq, d_model = x_prompt.shape
    db, dec_seq, _ = x_sample.shape
    assert dec_seq == 1, "one new token per sample row"
    depth, n_pool, page, n_ha, a_width = cache_a_k.shape
    assert page == PAGE_SIZE
    dqk = a_width // 2
    dv = cache_a_v.shape[4]
    n_kv, dh = cache_b_k.shape[3:]
    d_idx = cache_idx_k.shape[3]
    n_hb = w_branch_b.shape[1] // dh
    n_in = w_in.shape[2]
    wa, wva, wqb, wkb = n_ha * 2 * dqk, n_ha * dv, n_hb * dh, n_kv * dh
    n_idx = (n_in - (2 * wa + wva + wqb + 2 * wkb + d_idx + 2 * d_model)) // (d_idx + 1)
    assert dv == 2 * dqk == LANES and dh == LANES and 2 * d_idx == LANES
    n_pages = page_table.shape[1]
    past_len = n_pages * page
    topk_p = min(TOPK_MAX, seq // 4)
    topk_s = min(TOPK_MAX, (past_len + dec_seq) // 4)
    n_exp = w_router.shape[2]

    offs = [0]
    for wdt in (wa, wa, wva, wqb, wkb, wkb, n_idx * d_idx, d_idx, n_idx, d_model, d_model):
        offs.append(offs[-1] + wdt)
    o_qa, o_ka, o_va, o_qb, o_kb, o_vb, o_qi, o_ki, o_wi, o_ga, o_gb, _ = offs

    pos_p = jnp.arange(seq, dtype=F32)
    pos_s = jnp.full((db,), float(past_len), F32)
    tabs = {}
    for name, pos in (("p", pos_p), ("s", pos_s)):
        tabs[name, "a"] = rope_table(pos, dqk)
        tabs[name, "b"] = rope_table(pos, dh)
        tabs[name, "i"] = rope_table(pos, d_idx, valid_lanes=d_idx)

    ck_b = cache_b_k.reshape(depth, n_pool, page * n_kv, dh)
    cv_b = cache_b_v.reshape(depth, n_pool, page * n_kv, dh)
    ci_t = jnp.swapaxes(cache_idx_k, 2, 3)

    xp = x_prompt.reshape(batch * seq, d_model)
    xs = x_sample.reshape(db * dec_seq, d_model)
    rows = {"p": [[] for _ in range(5)], "s": [[] for _ in range(5)]}

    for l in range(depth):
        lam_init = 0.8 - 0.6 * math.exp(-0.3 * l)
        lq, lk = lambda_q[l], lambda_k[l]
        mix_w = {}
        for grp, dt in (("p", BF16), ("s", F32)):
            wl = w_in[l].astype(dt)
            mix_w[grp] = dict(
                wl=wl, kiwi=jnp.pad(wl[:, o_ki:o_ga], ((0, 0), (0, LANES - d_idx - n_idx))),
                pa=w_branch_a[l].astype(dt), pb=w_branch_b[l].astype(dt), o=w_out[l].astype(dt))

        def in_proj(x, grp):
            act = BF16 if grp == "p" else F32
            wl = mix_w[grp]["wl"]
            xn = rmsnorm(x, norm_mix_g[l], act)
            ta, ha = tabs[grp, "a"]
            tb, hb = tabs[grp, "b"]
            ti, hi = tabs[grp, "i"]

            def kv_pair(w, tab=None, half=0):
                outs = project(xn, w, [F32] if act == F32 else [F32, BF16], tab, half)
                return outs[0], outs[-1]

            z = {}
            z["qa"], = project(xn, wl[:, o_qa:o_ka], [act], ta, ha)
            z["ka32"], z["ka"] = kv_pair(wl[:, o_ka:o_va], ta, ha)
            z["va32"], z["va"] = kv_pair(wl[:, o_va:o_qb])
            z["qb"], = project(xn, wl[:, o_qb:o_kb], [act], tb, hb)
            z["kb32"], z["kb"] = kv_pair(wl[:, o_kb:o_vb], tb, hb)
            z["vb32"], z["vb"] = kv_pair(wl[:, o_vb:o_qi])
            z["qi"], = project(xn, wl[:, o_qi:o_ki], [act], ta, ha)
            z["kiwi"], = project(xn, mix_w[grp]["kiwi"], [F32], ti, hi)
            z["ga"], = project(xn, wl[:, o_ga:o_gb], [F32])
            z["gb"], = project(xn, wl[:, o_gb:], [F32])
            return z

        def mix_out(x, z, oa, ob, grp):
            w = mix_w[grp]
            u = merge_branches(oa, ob, w["pa"], w["pb"], z["ga"], z["gb"])
            return matmul_residual(u, w["o"], x)

        zp = in_proj(xp, "p")
        oa = diff_attention_prompt(zp["qa"], zp["ka"], zp["va"].T, lq, lk, subln_g[l], batch=batch, seq=seq,
                                   n_heads=n_ha, dqk=dqk, lam_init=lam_init)
        ob = dsa_prompt(zp["qi"], zp["kiwi"], zp["qb"], zp["kb"], zp["vb"], batch=batch, seq=seq, n_idx=n_idx,
                        d_idx=d_idx, n_heads=n_hb, n_kv=n_kv, dh=dh, topk=topk_p)
        xp = mix_out(xp, zp, oa, ob, "p")

        zs = in_proj(xs, "s")
        ki_s = zs["kiwi"][:, :d_idx]
        wi_s = zs["kiwi"][:, d_idx:d_idx + n_idx]
        oa, scores, score_new = sample_attention(
            page_table, l, cache_a_k, cache_a_v, ci_t, zs["qa"], zs["qi"], wi_s, zs["ka32"], zs["va32"], ki_s,
            lq, lk, subln_g[l], n_heads=n_ha, dqk=dqk, lam_init=lam_init, n_idx=n_idx, d_idx=d_idx)
        ob = sample_dsa(page_table, l, ck_b, cv_b, scores, score_new, zs["qb"], zs["kb32"], zs["vb32"],
                        n_heads=n_hb, n_kv=n_kv, dh=dh, topk=topk_s)
        xs = mix_out(xs, zs, oa, ob, "s")

        for grp, z, bt, tt in (("p", zp, batch, seq), ("s", zs, db, dec_seq)):
            rows[grp][0].append(z["ka32"].reshape(bt, tt, n_ha, 2 * dqk))
            rows[grp][1].append(z["va32"].reshape(bt, tt, n_ha, dv))
            rows[grp][2].append(z["kb32"].reshape(bt, tt, n_kv, dh))
            rows[grp][3].append(z["vb32"].reshape(bt, tt, n_kv, dh))
            rows[grp][4].append(z["kiwi"][:, :d_idx].reshape(bt, tt, d_idx))

        i = l // 2
        if l % 2 == 0:
            hp = rmsnorm(xp, norm_ffn_g[l], BF16)
            hs = rmsnorm(xs, norm_ffn_g[l], F32)
            xp = matmul_residual(gate_up(hp, w_dense_gate[i], w_dense_up[i]), w_dense_down[i].astype(BF16), xp)
            xs = matmul_residual(gate_up(hs, w_dense_gate[i], w_dense_up[i]), w_dense_down[i], xs)
        else:
            xp, xs = moe_ffn(xp, xs, norm_ffn_g[l], w_router[i], w_exp_gate[i], w_exp_up[i], w_exp_down[i])

    y_prompt = rmsnorm(xp, norm_final_g, F32).reshape(batch, seq, d_model)
    y_sample = rmsnorm(xs, norm_final_g, F32).reshape(db, dec_seq, d_model)
    outs_p = [jnp.stack(r, axis=0) for r in rows["p"]]
    outs_s = [jnp.stack(r, axis=0) for r in rows["s"]]
    return (y_prompt, y_sample, *outs_p, *outs_s)
```

```python
import functools
import math

import jax
import jax.numpy as jnp
from jax import lax
from jax.experimental import pallas as pl
from jax.experimental.pallas import tpu as pltpu

LANES = 128
VMEM_LIMIT_BYTES = 56 * 1024 * 1024
ROPE_THETA = 500000.0
ROPE_FRAC = 4
RMS_EPS = 1e-6
TOPK_MAX = 256
PAGE_SIZE = 128
TOP_K_EXPERTS = 2
INT_MIN = -2 ** 31
LOG2_E = math.log2(math.e)
F32 = jnp.float32
BF16 = jnp.bfloat16


def _pick(n, pref, mult):
    if n <= pref:
        return n
    t = (pref // mult) * mult
    while t >= mult:
        if n % t == 0:
            return t
        t -= mult
    return n


def _params(*sem):
    return pltpu.CompilerParams(dimension_semantics=sem, vmem_limit_bytes=VMEM_LIMIT_BYTES)


def _precision(a, b):
    return lax.Precision.HIGHEST if a.dtype == F32 and b.dtype == F32 else None


def _dot(a, b):
    return jnp.dot(a, b, preferred_element_type=F32, precision=_precision(a, b))


def _dot_nt(a, b):
    return lax.dot_general(a, b, (((1,), (1,)), ((), ())), preferred_element_type=F32,
                           precision=_precision(a, b))


def _rms(x, g):
    return x * lax.rsqrt(jnp.mean(x * x, axis=-1, keepdims=True) + RMS_EPS) * g


def _rmsnorm_kernel(x_ref, g_ref, o_ref):
    o_ref[...] = _rms(x_ref[...], g_ref[...]).astype(o_ref.dtype)


def rmsnorm(x, g, out_dtype):
    m, d = x.shape
    tm = _pick(m, 512, 8)
    return pl.pallas_call(
        _rmsnorm_kernel,
        grid=(m // tm,),
        in_specs=[pl.BlockSpec((tm, d), lambda i: (i, 0)), pl.BlockSpec((1, d), lambda i: (0, 0))],
        out_specs=pl.BlockSpec((tm, d), lambda i: (i, 0)),
        out_shape=jax.ShapeDtypeStruct((m, d), out_dtype),
        compiler_params=_params("parallel"),
        name="rmsnorm",
    )(x, g.reshape(1, d))


def _proj_kernel(*refs, half, tn):
    if half:
        x_ref, w_ref, tab_ref, *o_refs = refs
    else:
        x_ref, w_ref, *o_refs = refs
    z = _dot(x_ref[...], w_ref[...])
    if not half:
        for o in o_refs:
            o[...] = z.astype(o.dtype)
        return
    c, s_lo, s_hi = tab_ref[0], tab_ref[1], tab_ref[2]
    for g in range(tn // LANES):
        zg = z[:, g * LANES:(g + 1) * LANES]
        r = zg * c + pltpu.roll(zg, LANES - half, 1) * s_lo + pltpu.roll(zg, half, 1) * s_hi
        for o in o_refs:
            o[:, g * LANES:(g + 1) * LANES] = r.astype(o.dtype)


def project(xn, w, out_dtypes, tab=None, half=0, tn_pref=1024):
    m, k = xn.shape
    n = w.shape[1]
    p_rows = m if tab is None else tab.shape[1]
    assert m % p_rows == 0
    tm = _pick(p_rows, 1024, 8)
    tn = _pick(n, tn_pref, LANES)
    in_specs = [pl.BlockSpec((tm, k), lambda i, j: (i, 0)), pl.BlockSpec((k, tn), lambda i, j: (0, j))]
    args = [xn, w]
    if tab is not None:
        nblk = p_rows // tm
        in_specs.append(pl.BlockSpec((3, tm, LANES), lambda i, j: (0, i % nblk, 0)))
        args.append(tab)
    outs = pl.pallas_call(
        functools.partial(_proj_kernel, half=half if tab is not None else 0, tn=tn),
        grid=(m // tm, n // tn),
        in_specs=in_specs,
        out_specs=[pl.BlockSpec((tm, tn), lambda i, j: (i, j)) for _ in out_dtypes],
        out_shape=[jax.ShapeDtypeStruct((m, n), dt) for dt in out_dtypes],
        compiler_params=_params("parallel", "arbitrary"),
        name="in_proj",
    )(*args)
    return outs


def _sample_proj_kernel(x_ref, w_ref, tab_ref, o_ref, *, n_in, halves):
    k = pl.program_id(0)

    @pl.when(k == 0)
    def _():
        o_ref[...] = jnp.zeros(o_ref.shape, o_ref.dtype)

    o_ref[:, :n_in] += _dot(x_ref[...], w_ref[0])

    @pl.when(k == pl.num_programs(0) - 1)
    def _():
        for g, half in enumerate(halves):
            if half:
                cols = slice(g * LANES, (g + 1) * LANES)
                zg = o_ref[:, cols]
                o_ref[:, cols] = (zg * tab_ref[0, :, cols] + pltpu.roll(zg, LANES - half, 1) * tab_ref[1, :, cols]
                                  + pltpu.roll(zg, half, 1) * tab_ref[2, :, cols])


def sample_project(xn, w_in, layer, tab, halves):
    rows, k = xn.shape
    n_in = w_in.shape[2]
    width = LANES * len(halves)
    tk = _pick(k, 256, 8)
    return pl.pallas_call(
        functools.partial(_sample_proj_kernel, n_in=n_in, halves=halves),
        grid=(k // tk,),
        in_specs=[
            pl.BlockSpec((rows, tk), lambda kk: (0, kk)),
            pl.BlockSpec((1, tk, n_in), lambda kk: (layer, kk, 0)),
            pl.BlockSpec((3, rows, width), lambda kk: (0, 0, 0)),
        ],
        out_specs=pl.BlockSpec((rows, width), lambda kk: (0, 0)),
        out_shape=jax.ShapeDtypeStruct((rows, width), F32),
        compiler_params=_params("arbitrary"),
        name="sample_in_proj",
    )(xn, w_in, tab)


def rope_table(pos, head_dim, valid_lanes=LANES):
    rot = head_dim // ROPE_FRAC
    half = rot // 2
    inv_freq = jnp.power(ROPE_THETA, -jnp.arange(half, dtype=F32) * 2.0 / rot)
    ang = pos[:, None] * inv_freq[None, :]
    cos, sin = jnp.cos(ang), jnp.sin(ang)
    n = pos.shape[0]
    ones = jnp.ones((n, head_dim - rot), F32)
    zeros_h = jnp.zeros((n, half), F32)
    zeros_r = jnp.zeros((n, head_dim - rot), F32)
    c = jnp.concatenate([cos, cos, ones], axis=1)
    s_lo = jnp.concatenate([-sin, zeros_h, zeros_r], axis=1)
    s_hi = jnp.concatenate([zeros_h, sin, zeros_r], axis=1)
    reps = LANES // head_dim
    tabs = [jnp.tile(t, (1, reps)) for t in (c, s_lo, s_hi)]
    if valid_lanes < LANES:
        lane = jnp.arange(LANES)[None, :]
        tabs = [jnp.where(lane < valid_lanes, tabs[0], 1.0), jnp.where(lane < valid_lanes, tabs[1], 0.0),
                jnp.where(lane < valid_lanes, tabs[2], 0.0)]
    return jnp.stack(tabs, axis=0), half


def _lambda(lq_ref, lk_ref, lam_init):
    lq = lq_ref[...].astype(F32)
    lk = lk_ref[...].astype(F32)
    prod = lq * lk
    return (jnp.exp(jnp.sum(prod[0:1, :], axis=1, keepdims=True))
            - jnp.exp(jnp.sum(prod[1:2, :], axis=1, keepdims=True)) + lam_init)


def _diffattn_kernel(lq_ref, lk_ref, g_ref, q_ref, k_ref, vt_ref, o_ref, m_ref, l_ref, acc_ref,
                     *, tq, dqk, lam_init):
    i = pl.program_id(2)
    c_exp = dqk ** -0.5 * LOG2_E
    q = q_ref[...].astype(F32)
    lane = lax.broadcasted_iota(jnp.int32, q.shape, 1)
    qs = jnp.concatenate([jnp.where(lane < dqk, q, 0.0), jnp.where(lane >= dqk, q, 0.0)], axis=0)
    qs = qs.astype(q_ref.dtype)
    m_ref[...] = jnp.full(m_ref.shape, -jnp.inf, F32)
    l_ref[...] = jnp.zeros(l_ref.shape, F32)
    acc_ref[...] = jnp.zeros(acc_ref.shape, F32)

    def step(j, masked):
        start = pl.multiple_of(j * tq, tq)
        kt = k_ref[pl.ds(start, tq), :]
        vt = vt_ref[:, pl.ds(start, tq)]
        s = _dot_nt(kt, qs) * c_exp
        if masked:
            key = lax.broadcasted_iota(jnp.int32, s.shape, 0)
            qry = lax.broadcasted_iota(jnp.int32, s.shape, 1) & (tq - 1)
            s = jnp.where(key <= qry, s, -jnp.inf)
        m_old = m_ref[...]
        m_new = jnp.maximum(m_old, jnp.max(s, axis=0, keepdims=True))
        alpha = jnp.exp2(m_old - m_new)
        p = jnp.exp2(s - m_new)
        l_ref[...] = alpha * l_ref[...] + jnp.sum(p, axis=0, keepdims=True)
        acc_ref[...] = alpha * acc_ref[...] + _dot(vt, p.astype(vt.dtype))
        m_ref[...] = m_new

    def body(j, carry):
        step(j, False)
        return carry

    lax.fori_loop(0, i, body, 0)
    step(i, True)

    lam = _lambda(lq_ref, lk_ref, lam_init)
    o_all = acc_ref[...] / l_ref[...]
    o = o_all[:, :tq] - lam * o_all[:, tq:]
    inv = lax.rsqrt(jnp.mean(o * o, axis=0, keepdims=True) + RMS_EPS)
    y = o * inv * g_ref[...] * (1.0 - lam_init)
    o_ref[...] = y.T.astype(o_ref.dtype)


def diff_attention_prompt(q, k, v_t, lq, lk, g, *, batch, seq, n_heads, dqk, lam_init):
    m, width = q.shape
    hd = width // n_heads
    tq = _pick(seq, 512, LANES)
    assert tq & (tq - 1) == 0
    nq = seq // tq
    return pl.pallas_call(
        functools.partial(_diffattn_kernel, tq=tq, dqk=dqk, lam_init=lam_init),
        grid=(batch, n_heads, nq),
        in_specs=[
            pl.BlockSpec(lq.shape, lambda b, h, i: (0, 0)),
            pl.BlockSpec(lk.shape, lambda b, h, i: (0, 0)),
            pl.BlockSpec((hd, 1), lambda b, h, i: (0, 0)),
            pl.BlockSpec((tq, hd), lambda b, h, i: (b * nq + i, h)),
            pl.BlockSpec((seq, hd), lambda b, h, i: (b, h)),
            pl.BlockSpec((hd, seq), lambda b, h, i: (h, b)),
        ],
        out_specs=pl.BlockSpec((tq, hd), lambda b, h, i: (b * nq + i, h)),
        out_shape=jax.ShapeDtypeStruct((m, width), BF16),
        scratch_shapes=[pltpu.VMEM((1, 2 * tq), F32), pltpu.VMEM((1, 2 * tq), F32), pltpu.VMEM((hd, 2 * tq), F32)],
        compiler_params=_params("parallel", "parallel", "arbitrary"),
        name="diff_attn_prompt",
    )(lq, lk, g.reshape(hd, 1), q, k, v_t)


def _order_key(score):
    score = jnp.where(score == 0.0, 0.0, score)
    bits = lax.bitcast_convert_type(score, jnp.int32)
    return jnp.where(bits < 0, bits ^ jnp.int32(0x7FFFFFFF), bits)


def _count_rows(mask):
    return jnp.sum(mask.astype(jnp.int32), axis=1, keepdims=True)


def _kth_largest_key(load_keys, shape, k, count_fn):
    t0 = jnp.full(shape, INT_MIN, jnp.int32)
    zero = jnp.zeros(shape, jnp.int32)
    t0 = jnp.where(count_fn(load_keys() >= zero) >= k, zero, t0)

    def body(it, t):
        cand = t | (jnp.int32(1) << (30 - it))
        return jnp.where(count_fn(load_keys() >= cand) >= k, cand, t)

    return lax.fori_loop(0, 31, body, t0)


def _tie_cut(load_eq_pos, shape, need, n_bits, count_fn):
    def body(it, x):
        cand = x | (jnp.int32(1) << (n_bits - 1 - it))
        eq, pos = load_eq_pos()
        return jnp.where(count_fn(eq & (pos < cand)) < need, cand, x)

    return lax.fori_loop(0, n_bits, body, jnp.zeros(shape, jnp.int32))


def _dsa_prompt_kernel(qi_ref, kiwi_k_ref, kiwi_q_ref, qb_ref, kb_ref, vb_ref, o_ref,
                       kdup_ref, wb_ref, score_ref, key_ref, bias_ref, cut_ref,
                       *, tq, seq, extents, n_idx, d_idx, n_heads, n_kv, dh, topk):
    i = pl.program_id(1)

    @pl.when(i == 0)
    def _():
        kf = kiwi_k_ref[...]
        lane = lax.broadcasted_iota(jnp.int32, kf.shape, 1)
        klo = jnp.where(lane < d_idx, kf, 0.0)
        kdup_ref[...] = (klo + pltpu.roll(klo, d_idx, 1)).astype(kdup_ref.dtype)

    wscale = n_idx ** -0.5 * d_idx ** -0.5
    c_exp = dh ** -0.5 * LOG2_E
    rep_shift = (n_heads // n_kv).bit_length() - 1

    w_all = kiwi_q_ref[...] * wscale
    for h in range(n_idx):
        wb_ref[h] = jnp.broadcast_to(w_all[:, d_idx + h:d_idx + h + 1], (tq, LANES))

    def window(j, width):
        return pl.ds(pl.multiple_of(j * width, width), width)

    def run(ncol):
        score_ref[:, :ncol] = jnp.zeros((tq, ncol), F32)

        def head_pair(j, carry):
            grp = qi_ref[:, window(j, LANES)].astype(F32)
            lane = lax.broadcasted_iota(jnp.int32, grp.shape, 1)
            for half in range(2):
                keep = (lane < d_idx) if half == 0 else (lane >= d_idx)
                qh = jnp.where(keep, grp, 0.0).astype(qi_ref.dtype)
                w = wb_ref[2 * j + half][:, 0:1]
                score_ref[:, :ncol] += jnp.maximum(_dot_nt(qh, kdup_ref[:ncol, :]), 0.0) * w
            return carry

        lax.fori_loop(0, n_idx // 2, head_pair, 0)

        row = i * tq + lax.broadcasted_iota(jnp.int32, (tq, ncol), 0)
        col = lax.broadcasted_iota(jnp.int32, (tq, ncol), 1)
        causal = col <= row
        key_ref[:, :ncol] = jnp.where(causal, _order_key(score_ref[:, :ncol]), INT_MIN)

        t = _kth_largest_key(lambda: key_ref[:, :ncol], (tq, 1), topk, _count_rows)
        key = key_ref[:, :ncol]
        eq = key == t
        need = topk - _count_rows(key > t)
        tie = (_count_rows(eq) > need) & (t > INT_MIN)
        cut_ref[...] = jnp.full((tq, 1), ncol, jnp.int32)

        @pl.when(jnp.max(tie.astype(jnp.int32)) > 0)
        def _():
            def load():
                return key_ref[:, :ncol] == t, lax.broadcasted_iota(jnp.int32, (tq, ncol), 1)
            cut_ref[...] = _tie_cut(load, (tq, 1), need, int(math.log2(seq)), _count_rows)

        sel = ((key > t) | (eq & (col <= cut_ref[...]))) & causal
        bias_ref[:, :ncol] = jnp.where(sel, 0.0, -jnp.inf)

        def head(h, carry):
            g = h >> rep_shift
            s = _dot_nt(qb_ref[:, window(h, dh)], kb_ref[:ncol, window(g, dh)]) * c_exp + bias_ref[:, :ncol]
            p = jnp.exp2(s - jnp.max(s, axis=1, keepdims=True))
            l = jnp.sum(p, axis=1, keepdims=True)
            o = _dot(p.astype(vb_ref.dtype), vb_ref[:ncol, window(g, dh)]) / l
            o_ref[:, window(h, dh)] = o.astype(o_ref.dtype)
            return carry

        lax.fori_loop(0, n_heads, head, 0)

    lo = 0
    for ncol in extents:
        @pl.when(((i + 1) * tq > lo) & ((i + 1) * tq <= ncol))
        def _(ncol=ncol):
            run(ncol)
        lo = ncol


def dsa_prompt(qi, kiwi, qb, kb, vb, *, batch, seq, n_idx, d_idx, n_heads, n_kv, dh, topk):
    m = qi.shape[0]
    rep = n_heads // n_kv
    assert 2 * d_idx == LANES and dh == LANES and seq & (seq - 1) == 0 and rep & (rep - 1) == 0 and n_idx % 2 == 0
    tq = _pick(seq, 256, LANES)
    nq = seq // tq
    n_ext = min(4, nq)
    extents = tuple(seq * (c + 1) // n_ext for c in range(n_ext))
    return pl.pallas_call(
        functools.partial(_dsa_prompt_kernel, tq=tq, seq=seq, extents=extents, n_idx=n_idx, d_idx=d_idx,
                          n_heads=n_heads,
                          n_kv=n_kv, dh=dh, topk=topk),
        grid=(batch, nq),
        in_specs=[
            pl.BlockSpec((tq, qi.shape[1]), lambda b, i: (b * nq + i, 0)),
            pl.BlockSpec((seq, LANES), lambda b, i: (b, 0)),
            pl.BlockSpec((tq, LANES), lambda b, i: (b * nq + i, 0)),
            pl.BlockSpec((tq, qb.shape[1]), lambda b, i: (b * nq + i, 0)),
            pl.BlockSpec((seq, kb.shape[1]), lambda b, i: (b, 0)),
            pl.BlockSpec((seq, vb.shape[1]), lambda b, i: (b, 0)),
        ],
        out_specs=pl.BlockSpec((tq, qb.shape[1]), lambda b, i: (b * nq + i, 0)),
        out_shape=jax.ShapeDtypeStruct((m, qb.shape[1]), BF16),
        scratch_shapes=[pltpu.VMEM((seq, LANES), BF16), pltpu.VMEM((n_idx, tq, LANES), F32),
                        pltpu.VMEM((tq, seq), F32), pltpu.VMEM((tq, seq), jnp.int32),
                        pltpu.VMEM((tq, seq), F32), pltpu.VMEM((tq, 1), jnp.int32)],
        compiler_params=_params("parallel", "arbitrary"),
        name="dsa_prompt",
    )(qi, kiwi, kiwi, qb, kb, vb)


def _merge_kernel(oa_ref, ob_ref, wa_ref, wb_ref, ga_ref, gb_ref, o_ref):
    ya = _dot(oa_ref[...], wa_ref[...])
    yb = _dot(ob_ref[...], wb_ref[...])
    u = jax.nn.sigmoid(ga_ref[...]) * ya + jax.nn.sigmoid(gb_ref[...]) * yb
    o_ref[...] = u.astype(o_ref.dtype)


def merge_branches(oa, ob, w_pa, w_pb, ga, gb):
    m, ka = oa.shape
    kb = ob.shape[1]
    n = w_pa.shape[1]
    tm = _pick(m, 1024, 8)
    tn = _pick(n, 512, LANES)
    return pl.pallas_call(
        _merge_kernel,
        grid=(m // tm, n // tn),
        in_specs=[
            pl.BlockSpec((tm, ka), lambda i, j: (i, 0)),
            pl.BlockSpec((tm, kb), lambda i, j: (i, 0)),
            pl.BlockSpec((ka, tn), lambda i, j: (0, j)),
            pl.BlockSpec((kb, tn), lambda i, j: (0, j)),
            pl.BlockSpec((tm, tn), lambda i, j: (i, j)),
            pl.BlockSpec((tm, tn), lambda i, j: (i, j)),
        ],
        out_specs=pl.BlockSpec((tm, tn), lambda i, j: (i, j)),
        out_shape=jax.ShapeDtypeStruct((m, n), oa.dtype),
        compiler_params=_params("parallel", "arbitrary"),
        name="merge_branches",
    )(oa, ob, w_pa, w_pb, ga, gb)


def _matmul_res_kernel(a_ref, w_ref, r_ref, o_ref):
    o_ref[...] = r_ref[...] + _dot(a_ref[...], w_ref[...])


def matmul_residual(a, w, res):
    m, k = a.shape
    n = w.shape[1]
    tm = _pick(m, 1024, 8)
    tn = _pick(n, 512, LANES)
    return pl.pallas_call(
        _matmul_res_kernel,
        grid=(m // tm, n // tn),
        in_specs=[
            pl.BlockSpec((tm, k), lambda i, j: (i, 0)),
            pl.BlockSpec((k, tn), lambda i, j: (0, j)),
            pl.BlockSpec((tm, tn), lambda i, j: (i, j)),
        ],
        out_specs=pl.BlockSpec((tm, tn), lambda i, j: (i, j)),
        out_shape=jax.ShapeDtypeStruct((m, n), F32),
        compiler_params=_params("parallel", "arbitrary"),
        name="matmul_residual",
    )(a, w, res)


def _gateup_kernel(x_ref, wg_ref, wu_ref, o_ref, *w_cast):
    x = x_ref[...]
    if w_cast:
        wg_c, wu_c = w_cast

        @pl.when(pl.program_id(1) == 0)
        def _():
            wg_c[...] = wg_ref[...].astype(wg_c.dtype)
            wu_c[...] = wu_ref[...].astype(wu_c.dtype)

        wg_ref, wu_ref = wg_c, wu_c
    gate = _dot(x, wg_ref[...])
    up = _dot(x, wu_ref[...])
    o_ref[...] = (jax.nn.silu(gate) * up).astype(o_ref.dtype)


def gate_up(xn, wg, wu):
    m, k = xn.shape
    n = wg.shape[1]
    tm = _pick(m, 1024, 8)
    tn = _pick(n, 512, LANES)
    scratch = [] if wg.dtype == xn.dtype else [pltpu.VMEM((k, tn), xn.dtype), pltpu.VMEM((k, tn), xn.dtype)]
    return pl.pallas_call(
        _gateup_kernel,
        grid=(n // tn, m // tm),
        in_specs=[
            pl.BlockSpec((tm, k), lambda j, i: (i, 0)),
            pl.BlockSpec((k, tn), lambda j, i: (0, j)),
            pl.BlockSpec((k, tn), lambda j, i: (0, j)),
        ],
        out_specs=pl.BlockSpec((tm, tn), lambda j, i: (i, j)),
        out_shape=jax.ShapeDtypeStruct((m, n), xn.dtype),
        scratch_shapes=scratch,
        compiler_params=_params("parallel", "arbitrary"),
        name="ffn_gate_up",
    )(xn, wg, wu)


def _router_kernel(x_ref, g_ref, wr_ref, base_ref, hn_ref, gate_ref, route_ref, cnt_ref, run_ref, *, n_exp):
    @pl.when(pl.program_id(0) == 0)
    def _():
        run_ref[...] = base_ref[...]

    hn = _rms(x_ref[...], g_ref[...])
    hn_ref[...] = hn
    tm = hn.shape[0]
    lane = lax.broadcasted_iota(jnp.int32, (tm, LANES), 1)
    logits = jnp.full((tm, LANES), -jnp.inf, F32)
    for e in range(n_exp):
        le = jnp.sum(hn * wr_ref[e:e + 1, :], axis=1, keepdims=True)
        logits = jnp.where(lane == e, le, logits)
    v1 = jnp.max(logits, axis=1, keepdims=True)
    i1 = jnp.min(jnp.where(logits == v1, lane, LANES), axis=1, keepdims=True)
    rest = jnp.where(lane == i1, -jnp.inf, logits)
    v2 = jnp.max(rest, axis=1, keepdims=True)
    i2 = jnp.min(jnp.where(rest == v2, lane, LANES), axis=1, keepdims=True)
    e2 = jnp.exp(v2 - v1)
    den = 1.0 + e2
    gate_ref[...] = jnp.where(lane == 0, 1.0 / den, jnp.where(lane == 1, e2 / den, 0.0))

    onehot = jnp.where((lane == i1) | (lane == i2), 1.0, 0.0)
    r = lax.broadcasted_iota(jnp.int32, (tm, tm), 0)
    c = lax.broadcasted_iota(jnp.int32, (tm, tm), 1)
    before = _dot(jnp.where(c < r, 1.0, 0.0).astype(BF16), onehot.astype(BF16)) + run_ref[...]
    r1 = jnp.sum(jnp.where(lane == i1, before, 0.0), axis=1, keepdims=True).astype(jnp.int32)
    r2 = jnp.sum(jnp.where(lane == i2, before, 0.0), axis=1, keepdims=True).astype(jnp.int32)
    route_ref[...] = jnp.where(lane == 0, i1, jnp.where(lane == 1, i2, jnp.where(lane == 2, r1,
                               jnp.where(lane == 3, r2, 0))))
    run_ref[...] = run_ref[...] + jnp.sum(onehot, axis=0, keepdims=True)
    cnt_ref[...] = run_ref[...]


def moe_router(x, g, w_router, base_counts):
    m, d = x.shape
    n_exp = w_router.shape[1]
    assert TOP_K_EXPERTS == 2 and n_exp <= LANES
    tm = _pick(m, 256, 8)
    row = lambda i: (i, 0)
    fixed = lambda i: (0, 0)
    return pl.pallas_call(
        functools.partial(_router_kernel, n_exp=n_exp),
        grid=(m // tm,),
        in_specs=[
            pl.BlockSpec((tm, d), row),
            pl.BlockSpec((1, d), fixed),
            pl.BlockSpec((n_exp, d), fixed),
            pl.BlockSpec((1, LANES), fixed),
        ],
        out_specs=[pl.BlockSpec((tm, d), row), pl.BlockSpec((tm, LANES), row), pl.BlockSpec((tm, LANES), row),
                   pl.BlockSpec((1, LANES), fixed)],
        out_shape=[jax.ShapeDtypeStruct((m, d), F32), jax.ShapeDtypeStruct((m, LANES), F32),
                   jax.ShapeDtypeStruct((m, LANES), jnp.int32), jax.ShapeDtypeStruct((1, LANES), F32)],
        scratch_shapes=[pltpu.VMEM((1, LANES), F32)],
        compiler_params=_params("arbitrary"),
        name="moe_router",
    )(x, g.reshape(1, d), w_router.T, base_counts)


def _row_copy(src_ref, src_row, dst_ref, dst_row, sem):
    return pltpu.make_async_copy(src_ref.at[pl.ds(src_row, 1)], dst_ref.at[pl.ds(dst_row, 1)], sem)


def _dispatch_kernel(pos1_ref, pos2_ref, x_ref, xs_in_ref, xs_ref, sem, *, tm):
    del xs_in_ref
    base = pl.program_id(0) * tm

    def start(r, carry):
        _row_copy(x_ref, r, xs_ref, pos1_ref[base + r], sem).start()
        _row_copy(x_ref, r, xs_ref, pos2_ref[base + r], sem).start()
        return carry

    def wait(r, carry):
        _row_copy(x_ref, 0, xs_ref, 0, sem).wait()
        _row_copy(x_ref, 0, xs_ref, 0, sem).wait()
        return carry

    lax.fori_loop(0, tm, start, 0)
    lax.fori_loop(0, tm, wait, 0)


def moe_dispatch(hn, pos1, pos2, x_sorted):
    m, d = hn.shape
    tm = _pick(m, 256, 8)
    grid_spec = pltpu.PrefetchScalarGridSpec(
        num_scalar_prefetch=2,
        grid=(m // tm,),
        in_specs=[pl.BlockSpec((tm, d), lambda i, p1, p2: (i, 0)), pl.BlockSpec(memory_space=pl.ANY)],
        out_specs=pl.BlockSpec(memory_space=pl.ANY),
        scratch_shapes=[pltpu.SemaphoreType.DMA(())],
    )
    return pl.pallas_call(
        functools.partial(_dispatch_kernel, tm=tm),
        grid_spec=grid_spec,
        out_shape=jax.ShapeDtypeStruct(x_sorted.shape, x_sorted.dtype),
        input_output_aliases={3: 0},
        compiler_params=_params("arbitrary"),
        name="moe_dispatch",
    )(pos1, pos2, hn, x_sorted)


def _combine_kernel(pos1_ref, pos2_ref, x_ref, gate_ref, ys_ref, o_ref, buf_ref, sem, *, tm):
    base = pl.program_id(0) * tm

    def start(r, carry):
        _row_copy(ys_ref, pos1_ref[base + r], buf_ref.at[0], r, sem).start()
        _row_copy(ys_ref, pos2_ref[base + r], buf_ref.at[1], r, sem).start()
        return carry

    def wait(r, carry):
        _row_copy(ys_ref, 0, buf_ref.at[0], 0, sem).wait()
        _row_copy(ys_ref, 0, buf_ref.at[1], 0, sem).wait()
        return carry

    lax.fori_loop(0, tm, start, 0)
    lax.fori_loop(0, tm, wait, 0)
    gate = gate_ref[...]
    o_ref[...] = x_ref[...] + gate[:, 0:1] * buf_ref[0] + gate[:, 1:2] * buf_ref[1]


def moe_combine(x, gates, pos1, pos2, y_sorted):
    m, d = x.shape
    tm = _pick(m, 256, 8)
    row = lambda i, p1, p2: (i, 0)
    grid_spec = pltpu.PrefetchScalarGridSpec(
        num_scalar_prefetch=2,
        grid=(m // tm,),
        in_specs=[pl.BlockSpec((tm, d), row), pl.BlockSpec((tm, LANES), row), pl.BlockSpec(memory_space=pl.ANY)],
        out_specs=pl.BlockSpec((tm, d), row),
        scratch_shapes=[pltpu.VMEM((2, tm, d), F32), pltpu.SemaphoreType.DMA(())],
    )
    return pl.pallas_call(
        functools.partial(_combine_kernel, tm=tm),
        grid_spec=grid_spec,
        out_shape=jax.ShapeDtypeStruct((m, d), F32),
        compiler_params=_params("arbitrary"),
        name="moe_combine",
    )(pos1, pos2, x, gates, y_sorted)


def _expert_gateup_kernel(te_ref, nu_ref, x_ref, wg_ref, wu_ref, o_ref, wg_c, wu_c):
    i = pl.program_id(1)
    used = i < nu_ref[0]
    fresh = (i == 0) | (te_ref[i] != te_ref[jnp.maximum(i - 1, 0)])

    @pl.when(used & fresh)
    def _():
        wg_c[...] = wg_ref[0].astype(wg_c.dtype)
        wu_c[...] = wu_ref[0].astype(wu_c.dtype)

    @pl.when(used)
    def _():
        x = x_ref[...]
        gate = _dot(x, wg_c[...])
        up = _dot(x, wu_c[...])
        o_ref[...] = (jax.nn.silu(gate) * up).astype(o_ref.dtype)

    @pl.when(jnp.logical_not(used))
    def _():
        o_ref[...] = jnp.zeros(o_ref.shape, o_ref.dtype)


def expert_gate_up(x_sorted, wg, wu, tile_expert, n_used, tm):
    p_rows, d = x_sorted.shape
    ff = wg.shape[2]
    tn = _pick(ff, 1024, LANES)

    def w_map(j, i, te, nu):
        return (te[i], 0, j)

    grid_spec = pltpu.PrefetchScalarGridSpec(
        num_scalar_prefetch=2,
        grid=(ff // tn, p_rows // tm),
        in_specs=[
            pl.BlockSpec((tm, d), lambda j, i, te, nu: (jnp.minimum(i, nu[0] - 1), 0)),
            pl.BlockSpec((1, d, tn), w_map),
            pl.BlockSpec((1, d, tn), w_map),
        ],
        out_specs=pl.BlockSpec((tm, tn), lambda j, i, te, nu: (i, j)),
        scratch_shapes=[pltpu.VMEM((d, tn), BF16), pltpu.VMEM((d, tn), BF16)],
    )
    return pl.pallas_call(
        _expert_gateup_kernel,
        grid_spec=grid_spec,
        out_shape=jax.ShapeDtypeStruct((p_rows, ff), BF16),
        compiler_params=_params("parallel", "arbitrary"),
        name="expert_gate_up",
    )(tile_expert, n_used, x_sorted, wg, wu)


def _expert_down_kernel(te_ref, nu_ref, h_ref, wd_ref, o_ref):
    @pl.when(pl.program_id(0) < nu_ref[0])
    def _():
        o_ref[...] = _dot(h_ref[...], wd_ref[0])

    @pl.when(pl.program_id(0) >= nu_ref[0])
    def _():
        o_ref[...] = jnp.zeros(o_ref.shape, o_ref.dtype)


def expert_down(h_sorted, wd, tile_expert, n_used, tm):
    p_rows, ff = h_sorted.shape
    d = wd.shape[2]
    tn = _pick(d, 512, LANES)
    nj = d // tn
    grid_spec = pltpu.PrefetchScalarGridSpec(
        num_scalar_prefetch=2,
        grid=(p_rows // tm, nj),
        in_specs=[
            pl.BlockSpec((tm, ff), lambda i, j, te, nu: (jnp.minimum(i, nu[0] - 1), 0)),
            pl.BlockSpec((1, ff, tn), lambda i, j, te, nu: (te[i], 0, jnp.where(i < nu[0], j, nj - 1))),
        ],
        out_specs=pl.BlockSpec((tm, tn), lambda i, j, te, nu: (i, j)),
    )
    return pl.pallas_call(
        _expert_down_kernel,
        grid_spec=grid_spec,
        out_shape=jax.ShapeDtypeStruct((p_rows, d), F32),
        compiler_params=_params("parallel", "arbitrary"),
        name="expert_down",
    )(tile_expert, n_used, h_sorted, wd)


MOE_ROW_TILE = 512
SAMPLE_ATTN_PAGES = 8
SAMPLE_DSA_PAGES = 16


def moe_ffn(xp, xs, g, w_router, wg, wu, wd):
    n_exp = w_router.shape[1]
    d = xp.shape[1]
    tm = MOE_ROW_TILE
    zeros = jnp.zeros((1, LANES), F32)
    hn_p, gate_p, route_p, cnt_p = moe_router(xp, g, w_router, zeros)
    hn_s, gate_s, route_s, cnt = moe_router(xs, g, w_router, cnt_p)

    counts = cnt[0, :n_exp].astype(jnp.int32)
    sizes = (counts + tm - 1) // tm * tm
    ends = jnp.cumsum(sizes)
    starts = ends - sizes
    n_assign = TOP_K_EXPERTS * (xp.shape[0] + xs.shape[0])
    n_tiles = (n_assign + n_exp * (tm - 1) + tm - 1) // tm
    tile_expert = jnp.minimum(jnp.sum(jnp.arange(n_tiles)[:, None] * tm >= ends[None, :], axis=1), n_exp - 1)
    tile_expert = tile_expert.astype(jnp.int32)
    n_used = (ends[-1:] // tm).astype(jnp.int32)

    def slots(route):
        return (jnp.take(starts, route[:, 0]) + route[:, 2], jnp.take(starts, route[:, 1]) + route[:, 3])

    p1_p, p2_p = slots(route_p)
    p1_s, p2_s = slots(route_s)
    x_sorted = jnp.zeros((n_tiles * tm, d), F32)
    x_sorted = moe_dispatch(hn_p, p1_p, p2_p, x_sorted)
    x_sorted = moe_dispatch(hn_s, p1_s, p2_s, x_sorted)
    h_sorted = expert_gate_up(x_sorted.astype(BF16), wg, wu, tile_expert, n_used, tm)
    y_sorted = expert_down(h_sorted, wd.astype(BF16), tile_expert, n_used, tm)
    return (moe_combine(xp, gate_p, p1_p, p2_p, y_sorted), moe_combine(xs, gate_s, p1_s, p2_s, y_sorted))


def _sample_attn_kernel(pt_ref, lq_ref, lk_ref, g_ref, qa_ref, qi_ref, wi_ref, *rest,
                        n_steps, group, n_heads, dqk, lam_init, n_idx, d_idx):
    kp_refs, vp_refs, ip_refs = rest[:group], rest[group:2 * group], rest[2 * group:3 * group]
    kn_ref, vn_ref, in_ref, o_ref, sc_ref, scn_ref, m_ref, l_ref, acc_ref = rest[3 * group:]
    p = pl.program_id(1)
    hd = 2 * dqk
    page = kp_refs[0].shape[2]
    cols = page * n_heads
    scale = dqk ** -0.5

    q = qa_ref[0].astype(F32)
    lane = lax.broadcasted_iota(jnp.int32, q.shape, 1)
    qs = jnp.concatenate([jnp.where(lane < dqk, q, 0.0), jnp.where(lane >= dqk, q, 0.0)], axis=0)

    @pl.when(p == 0)
    def _():
        m_ref[...] = jnp.full(m_ref.shape, -jnp.inf, F32)
        l_ref[...] = jnp.zeros(l_ref.shape, F32)
        acc_ref[...] = jnp.zeros(acc_ref.shape, F32)

    k2 = jnp.concatenate([r[0, 0].reshape(cols, hd).astype(BF16) for r in kp_refs], axis=0)
    v2 = jnp.concatenate([r[0, 0].reshape(cols, hd).astype(BF16) for r in vp_refs], axis=0)
    s = _dot_nt(qs.astype(BF16), k2) * scale
    r_id = lax.broadcasted_iota(jnp.int32, s.shape, 0) & (n_heads - 1)
    c_id = lax.broadcasted_iota(jnp.int32, s.shape, 1) & (n_heads - 1)
    s = jnp.where(r_id == c_id, s, -jnp.inf)
    m_old = m_ref[...]
    m_new = jnp.maximum(m_old, jnp.max(s, axis=1, keepdims=True))
    alpha = jnp.exp(m_old - m_new)
    pr = jnp.exp(s - m_new)
    l_ref[...] = alpha * l_ref[...] + jnp.sum(pr, axis=1, keepdims=True)
    acc_ref[...] = alpha * acc_ref[...] + _dot(pr.astype(BF16), v2)
    m_ref[...] = m_new

    wscale = n_idx ** -0.5 * d_idx ** -0.5
    qi = qi_ref[0]
    w = wi_ref[0] * wscale
    for k, ip_ref in enumerate(ip_refs):
        si = jnp.maximum(_dot(qi, ip_ref[0, 0]), 0.0) * w
        sc_ref[0, k:k + 1, :] = jnp.sum(si, axis=0, keepdims=True)

    @pl.when(p == n_steps - 1)
    def _():
        kn = jnp.concatenate([kn_ref[0], kn_ref[0]], axis=0)
        vn = jnp.concatenate([vn_ref[0], vn_ref[0]], axis=0)
        s_new = jnp.sum(qs * kn, axis=1, keepdims=True) * scale
        m_o = m_ref[...]
        m_n = jnp.maximum(m_o, s_new)
        a = jnp.exp(m_o - m_n)
        p_new = jnp.exp(s_new - m_n)
        l_fin = a * l_ref[...] + p_new
        o_all = (a * acc_ref[...] + p_new * vn) / l_fin
        lam = _lambda(lq_ref, lk_ref, lam_init)
        o = o_all[:n_heads] - lam * o_all[n_heads:]
        o_ref[0] = (_rms(o, g_ref[...]) * (1.0 - lam_init)).astype(o_ref.dtype)
        kin = in_ref[0]
        s_in = jnp.maximum(jnp.sum(qi.astype(F32) * kin, axis=1, keepdims=True), 0.0) * w
        scn_ref[0] = jnp.broadcast_to(jnp.sum(s_in, axis=0, keepdims=True), (1, LANES))


def sample_attention(page_table, layer, cache_k, cache_v, cache_i_t, qa, qi, wi, k_new, v_new, i_new, lq, lk, g,
                     *, n_heads, dqk, lam_init, n_idx, d_idx):
    db, n_pages = page_table.shape
    width = qa.shape[1]
    page = cache_k.shape[2]
    hd = 2 * dqk
    rows = 2 * n_heads
    assert n_heads & (n_heads - 1) == 0
    pt = page_table.reshape(-1)
    group = SAMPLE_ATTN_PAGES if n_pages % SAMPLE_ATTN_PAGES == 0 else 1
    n_steps = n_pages // group

    def pool5(k):
        return lambda b, p, pt_ref: (layer, pt_ref[b * n_pages + p * group + k], 0, 0, 0)

    def pool4(k):
        return lambda b, p, pt_ref: (layer, pt_ref[b * n_pages + p * group + k], 0, 0)

    row3 = lambda b, p, pt_ref: (b, 0, 0)
    const2 = lambda b, p, pt_ref: (0, 0)
    grid_spec = pltpu.PrefetchScalarGridSpec(
        num_scalar_prefetch=1,
        grid=(db, n_steps),
        in_specs=[
            pl.BlockSpec(lq.shape, const2),
            pl.BlockSpec(lk.shape, const2),
            pl.BlockSpec((1, hd), const2),
            pl.BlockSpec((1, n_heads, hd), row3),
            pl.BlockSpec((1, n_idx, d_idx), row3),
            pl.BlockSpec((1, n_idx, 1), row3),
            *[pl.BlockSpec((1, 1, page, n_heads, hd), pool5(k)) for k in range(group)],
            *[pl.BlockSpec((1, 1, page, n_heads, hd), pool5(k)) for k in range(group)],
            *[pl.BlockSpec((1, 1, d_idx, page), pool4(k)) for k in range(group)],
            pl.BlockSpec((1, n_heads, hd), row3),
            pl.BlockSpec((1, n_heads, hd), row3),
            pl.BlockSpec((1, 1, d_idx), row3),
        ],
        out_specs=[
            pl.BlockSpec((1, n_heads, hd), row3),
            pl.BlockSpec((1, group, page), lambda b, p, pt_ref: (b * n_steps + p, 0, 0)),
            pl.BlockSpec((1, 1, LANES), row3),
        ],
        scratch_shapes=[pltpu.VMEM((rows, 1), F32), pltpu.VMEM((rows, 1), F32), pltpu.VMEM((rows, hd), F32)],
    )
    out, scores, score_new = pl.pallas_call(
        functools.partial(_sample_attn_kernel, n_steps=n_steps, group=group, n_heads=n_heads, dqk=dqk,
                          lam_init=lam_init, n_idx=n_idx, d_idx=d_idx),
        grid_spec=grid_spec,
        out_shape=[jax.ShapeDtypeStruct((db, n_heads, hd), F32),
                   jax.ShapeDtypeStruct((db * n_steps, group, page), F32),
                   jax.ShapeDtypeStruct((db, 1, LANES), F32)],
        compiler_params=_params("parallel", "arbitrary"),
        name="sample_attention",
    )(pt, lq, lk, g.reshape(1, hd), qa.reshape(db, n_heads, hd), qi.reshape(db, n_idx, d_idx),
      wi.reshape(db, n_idx, 1), *([cache_k] * group), *([cache_v] * group), *([cache_i_t] * group),
      k_new.reshape(db, n_heads, hd), v_new.reshape(db, n_heads, hd), i_new.reshape(db, 1, d_idx))
    return out.reshape(db, width), scores.reshape(db, n_pages, page), score_new


def _count_all(mask):
    c = jnp.sum(mask.astype(jnp.int32), axis=1, keepdims=True)
    return jnp.sum(c, axis=0, keepdims=True)


def _sample_dsa_kernel(pt_ref, sc_ref, scn_ref, qb_ref, *rest, n_pages, group, page, n_heads, n_kv, dh, topk):
    kp_refs, vp_refs = rest[:group], rest[group:2 * group]
    kn_ref, vn_ref, o_ref, key_ref, keyn_ref, t_ref, cut_ref, selx_ref, m_ref, l_ref, acc_ref = rest[2 * group:]
    p = pl.program_id(1)
    rep = n_heads // n_kv
    scale = dh ** -0.5
    past = n_pages * page
    cols = page * n_kv
    kv_shift = n_kv.bit_length() - 1
    rep_shift = rep.bit_length() - 1

    @pl.when(p == 0)
    def _():
        m_ref[...] = jnp.full(m_ref.shape, -jnp.inf, F32)
        l_ref[...] = jnp.zeros(l_ref.shape, F32)
        acc_ref[...] = jnp.zeros(acc_ref.shape, F32)
        key_ref[...] = _order_key(sc_ref[0])
        keyn_ref[...] = _order_key(scn_ref[0])
        key_new = keyn_ref[:, 0:1]

        def cnt(t):
            return _count_all(key_ref[...] >= t) + (key_new >= t).astype(jnp.int32)

        t0 = jnp.full((1, 1), INT_MIN, jnp.int32)
        zero = jnp.zeros((1, 1), jnp.int32)
        t0 = jnp.where(cnt(zero) >= topk, zero, t0)

        def body(it, t):
            cand = t | (jnp.int32(1) << (30 - it))
            return jnp.where(cnt(cand) >= topk, cand, t)

        t = lax.fori_loop(0, 31, body, t0)
        t_ref[...] = t
        key = key_ref[...]
        n_gt = _count_all(key > t) + (key_new > t).astype(jnp.int32)
        need = topk - n_gt
        pos = (lax.broadcasted_iota(jnp.int32, key.shape, 0) * page
               + lax.broadcasted_iota(jnp.int32, key.shape, 1))
        n_bits = int(math.ceil(math.log2(past + 1)))

        def body2(it, x):
            cand = x | (jnp.int32(1) << (n_bits - 1 - it))
            c = _count_all((key_ref[...] == t) & (pos < cand)) + ((key_new == t) & (past < cand)).astype(jnp.int32)
            return jnp.where(c < need, cand, x)

        cut = lax.fori_loop(0, n_bits, body2, jnp.zeros((1, 1), jnp.int32))
        cut_ref[...] = cut
        sel = (key > t) | ((key == t) & (pos <= cut))
        e_t = lax.broadcasted_iota(jnp.int32, (page, cols), 0)
        e_c = lax.broadcasted_iota(jnp.int32, (page, cols), 1)
        spread = jnp.where((e_c >> kv_shift) == e_t, 1.0, 0.0).astype(BF16)
        selx_ref[...] = _dot(jnp.where(sel, 1.0, 0.0).astype(BF16), spread)

    t = t_ref[...]
    cut = cut_ref[...]
    q = qb_ref[0]

    k2 = jnp.concatenate([r[0, 0].astype(BF16) for r in kp_refs], axis=0)
    v2 = jnp.concatenate([r[0, 0].astype(BF16) for r in vp_refs], axis=0)
    sel_cols = jnp.concatenate(
        [jnp.broadcast_to(selx_ref[pl.ds(p * group + k, 1), :], (n_heads, cols)) for k in range(group)], axis=1)
    r_id = lax.broadcasted_iota(jnp.int32, (n_heads, group * cols), 0)
    c_id = lax.broadcasted_iota(jnp.int32, (n_heads, group * cols), 1)
    keep = (sel_cols > 0.5) & ((c_id & (n_kv - 1)) == (r_id >> rep_shift))
    s = _dot_nt(q.astype(BF16), k2) * scale
    s = jnp.where(keep, s, -jnp.inf)
    m_old = m_ref[...]
    m_new = jnp.maximum(m_old, jnp.max(s, axis=1, keepdims=True))
    m_safe = jnp.where(m_new == -jnp.inf, 0.0, m_new)
    alpha = jnp.exp(m_old - m_safe)
    pr = jnp.exp(s - m_safe)
    l_ref[...] = alpha * l_ref[...] + jnp.sum(pr, axis=1, keepdims=True)
    acc_ref[...] = alpha * acc_ref[...] + _dot(pr.astype(BF16), v2)
    m_ref[...] = m_new

    @pl.when(p == n_pages // group - 1)
    def _():
        key_new = keyn_ref[:, 0:1]
        sel_new = (key_new > t) | ((key_new == t) & (past <= cut))
        h_id = lax.broadcasted_iota(jnp.int32, (n_heads, dh), 0) >> rep_shift
        kn = jnp.zeros((n_heads, dh), F32)
        vn = jnp.zeros((n_heads, dh), F32)
        for g in range(n_kv):
            kn = jnp.where(h_id == g, kn_ref[0, g:g + 1, :], kn)
            vn = jnp.where(h_id == g, vn_ref[0, g:g + 1, :], vn)
        s_new = jnp.sum(q.astype(F32) * kn, axis=1, keepdims=True) * scale
        s_new = jnp.where(sel_new, s_new, -jnp.inf)
        m_o = m_ref[...]
        m_n = jnp.maximum(m_o, s_new)
        m_s = jnp.where(m_n == -jnp.inf, 0.0, m_n)
        a = jnp.exp(m_o - m_s)
        p_new = jnp.exp(s_new - m_s)
        l_fin = a * l_ref[...] + p_new
        o_ref[0] = ((a * acc_ref[...] + p_new * vn) / l_fin).astype(o_ref.dtype)


def sample_dsa(page_table, layer, cache_k, cache_v, scores, score_new, qb, k_new, v_new,
               *, n_heads, n_kv, dh, topk):
    db, n_pages = page_table.shape
    cols = cache_k.shape[2]
    page = cols // n_kv
    rep = n_heads // n_kv
    assert n_kv & (n_kv - 1) == 0 and rep & (rep - 1) == 0
    pt = page_table.reshape(-1)

    group = SAMPLE_DSA_PAGES if n_pages % SAMPLE_DSA_PAGES == 0 else 1

    def pool_map(k):
        return lambda b, p, pt_ref: (layer, pt_ref[b * n_pages + p * group + k], 0, 0)

    row3 = lambda b, p, pt_ref: (b, 0, 0)
    grid_spec = pltpu.PrefetchScalarGridSpec(
        num_scalar_prefetch=1,
        grid=(db, n_pages // group),
        in_specs=[
            pl.BlockSpec((1, n_pages, page), row3),
            pl.BlockSpec((1, 1, LANES), row3),
            pl.BlockSpec((1, n_heads, dh), row3),
            *[pl.BlockSpec((1, 1, cols, dh), pool_map(k)) for k in range(group)],
            *[pl.BlockSpec((1, 1, cols, dh), pool_map(k)) for k in range(group)],
            pl.BlockSpec((1, n_kv, dh), row3),
            pl.BlockSpec((1, n_kv, dh), row3),
        ],
        out_specs=pl.BlockSpec((1, n_heads, dh), row3),
        scratch_shapes=[pltpu.VMEM((n_pages, page), jnp.int32), pltpu.VMEM((1, LANES), jnp.int32),
                        pltpu.VMEM((1, 1), jnp.int32), pltpu.VMEM((1, 1), jnp.int32),
                        pltpu.VMEM((n_pages, cols), F32),
                        pltpu.VMEM((n_heads, 1), F32), pltpu.VMEM((n_heads, 1), F32),
                        pltpu.VMEM((n_heads, dh), F32)],
    )
    out = pl.pallas_call(
        functools.partial(_sample_dsa_kernel, n_pages=n_pages, group=group, page=page, n_heads=n_heads,
                          n_kv=n_kv, dh=dh, topk=topk),
        grid_spec=grid_spec,
        out_shape=jax.ShapeDtypeStruct((db, n_heads, dh), F32),
        compiler_params=_params("parallel", "arbitrary"),
        name="sample_dsa",
    )(pt, scores, score_new, qb.reshape(db, n_heads, dh), *([cache_k] * group), *([cache_v] * group),
      k_new.reshape(db, n_kv, dh), v_new.reshape(db, n_kv, dh))
    return out.reshape(db, n_heads * dh)


def kernel(x_prompt, x_sample, cache_a_k, cache_a_v, cache_b_k, cache_b_v, cache_idx_k, page_table, w_in, lambda_q, lambda_k, subln_g, w_branch_a, w_branch_b, w_out, norm_mix_g, norm_ffn_g, w_dense_gate, w_dense_up, w_dense_down, w_router, w_exp_gate, w_exp_up, w_exp_down, norm_final_g):
    batch, seq, d_model = x_prompt.shape
    db, dec_seq, _ = x_sample.shape
    assert dec_seq == 1, "one new token per sample row"
    depth, n_pool, page, n_ha, a_width = cache_a_k.shape
    assert page == PAGE_SIZE
    dqk = a_width // 2
    dv = cache_a_v.shape[4]
    n_kv, dh = cache_b_k.shape[3:]
    d_idx = cache_idx_k.shape[3]
    n_hb = w_branch_b.shape[1] // dh
    n_in = w_in.shape[2]
    wa, wva, wqb, wkb = n_ha * 2 * dqk, n_ha * dv, n_hb * dh, n_kv * dh
    n_idx = (n_in - (2 * wa + wva + wqb + 2 * wkb + d_idx + 2 * d_model)) // (d_idx + 1)
    assert dv == 2 * dqk == LANES and dh == LANES and 2 * d_idx == LANES
    n_pages = page_table.shape[1]
    past_len = n_pages * page
    topk_p = min(TOPK_MAX, seq // 4)
    topk_s = min(TOPK_MAX, (past_len + dec_seq) // 4)
    n_exp = w_router.shape[2]

    offs = [0]
    for wdt in (wa, wa, wva, wqb, wkb, wkb, n_idx * d_idx, d_idx, n_idx, d_model, d_model):
        offs.append(offs[-1] + wdt)
    o_qa, o_ka, o_va, o_qb, o_kb, o_vb, o_qi, o_ki, o_wi, o_ga, o_gb, _ = offs

    pos_p = jnp.arange(seq, dtype=F32)
    pos_s = jnp.full((db,), float(past_len), F32)
    tabs = {}
    for name, pos in (("p", pos_p), ("s", pos_s)):
        tabs[name, "a"] = rope_table(pos, dqk)
        tabs[name, "b"] = rope_table(pos, dh)
        tabs[name, "i"] = rope_table(pos, d_idx, valid_lanes=d_idx)

    n_groups = -(-n_in // LANES)
    kind = {}
    for start, stop, name in ((o_qa, o_va, "a"), (o_qb, o_vb, "b"), (o_qi, o_ki, "a"), (o_ki, o_ki + LANES, "i")):
        assert start % LANES == 0 and stop % LANES == 0
        for g in range(start // LANES, stop // LANES):
            kind[g] = name
    identity = jnp.stack([jnp.ones((db, LANES), F32), jnp.zeros((db, LANES), F32), jnp.zeros((db, LANES), F32)])
    tab_s = jnp.concatenate([tabs["s", kind[g]][0] if g in kind else identity for g in range(n_groups)], axis=2)
    halves_s = tuple(tabs["s", kind[g]][1] if g in kind else 0 for g in range(n_groups))

    ck_b = cache_b_k.reshape(depth, n_pool, page * n_kv, dh)
    cv_b = cache_b_v.reshape(depth, n_pool, page * n_kv, dh)
    ci_t = jnp.swapaxes(cache_idx_k, 2, 3)

    xp = x_prompt.reshape(batch * seq, d_model)
    xs = x_sample.reshape(db * dec_seq, d_model)
    rows = {"p": [[] for _ in range(5)], "s": [[] for _ in range(5)]}

    for l in range(depth):
        lam_init = 0.8 - 0.6 * math.exp(-0.3 * l)
        lq, lk = lambda_q[l], lambda_k[l]
        wl = w_in[l].astype(BF16)
        w_kiwi = jnp.pad(wl[:, o_ki:o_ga], ((0, 0), (0, LANES - d_idx - n_idx)))
        mix_w = {"p": dict(pa=w_branch_a[l].astype(BF16), pb=w_branch_b[l].astype(BF16), o=w_out[l].astype(BF16)),
                 "s": dict(pa=w_branch_a[l], pb=w_branch_b[l], o=w_out[l])}

        def in_proj_prompt(x):
            xn = rmsnorm(x, norm_mix_g[l], BF16)
            ta, ha = tabs["p", "a"]
            tb, hb = tabs["p", "b"]
            ti, hi = tabs["p", "i"]
            z = {}
            z["qa"], = project(xn, wl[:, o_qa:o_ka], [BF16], ta, ha)
            z["ka32"], z["ka"] = project(xn, wl[:, o_ka:o_va], [F32, BF16], ta, ha)
            z["va32"], z["va"] = project(xn, wl[:, o_va:o_qb], [F32, BF16])
            z["qb"], = project(xn, wl[:, o_qb:o_kb], [BF16], tb, hb)
            z["kb32"], z["kb"] = project(xn, wl[:, o_kb:o_vb], [F32, BF16], tb, hb)
            z["vb32"], z["vb"] = project(xn, wl[:, o_vb:o_qi], [F32, BF16])
            z["qi"], = project(xn, wl[:, o_qi:o_ki], [BF16], ta, ha)
            z["kiwi"], = project(xn, w_kiwi, [F32], ti, hi)
            z["ga"], = project(xn, wl[:, o_ga:o_gb], [F32])
            z["gb"], = project(xn, wl[:, o_gb:], [F32])
            return z

        def in_proj_sample(x):
            xn = rmsnorm(x, norm_mix_g[l], F32)
            zz = sample_project(xn, w_in, l, tab_s, halves_s)
            z = {name: zz[:, a:b] for name, a, b in (
                ("qa", o_qa, o_ka), ("ka32", o_ka, o_va), ("va32", o_va, o_qb), ("qb", o_qb, o_kb),
                ("kb32", o_kb, o_vb), ("vb32", o_vb, o_qi), ("qi", o_qi, o_ki), ("kiwi", o_ki, o_ki + LANES),
                ("ga", o_ga, o_gb), ("gb", o_gb, n_in))}
            return z

        def mix_out(x, z, oa, ob, grp):
            w = mix_w[grp]
            u = merge_branches(oa, ob, w["pa"], w["pb"], z["ga"], z["gb"])
            return matmul_residual(u, w["o"], x)

        zp = in_proj_prompt(xp)
        oa = diff_attention_prompt(zp["qa"], zp["ka"], zp["va"].T, lq, lk, subln_g[l], batch=batch, seq=seq,
                                   n_heads=n_ha, dqk=dqk, lam_init=lam_init)
        ob = dsa_prompt(zp["qi"], zp["kiwi"], zp["qb"], zp["kb"], zp["vb"], batch=batch, seq=seq, n_idx=n_idx,
                        d_idx=d_idx, n_heads=n_hb, n_kv=n_kv, dh=dh, topk=topk_p)
        xp = mix_out(xp, zp, oa, ob, "p")

        zs = in_proj_sample(xs)
        ki_s = zs["kiwi"][:, :d_idx]
        wi_s = zs["kiwi"][:, d_idx:d_idx + n_idx]
        oa, scores, score_new = sample_attention(
            page_table, l, cache_a_k, cache_a_v, ci_t, zs["qa"], zs["qi"], wi_s, zs["ka32"], zs["va32"], ki_s,
            lq, lk, subln_g[l], n_heads=n_ha, dqk=dqk, lam_init=lam_init, n_idx=n_idx, d_idx=d_idx)
        ob = sample_dsa(page_table, l, ck_b, cv_b, scores, score_new, zs["qb"], zs["kb32"], zs["vb32"],
                        n_heads=n_hb, n_kv=n_kv, dh=dh, topk=topk_s)
        xs = mix_out(xs, zs, oa, ob, "s")

        for grp, z, bt, tt in (("p", zp, batch, seq), ("s", zs, db, dec_seq)):
            rows[grp][0].append(z["ka32"].reshape(bt, tt, n_ha, 2 * dqk))
            rows[grp][1].append(z["va32"].reshape(bt, tt, n_ha, dv))
            rows[grp][2].append(z["kb32"].reshape(bt, tt, n_kv, dh))
            rows[grp][3].append(z["vb32"].reshape(bt, tt, n_kv, dh))
            rows[grp][4].append(z["kiwi"][:, :d_idx].reshape(bt, tt, d_idx))

        i = l // 2
        if l % 2 == 0:
            hp = rmsnorm(xp, norm_ffn_g[l], BF16)
            hs = rmsnorm(xs, norm_ffn_g[l], F32)
            xp = matmul_residual(gate_up(hp, w_dense_gate[i], w_dense_up[i]), w_dense_down[i].astype(BF16), xp)
            xs = matmul_residual(gate_up(hs, w_dense_gate[i], w_dense_up[i]), w_dense_down[i], xs)
        else:
            xp, xs = moe_ffn(xp, xs, norm_ffn_g[l], w_router[i], w_exp_gate[i], w_exp_up[i], w_exp_down[i])

    y_prompt = rmsnorm(xp, norm_final_g, F32).reshape(batch, seq, d_model)
    y_sample = rmsnorm(xs, norm_final_g, F32).reshape(db, dec_seq, d_model)
    outs_p = [jnp.stack(r, axis=0) for r in rows["p"]]
    outs_s = [jnp.stack(r, axis=0) for r in rows["s"]]
    return (y_prompt, y_sample, *outs_p, *outs_s)
```

```python
import functools
import math

import jax
import jax.numpy as jnp
from jax import lax
from jax.experimental import pallas as pl
from jax.experimental.pallas import tpu as pltpu

LANES = 128
VMEM_LIMIT_BYTES = 56 * 1024 * 1024
ROPE_THETA = 500000.0
ROPE_FRAC = 4
RMS_EPS = 1e-6
TOPK_MAX = 256
PAGE_SIZE = 128
TOP_K_EXPERTS = 2
INT_MIN = -2 ** 31
LOG2_E = math.log2(math.e)
F32 = jnp.float32
BF16 = jnp.bfloat16


def _pick(n, pref, mult):
    if n <= pref:
        return n
    t = (pref // mult) * mult
    while t >= mult:
        if n % t == 0:
            return t
        t -= mult
    return n


def _params(*sem):
    return pltpu.CompilerParams(dimension_semantics=sem, vmem_limit_bytes=VMEM_LIMIT_BYTES)


def _precision(a, b):
    return lax.Precision.HIGHEST if a.dtype == F32 and b.dtype == F32 else None


def _dot(a, b):
    return jnp.dot(a, b, preferred_element_type=F32, precision=_precision(a, b))


def _dot_nt(a, b):
    return lax.dot_general(a, b, (((1,), (1,)), ((), ())), preferred_element_type=F32,
                           precision=_precision(a, b))


def _rms(x, g):
    return x * lax.rsqrt(jnp.mean(x * x, axis=-1, keepdims=True) + RMS_EPS) * g


def _rmsnorm_kernel(x_ref, g_ref, o_ref):
    o_ref[...] = _rms(x_ref[...], g_ref[...]).astype(o_ref.dtype)


def rmsnorm(x, g, out_dtype):
    m, d = x.shape
    tm = _pick(m, 512, 8)
    return pl.pallas_call(
        _rmsnorm_kernel,
        grid=(m // tm,),
        in_specs=[pl.BlockSpec((tm, d), lambda i: (i, 0)), pl.BlockSpec((1, d), lambda i: (0, 0))],
        out_specs=pl.BlockSpec((tm, d), lambda i: (i, 0)),
        out_shape=jax.ShapeDtypeStruct((m, d), out_dtype),
        compiler_params=_params("parallel"),
        name="rmsnorm",
    )(x, g.reshape(1, d))


def _proj_kernel(*refs, half, tn):
    if half:
        x_ref, w_ref, tab_ref, *o_refs = refs
    else:
        x_ref, w_ref, *o_refs = refs
    z = _dot_nt(x_ref[...], w_ref[...])
    if not half:
        for o in o_refs:
            o[...] = z.astype(o.dtype)
        return
    c, s_lo, s_hi = tab_ref[0], tab_ref[1], tab_ref[2]
    for g in range(tn // LANES):
        zg = z[:, g * LANES:(g + 1) * LANES]
        r = zg * c + pltpu.roll(zg, LANES - half, 1) * s_lo + pltpu.roll(zg, half, 1) * s_hi
        for o in o_refs:
            o[:, g * LANES:(g + 1) * LANES] = r.astype(o.dtype)


def project(xn, w, out_dtypes, tab=None, half=0, tn_pref=1024):
    m, k = xn.shape
    n = w.shape[0]
    p_rows = m if tab is None else tab.shape[1]
    assert m % p_rows == 0
    tm = _pick(p_rows, 1024, 8)
    tn = _pick(n, tn_pref, LANES)
    in_specs = [pl.BlockSpec((tm, k), lambda i, j: (i, 0)), pl.BlockSpec((tn, k), lambda i, j: (j, 0))]
    args = [xn, w]
    if tab is not None:
        nblk = p_rows // tm
        in_specs.append(pl.BlockSpec((3, tm, LANES), lambda i, j: (0, i % nblk, 0)))
        args.append(tab)
    outs = pl.pallas_call(
        functools.partial(_proj_kernel, half=half if tab is not None else 0, tn=tn),
        grid=(m // tm, n // tn),
        in_specs=in_specs,
        out_specs=[pl.BlockSpec((tm, tn), lambda i, j: (i, j)) for _ in out_dtypes],
        out_shape=[jax.ShapeDtypeStruct((m, n), dt) for dt in out_dtypes],
        compiler_params=_params("parallel", "arbitrary"),
        name="in_proj",
    )(*args)
    return outs


def _sample_proj_kernel(x_ref, w_ref, tab_ref, o_ref, *, n_in, halves):
    k = pl.program_id(0)

    @pl.when(k == 0)
    def _():
        o_ref[...] = jnp.zeros(o_ref.shape, o_ref.dtype)

    o_ref[:, :n_in] += _dot_nt(x_ref[...], w_ref[0])

    @pl.when(k == pl.num_programs(0) - 1)
    def _():
        for g, half in enumerate(halves):
            if half:
                cols = slice(g * LANES, (g + 1) * LANES)
                zg = o_ref[:, cols]
                o_ref[:, cols] = (zg * tab_ref[0, :, cols] + pltpu.roll(zg, LANES - half, 1) * tab_ref[1, :, cols]
                                  + pltpu.roll(zg, half, 1) * tab_ref[2, :, cols])


def sample_project(xn, w_in_t, layer, tab, halves):
    rows, k = xn.shape
    n_in = w_in_t.shape[1]
    width = LANES * len(halves)
    tk = _pick(k, 256, 8)
    return pl.pallas_call(
        functools.partial(_sample_proj_kernel, n_in=n_in, halves=halves),
        grid=(k // tk,),
        in_specs=[
            pl.BlockSpec((rows, tk), lambda kk: (0, kk)),
            pl.BlockSpec((1, n_in, tk), lambda kk: (layer, 0, kk)),
            pl.BlockSpec((3, rows, width), lambda kk: (0, 0, 0)),
        ],
        out_specs=pl.BlockSpec((rows, width), lambda kk: (0, 0)),
        out_shape=jax.ShapeDtypeStruct((rows, width), F32),
        compiler_params=_params("arbitrary"),
        name="sample_in_proj",
    )(xn, w_in_t, tab)


def rope_table(pos, head_dim, valid_lanes=LANES):
    rot = head_dim // ROPE_FRAC
    half = rot // 2
    inv_freq = jnp.power(ROPE_THETA, -jnp.arange(half, dtype=F32) * 2.0 / rot)
    ang = pos[:, None] * inv_freq[None, :]
    cos, sin = jnp.cos(ang), jnp.sin(ang)
    n = pos.shape[0]
    ones = jnp.ones((n, head_dim - rot), F32)
    zeros_h = jnp.zeros((n, half), F32)
    zeros_r = jnp.zeros((n, head_dim - rot), F32)
    c = jnp.concatenate([cos, cos, ones], axis=1)
    s_lo = jnp.concatenate([-sin, zeros_h, zeros_r], axis=1)
    s_hi = jnp.concatenate([zeros_h, sin, zeros_r], axis=1)
    reps = LANES // head_dim
    tabs = [jnp.tile(t, (1, reps)) for t in (c, s_lo, s_hi)]
    if valid_lanes < LANES:
        lane = jnp.arange(LANES)[None, :]
        tabs = [jnp.where(lane < valid_lanes, tabs[0], 1.0), jnp.where(lane < valid_lanes, tabs[1], 0.0),
                jnp.where(lane < valid_lanes, tabs[2], 0.0)]
    return jnp.stack(tabs, axis=0), half


def _lambda(lq_ref, lk_ref, lam_init):
    lq = lq_ref[...].astype(F32)
    lk = lk_ref[...].astype(F32)
    prod = lq * lk
    return (jnp.exp(jnp.sum(prod[0:1, :], axis=1, keepdims=True))
            - jnp.exp(jnp.sum(prod[1:2, :], axis=1, keepdims=True)) + lam_init)


def _diffattn_kernel(lq_ref, lk_ref, g_ref, q_ref, k_ref, vt_ref, o_ref, m_ref, l_ref, acc_ref,
                     *, tq, dqk, lam_init):
    i = pl.program_id(2)
    c_exp = dqk ** -0.5 * LOG2_E
    q = q_ref[...].astype(F32)
    lane = lax.broadcasted_iota(jnp.int32, q.shape, 1)
    qs = jnp.concatenate([jnp.where(lane < dqk, q, 0.0), jnp.where(lane >= dqk, q, 0.0)], axis=0)
    qs = qs.astype(q_ref.dtype)
    m_ref[...] = jnp.full(m_ref.shape, -jnp.inf, F32)
    l_ref[...] = jnp.zeros(l_ref.shape, F32)
    acc_ref[...] = jnp.zeros(acc_ref.shape, F32)

    def step(j, masked):
        start = pl.multiple_of(j * tq, tq)
        kt = k_ref[pl.ds(start, tq), :]
        vt = vt_ref[:, pl.ds(start, tq)]
        s = _dot_nt(kt, qs) * c_exp
        if masked:
            key = lax.broadcasted_iota(jnp.int32, s.shape, 0)
            qry = lax.broadcasted_iota(jnp.int32, s.shape, 1) & (tq - 1)
            s = jnp.where(key <= qry, s, -jnp.inf)
        m_old = m_ref[...]
        m_new = jnp.maximum(m_old, jnp.max(s, axis=0, keepdims=True))
        alpha = jnp.exp2(m_old - m_new)
        p = jnp.exp2(s - m_new)
        l_ref[...] = alpha * l_ref[...] + jnp.sum(p, axis=0, keepdims=True)
        acc_ref[...] = alpha * acc_ref[...] + _dot(vt, p.astype(vt.dtype))
        m_ref[...] = m_new

    def body(j, carry):
        step(j, False)
        return carry

    lax.fori_loop(0, i, body, 0)
    step(i, True)

    lam = _lambda(lq_ref, lk_ref, lam_init)
    o_all = acc_ref[...] / l_ref[...]
    o = o_all[:, :tq] - lam * o_all[:, tq:]
    inv = lax.rsqrt(jnp.mean(o * o, axis=0, keepdims=True) + RMS_EPS)
    y = o * inv * g_ref[...] * (1.0 - lam_init)
    o_ref[...] = y.T.astype(o_ref.dtype)


def diff_attention_prompt(q, k, v_t, lq, lk, g, *, batch, seq, n_heads, dqk, lam_init):
    m, width = q.shape
    hd = width // n_heads
    tq = _pick(seq, 512, LANES)
    assert tq & (tq - 1) == 0
    nq = seq // tq
    return pl.pallas_call(
        functools.partial(_diffattn_kernel, tq=tq, dqk=dqk, lam_init=lam_init),
        grid=(batch, n_heads, nq),
        in_specs=[
            pl.BlockSpec(lq.shape, lambda b, h, i: (0, 0)),
            pl.BlockSpec(lk.shape, lambda b, h, i: (0, 0)),
            pl.BlockSpec((hd, 1), lambda b, h, i: (0, 0)),
            pl.BlockSpec((tq, hd), lambda b, h, i: (b * nq + i, h)),
            pl.BlockSpec((seq, hd), lambda b, h, i: (b, h)),
            pl.BlockSpec((hd, seq), lambda b, h, i: (h, b)),
        ],
        out_specs=pl.BlockSpec((tq, hd), lambda b, h, i: (b * nq + i, h)),
        out_shape=jax.ShapeDtypeStruct((m, width), BF16),
        scratch_shapes=[pltpu.VMEM((1, 2 * tq), F32), pltpu.VMEM((1, 2 * tq), F32), pltpu.VMEM((hd, 2 * tq), F32)],
        compiler_params=_params("parallel", "parallel", "arbitrary"),
        name="diff_attn_prompt",
    )(lq, lk, g.reshape(hd, 1), q, k, v_t)


def _order_key(score):
    score = jnp.where(score == 0.0, 0.0, score)
    bits = lax.bitcast_convert_type(score, jnp.int32)
    return jnp.where(bits < 0, bits ^ jnp.int32(0x7FFFFFFF), bits)


def _count_rows(mask):
    return jnp.sum(mask.astype(jnp.int32), axis=1, keepdims=True)


def _kth_largest_key(load_keys, shape, k, count_fn):
    t0 = jnp.full(shape, INT_MIN, jnp.int32)
    zero = jnp.zeros(shape, jnp.int32)
    t0 = jnp.where(count_fn(load_keys() >= zero) >= k, zero, t0)

    def body(it, t):
        cand = t | (jnp.int32(1) << (30 - it))
        return jnp.where(count_fn(load_keys() >= cand) >= k, cand, t)

    return lax.fori_loop(0, 31, body, t0)


def _tie_cut(load_eq_pos, shape, need, n_bits, count_fn):
    def body(it, x):
        cand = x | (jnp.int32(1) << (n_bits - 1 - it))
        eq, pos = load_eq_pos()
        return jnp.where(count_fn(eq & (pos < cand)) < need, cand, x)

    return lax.fori_loop(0, n_bits, body, jnp.zeros(shape, jnp.int32))


def _dsa_prompt_kernel(qi_ref, kiwi_k_ref, kiwi_q_ref, qb_ref, kb_ref, vb_ref, o_ref,
                       kdup_ref, wb_ref, score_ref, key_ref, bias_ref, cut_ref,
                       *, tq, seq, extents, n_idx, d_idx, n_heads, n_kv, dh, topk):
    i = pl.program_id(1)

    @pl.when(i == 0)
    def _():
        kf = kiwi_k_ref[...]
        lane = lax.broadcasted_iota(jnp.int32, kf.shape, 1)
        klo = jnp.where(lane < d_idx, kf, 0.0)
        kdup_ref[...] = (klo + pltpu.roll(klo, d_idx, 1)).astype(kdup_ref.dtype)

    wscale = n_idx ** -0.5 * d_idx ** -0.5
    c_exp = dh ** -0.5 * LOG2_E
    rep_shift = (n_heads // n_kv).bit_length() - 1

    w_all = kiwi_q_ref[...] * wscale
    for h in range(n_idx):
        wb_ref[h] = jnp.broadcast_to(w_all[:, d_idx + h:d_idx + h + 1], (tq, LANES))

    def window(j, width):
        return pl.ds(pl.multiple_of(j * width, width), width)

    def run(ncol):
        score_ref[:, :ncol] = jnp.zeros((tq, ncol), F32)

        def head_pair(j, carry):
            grp = qi_ref[:, window(j, LANES)].astype(F32)
            lane = lax.broadcasted_iota(jnp.int32, grp.shape, 1)
            for half in range(2):
                keep = (lane < d_idx) if half == 0 else (lane >= d_idx)
                qh = jnp.where(keep, grp, 0.0).astype(qi_ref.dtype)
                w = wb_ref[2 * j + half][:, 0:1]
                score_ref[:, :ncol] += jnp.maximum(_dot_nt(qh, kdup_ref[:ncol, :]), 0.0) * w
            return carry

        lax.fori_loop(0, n_idx // 2, head_pair, 0)

        row = i * tq + lax.broadcasted_iota(jnp.int32, (tq, ncol), 0)
        col = lax.broadcasted_iota(jnp.int32, (tq, ncol), 1)
        causal = col <= row
        key_ref[:, :ncol] = jnp.where(causal, _order_key(score_ref[:, :ncol]), INT_MIN)

        t = _kth_largest_key(lambda: key_ref[:, :ncol], (tq, 1), topk, _count_rows)
        key = key_ref[:, :ncol]
        eq = key == t
        need = topk - _count_rows(key > t)
        tie = (_count_rows(eq) > need) & (t > INT_MIN)
        cut_ref[...] = jnp.full((tq, 1), ncol, jnp.int32)

        @pl.when(jnp.max(tie.astype(jnp.int32)) > 0)
        def _():
            def load():
                return key_ref[:, :ncol] == t, lax.broadcasted_iota(jnp.int32, (tq, ncol), 1)
            cut_ref[...] = _tie_cut(load, (tq, 1), need, int(math.log2(seq)), _count_rows)

        sel = ((key > t) | (eq & (col <= cut_ref[...]))) & causal
        bias_ref[:, :ncol] = jnp.where(sel, 0.0, -jnp.inf)

        def head(h, carry):
            g = h >> rep_shift
            s = _dot_nt(qb_ref[:, window(h, dh)], kb_ref[:ncol, window(g, dh)]) * c_exp + bias_ref[:, :ncol]
            p = jnp.exp2(s - jnp.max(s, axis=1, keepdims=True))
            l = jnp.sum(p, axis=1, keepdims=True)
            o = _dot(p.astype(vb_ref.dtype), vb_ref[:ncol, window(g, dh)]) / l
            o_ref[:, window(h, dh)] = o.astype(o_ref.dtype)
            return carry

        lax.fori_loop(0, n_heads, head, 0)

    lo = 0
    for ncol in extents:
        @pl.when(((i + 1) * tq > lo) & ((i + 1) * tq <= ncol))
        def _(ncol=ncol):
            run(ncol)
        lo = ncol


def dsa_prompt(qi, kiwi, qb, kb, vb, *, batch, seq, n_idx, d_idx, n_heads, n_kv, dh, topk):
    m = qi.shape[0]
    rep = n_heads // n_kv
    assert 2 * d_idx == LANES and dh == LANES and seq & (seq - 1) == 0 and rep & (rep - 1) == 0 and n_idx % 2 == 0
    tq = _pick(seq, 256, LANES)
    nq = seq // tq
    n_ext = min(4, nq)
    extents = tuple(seq * (c + 1) // n_ext for c in range(n_ext))
    return pl.pallas_call(
        functools.partial(_dsa_prompt_kernel, tq=tq, seq=seq, extents=extents, n_idx=n_idx, d_idx=d_idx,
                          n_heads=n_heads,
                          n_kv=n_kv, dh=dh, topk=topk),
        grid=(batch, nq),
        in_specs=[
            pl.BlockSpec((tq, qi.shape[1]), lambda b, i: (b * nq + i, 0)),
            pl.BlockSpec((seq, LANES), lambda b, i: (b, 0)),
            pl.BlockSpec((tq, LANES), lambda b, i: (b * nq + i, 0)),
            pl.BlockSpec((tq, qb.shape[1]), lambda b, i: (b * nq + i, 0)),
            pl.BlockSpec((seq, kb.shape[1]), lambda b, i: (b, 0)),
            pl.BlockSpec((seq, vb.shape[1]), lambda b, i: (b, 0)),
        ],
        out_specs=pl.BlockSpec((tq, qb.shape[1]), lambda b, i: (b * nq + i, 0)),
        out_shape=jax.ShapeDtypeStruct((m, qb.shape[1]), BF16),
        scratch_shapes=[pltpu.VMEM((seq, LANES), BF16), pltpu.VMEM((n_idx, tq, LANES), F32),
                        pltpu.VMEM((tq, seq), F32), pltpu.VMEM((tq, seq), jnp.int32),
                        pltpu.VMEM((tq, seq), F32), pltpu.VMEM((tq, 1), jnp.int32)],
        compiler_params=_params("parallel", "arbitrary"),
        name="dsa_prompt",
    )(qi, kiwi, kiwi, qb, kb, vb)


def _merge_kernel(oa_ref, ob_ref, wa_ref, wb_ref, ga_ref, gb_ref, o_ref):
    ya = _dot(oa_ref[...], wa_ref[...])
    yb = _dot(ob_ref[...], wb_ref[...])
    u = jax.nn.sigmoid(ga_ref[...]) * ya + jax.nn.sigmoid(gb_ref[...]) * yb
    o_ref[...] = u.astype(o_ref.dtype)


def merge_branches(oa, ob, w_pa, w_pb, ga, gb):
    m, ka = oa.shape
    kb = ob.shape[1]
    n = w_pa.shape[1]
    tm = _pick(m, 1024, 8)
    tn = _pick(n, 512, LANES)
    return pl.pallas_call(
        _merge_kernel,
        grid=(m // tm, n // tn),
        in_specs=[
            pl.BlockSpec((tm, ka), lambda i, j: (i, 0)),
            pl.BlockSpec((tm, kb), lambda i, j: (i, 0)),
            pl.BlockSpec((ka, tn), lambda i, j: (0, j)),
            pl.BlockSpec((kb, tn), lambda i, j: (0, j)),
            pl.BlockSpec((tm, tn), lambda i, j: (i, j)),
            pl.BlockSpec((tm, tn), lambda i, j: (i, j)),
        ],
        out_specs=pl.BlockSpec((tm, tn), lambda i, j: (i, j)),
        out_shape=jax.ShapeDtypeStruct((m, n), oa.dtype),
        compiler_params=_params("parallel", "arbitrary"),
        name="merge_branches",
    )(oa, ob, w_pa, w_pb, ga, gb)


def _matmul_res_kernel(a_ref, w_ref, r_ref, o_ref):
    o_ref[...] = r_ref[...] + _dot(a_ref[...], w_ref[...])


def matmul_residual(a, w, res):
    m, k = a.shape
    n = w.shape[1]
    tm = _pick(m, 1024, 8)
    tn = _pick(n, 512, LANES)
    return pl.pallas_call(
        _matmul_res_kernel,
        grid=(m // tm, n // tn),
        in_specs=[
            pl.BlockSpec((tm, k), lambda i, j: (i, 0)),
            pl.BlockSpec((k, tn), lambda i, j: (0, j)),
            pl.BlockSpec((tm, tn), lambda i, j: (i, j)),
        ],
        out_specs=pl.BlockSpec((tm, tn), lambda i, j: (i, j)),
        out_shape=jax.ShapeDtypeStruct((m, n), F32),
        compiler_params=_params("parallel", "arbitrary"),
        name="matmul_residual",
    )(a, w, res)


def _gateup_kernel(x_ref, wg_ref, wu_ref, o_ref, *w_cast):
    x = x_ref[...]
    if w_cast:
        wg_c, wu_c = w_cast

        @pl.when(pl.program_id(1) == 0)
        def _():
            wg_c[...] = wg_ref[...].astype(wg_c.dtype)
            wu_c[...] = wu_ref[...].astype(wu_c.dtype)

        wg_ref, wu_ref = wg_c, wu_c
    gate = _dot(x, wg_ref[...])
    up = _dot(x, wu_ref[...])
    o_ref[...] = (jax.nn.silu(gate) * up).astype(o_ref.dtype)


def gate_up(xn, wg, wu):
    m, k = xn.shape
    n = wg.shape[1]
    tm = _pick(m, 1024, 8)
    tn = _pick(n, 512, LANES)
    scratch = [] if wg.dtype == xn.dtype else [pltpu.VMEM((k, tn), xn.dtype), pltpu.VMEM((k, tn), xn.dtype)]
    return pl.pallas_call(
        _gateup_kernel,
        grid=(n // tn, m // tm),
        in_specs=[
            pl.BlockSpec((tm, k), lambda j, i: (i, 0)),
            pl.BlockSpec((k, tn), lambda j, i: (0, j)),
            pl.BlockSpec((k, tn), lambda j, i: (0, j)),
        ],
        out_specs=pl.BlockSpec((tm, tn), lambda j, i: (i, j)),
        out_shape=jax.ShapeDtypeStruct((m, n), xn.dtype),
        scratch_shapes=scratch,
        compiler_params=_params("parallel", "arbitrary"),
        name="ffn_gate_up",
    )(xn, wg, wu)


def _router_kernel(x_ref, g_ref, wr_ref, base_ref, hn_ref, gate_ref, route_ref, cnt_ref, run_ref, *, n_exp):
    @pl.when(pl.program_id(0) == 0)
    def _():
        run_ref[...] = base_ref[...]

    hn = _rms(x_ref[...], g_ref[...])
    hn_ref[...] = hn
    tm = hn.shape[0]
    lane = lax.broadcasted_iota(jnp.int32, (tm, LANES), 1)
    logits = jnp.full((tm, LANES), -jnp.inf, F32)
    for e in range(n_exp):
        le = jnp.sum(hn * wr_ref[e:e + 1, :], axis=1, keepdims=True)
        logits = jnp.where(lane == e, le, logits)
    v1 = jnp.max(logits, axis=1, keepdims=True)
    i1 = jnp.min(jnp.where(logits == v1, lane, LANES), axis=1, keepdims=True)
    rest = jnp.where(lane == i1, -jnp.inf, logits)
    v2 = jnp.max(rest, axis=1, keepdims=True)
    i2 = jnp.min(jnp.where(rest == v2, lane, LANES), axis=1, keepdims=True)
    e2 = jnp.exp(v2 - v1)
    den = 1.0 + e2
    gate_ref[...] = jnp.where(lane == 0, 1.0 / den, jnp.where(lane == 1, e2 / den, 0.0))

    onehot = jnp.where((lane == i1) | (lane == i2), 1.0, 0.0)
    r = lax.broadcasted_iota(jnp.int32, (tm, tm), 0)
    c = lax.broadcasted_iota(jnp.int32, (tm, tm), 1)
    before = _dot(jnp.where(c < r, 1.0, 0.0).astype(BF16), onehot.astype(BF16)) + run_ref[...]
    r1 = jnp.sum(jnp.where(lane == i1, before, 0.0), axis=1, keepdims=True).astype(jnp.int32)
    r2 = jnp.sum(jnp.where(lane == i2, before, 0.0), axis=1, keepdims=True).astype(jnp.int32)
    route_ref[...] = jnp.where(lane == 0, i1, jnp.where(lane == 1, i2, jnp.where(lane == 2, r1,
                               jnp.where(lane == 3, r2, 0))))
    run_ref[...] = run_ref[...] + jnp.sum(onehot, axis=0, keepdims=True)
    cnt_ref[...] = run_ref[...]


def moe_router(x, g, w_router, base_counts):
    m, d = x.shape
    n_exp = w_router.shape[1]
    assert TOP_K_EXPERTS == 2 and n_exp <= LANES
    tm = _pick(m, 256, 8)
    row = lambda i: (i, 0)
    fixed = lambda i: (0, 0)
    return pl.pallas_call(
        functools.partial(_router_kernel, n_exp=n_exp),
        grid=(m // tm,),
        in_specs=[
            pl.BlockSpec((tm, d), row),
            pl.BlockSpec((1, d), fixed),
            pl.BlockSpec((n_exp, d), fixed),
            pl.BlockSpec((1, LANES), fixed),
        ],
        out_specs=[pl.BlockSpec((tm, d), row), pl.BlockSpec((tm, LANES), row), pl.BlockSpec((tm, LANES), row),
                   pl.BlockSpec((1, LANES), fixed)],
        out_shape=[jax.ShapeDtypeStruct((m, d), F32), jax.ShapeDtypeStruct((m, LANES), F32),
                   jax.ShapeDtypeStruct((m, LANES), jnp.int32), jax.ShapeDtypeStruct((1, LANES), F32)],
        scratch_shapes=[pltpu.VMEM((1, LANES), F32)],
        compiler_params=_params("arbitrary"),
        name="moe_router",
    )(x, g.reshape(1, d), w_router.T, base_counts)


def _row_copy(src_ref, src_row, dst_ref, dst_row, sem):
    return pltpu.make_async_copy(src_ref.at[pl.ds(src_row, 1)], dst_ref.at[pl.ds(dst_row, 1)], sem)


def _dispatch_kernel(pos1_ref, pos2_ref, x_ref, xs_in_ref, xs_ref, sem, *, tm):
    del xs_in_ref
    base = pl.program_id(0) * tm

    def start(r, carry):
        _row_copy(x_ref, r, xs_ref, pos1_ref[base + r], sem).start()
        _row_copy(x_ref, r, xs_ref, pos2_ref[base + r], sem).start()
        return carry

    def wait(r, carry):
        _row_copy(x_ref, 0, xs_ref, 0, sem).wait()
        _row_copy(x_ref, 0, xs_ref, 0, sem).wait()
        return carry

    lax.fori_loop(0, tm, start, 0)
    lax.fori_loop(0, tm, wait, 0)


def moe_dispatch(hn, pos1, pos2, x_sorted):
    m, d = hn.shape
    tm = _pick(m, 256, 8)
    grid_spec = pltpu.PrefetchScalarGridSpec(
        num_scalar_prefetch=2,
        grid=(m // tm,),
        in_specs=[pl.BlockSpec((tm, d), lambda i, p1, p2: (i, 0)), pl.BlockSpec(memory_space=pl.ANY)],
        out_specs=pl.BlockSpec(memory_space=pl.ANY),
        scratch_shapes=[pltpu.SemaphoreType.DMA(())],
    )
    return pl.pallas_call(
        functools.partial(_dispatch_kernel, tm=tm),
        grid_spec=grid_spec,
        out_shape=jax.ShapeDtypeStruct(x_sorted.shape, x_sorted.dtype),
        input_output_aliases={3: 0},
        compiler_params=_params("arbitrary"),
        name="moe_dispatch",
    )(pos1, pos2, hn, x_sorted)


def _combine_kernel(pos1_ref, pos2_ref, x_ref, gate_ref, ys_ref, o_ref, buf_ref, sem, *, tm):
    base = pl.program_id(0) * tm

    def start(r, carry):
        _row_copy(ys_ref, pos1_ref[base + r], buf_ref.at[0], r, sem).start()
        _row_copy(ys_ref, pos2_ref[base + r], buf_ref.at[1], r, sem).start()
        return carry

    def wait(r, carry):
        _row_copy(ys_ref, 0, buf_ref.at[0], 0, sem).wait()
        _row_copy(ys_ref, 0, buf_ref.at[1], 0, sem).wait()
        return carry

    lax.fori_loop(0, tm, start, 0)
    lax.fori_loop(0, tm, wait, 0)
    gate = gate_ref[...]
    o_ref[...] = x_ref[...] + gate[:, 0:1] * buf_ref[0] + gate[:, 1:2] * buf_ref[1]


def moe_combine(x, gates, pos1, pos2, y_sorted):
    m, d = x.shape
    tm = _pick(m, 256, 8)
    row = lambda i, p1, p2: (i, 0)
    grid_spec = pltpu.PrefetchScalarGridSpec(
        num_scalar_prefetch=2,
        grid=(m // tm,),
        in_specs=[pl.BlockSpec((tm, d), row), pl.BlockSpec((tm, LANES), row), pl.BlockSpec(memory_space=pl.ANY)],
        out_specs=pl.BlockSpec((tm, d), row),
        scratch_shapes=[pltpu.VMEM((2, tm, d), F32), pltpu.SemaphoreType.DMA(())],
    )
    return pl.pallas_call(
        functools.partial(_combine_kernel, tm=tm),
        grid_spec=grid_spec,
        out_shape=jax.ShapeDtypeStruct((m, d), F32),
        compiler_params=_params("arbitrary"),
        name="moe_combine",
    )(pos1, pos2, x, gates, y_sorted)


def _expert_gateup_kernel(te_ref, nu_ref, x_ref, wg_ref, wu_ref, o_ref, wg_c, wu_c):
    i = pl.program_id(1)
    used = i < nu_ref[0]
    fresh = (i == 0) | (te_ref[i] != te_ref[jnp.maximum(i - 1, 0)])

    @pl.when(used & fresh)
    def _():
        wg_c[...] = wg_ref[0].astype(wg_c.dtype)
        wu_c[...] = wu_ref[0].astype(wu_c.dtype)

    @pl.when(used)
    def _():
        x = x_ref[...]
        gate = _dot(x, wg_c[...])
        up = _dot(x, wu_c[...])
        o_ref[...] = (jax.nn.silu(gate) * up).astype(o_ref.dtype)

    @pl.when(jnp.logical_not(used))
    def _():
        o_ref[...] = jnp.zeros(o_ref.shape, o_ref.dtype)


def expert_gate_up(x_sorted, wg, wu, tile_expert, n_used, tm):
    p_rows, d = x_sorted.shape
    ff = wg.shape[2]
    tn = _pick(ff, 1024, LANES)

    def w_map(j, i, te, nu):
        return (te[i], 0, j)

    grid_spec = pltpu.PrefetchScalarGridSpec(
        num_scalar_prefetch=2,
        grid=(ff // tn, p_rows // tm),
        in_specs=[
            pl.BlockSpec((tm, d), lambda j, i, te, nu: (jnp.minimum(i, nu[0] - 1), 0)),
            pl.BlockSpec((1, d, tn), w_map),
            pl.BlockSpec((1, d, tn), w_map),
        ],
        out_specs=pl.BlockSpec((tm, tn), lambda j, i, te, nu: (i, j)),
        scratch_shapes=[pltpu.VMEM((d, tn), BF16), pltpu.VMEM((d, tn), BF16)],
    )
    return pl.pallas_call(
        _expert_gateup_kernel,
        grid_spec=grid_spec,
        out_shape=jax.ShapeDtypeStruct((p_rows, ff), BF16),
        compiler_params=_params("parallel", "arbitrary"),
        name="expert_gate_up",
    )(tile_expert, n_used, x_sorted, wg, wu)


def _expert_down_kernel(te_ref, nu_ref, h_ref, wd_ref, o_ref):
    @pl.when(pl.program_id(0) < nu_ref[0])
    def _():
        o_ref[...] = _dot(h_ref[...], wd_ref[0])

    @pl.when(pl.program_id(0) >= nu_ref[0])
    def _():
        o_ref[...] = jnp.zeros(o_ref.shape, o_ref.dtype)


def expert_down(h_sorted, wd, tile_expert, n_used, tm):
    p_rows, ff = h_sorted.shape
    d = wd.shape[2]
    tn = _pick(d, 512, LANES)
    nj = d // tn
    grid_spec = pltpu.PrefetchScalarGridSpec(
        num_scalar_prefetch=2,
        grid=(p_rows // tm, nj),
        in_specs=[
            pl.BlockSpec((tm, ff), lambda i, j, te, nu: (jnp.minimum(i, nu[0] - 1), 0)),
            pl.BlockSpec((1, ff, tn), lambda i, j, te, nu: (te[i], 0, jnp.where(i < nu[0], j, nj - 1))),
        ],
        out_specs=pl.BlockSpec((tm, tn), lambda i, j, te, nu: (i, j)),
    )
    return pl.pallas_call(
        _expert_down_kernel,
        grid_spec=grid_spec,
        out_shape=jax.ShapeDtypeStruct((p_rows, d), F32),
        compiler_params=_params("parallel", "arbitrary"),
        name="expert_down",
    )(tile_expert, n_used, h_sorted, wd)


MOE_ROW_TILE = 512
SAMPLE_ATTN_PAGES = 8
SAMPLE_DSA_PAGES = 16


def moe_ffn(xp, xs, g, w_router, wg, wu, wd):
    n_exp = w_router.shape[1]
    d = xp.shape[1]
    tm = MOE_ROW_TILE
    zeros = jnp.zeros((1, LANES), F32)
    hn_p, gate_p, route_p, cnt_p = moe_router(xp, g, w_router, zeros)
    hn_s, gate_s, route_s, cnt = moe_router(xs, g, w_router, cnt_p)

    counts = cnt[0, :n_exp].astype(jnp.int32)
    sizes = (counts + tm - 1) // tm * tm
    ends = jnp.cumsum(sizes)
    starts = ends - sizes
    n_assign = TOP_K_EXPERTS * (xp.shape[0] + xs.shape[0])
    n_tiles = (n_assign + n_exp * (tm - 1) + tm - 1) // tm
    tile_expert = jnp.minimum(jnp.sum(jnp.arange(n_tiles)[:, None] * tm >= ends[None, :], axis=1), n_exp - 1)
    tile_expert = tile_expert.astype(jnp.int32)
    n_used = (ends[-1:] // tm).astype(jnp.int32)

    def slots(route):
        return (jnp.take(starts, route[:, 0]) + route[:, 2], jnp.take(starts, route[:, 1]) + route[:, 3])

    p1_p, p2_p = slots(route_p)
    p1_s, p2_s = slots(route_s)
    x_sorted = jnp.zeros((n_tiles * tm, d), F32)
    x_sorted = moe_dispatch(hn_p, p1_p, p2_p, x_sorted)
    x_sorted = moe_dispatch(hn_s, p1_s, p2_s, x_sorted)
    h_sorted = expert_gate_up(x_sorted.astype(BF16), wg, wu, tile_expert, n_used, tm)
    y_sorted = expert_down(h_sorted, wd.astype(BF16), tile_expert, n_used, tm)
    return (moe_combine(xp, gate_p, p1_p, p2_p, y_sorted), moe_combine(xs, gate_s, p1_s, p2_s, y_sorted))


def _sample_attn_kernel(pt_ref, lq_ref, lk_ref, g_ref, qa_ref, qi_ref, wi_ref, *rest,
                        n_steps, group, n_heads, dqk, lam_init, n_idx, d_idx):
    kp_refs, vp_refs, ip_refs = rest[:group], rest[group:2 * group], rest[2 * group:3 * group]
    kn_ref, vn_ref, in_ref, o_ref, sc_ref, scn_ref, m_ref, l_ref, acc_ref = rest[3 * group:]
    p = pl.program_id(1)
    hd = 2 * dqk
    page = kp_refs[0].shape[2]
    cols = page * n_heads
    scale = dqk ** -0.5

    q = qa_ref[0].astype(F32)
    lane = lax.broadcasted_iota(jnp.int32, q.shape, 1)
    qs = jnp.concatenate([jnp.where(lane < dqk, q, 0.0), jnp.where(lane >= dqk, q, 0.0)], axis=0)

    @pl.when(p == 0)
    def _():
        m_ref[...] = jnp.full(m_ref.shape, -jnp.inf, F32)
        l_ref[...] = jnp.zeros(l_ref.shape, F32)
        acc_ref[...] = jnp.zeros(acc_ref.shape, F32)

    k2 = jnp.concatenate([r[0, 0].reshape(cols, hd).astype(BF16) for r in kp_refs], axis=0)
    v2 = jnp.concatenate([r[0, 0].reshape(cols, hd).astype(BF16) for r in vp_refs], axis=0)
    s = _dot_nt(qs.astype(BF16), k2) * scale
    r_id = lax.broadcasted_iota(jnp.int32, s.shape, 0) & (n_heads - 1)
    c_id = lax.broadcasted_iota(jnp.int32, s.shape, 1) & (n_heads - 1)
    s = jnp.where(r_id == c_id, s, -jnp.inf)
    m_old = m_ref[...]
    m_new = jnp.maximum(m_old, jnp.max(s, axis=1, keepdims=True))
    alpha = jnp.exp(m_old - m_new)
    pr = jnp.exp(s - m_new)
    l_ref[...] = alpha * l_ref[...] + jnp.sum(pr, axis=1, keepdims=True)
    acc_ref[...] = alpha * acc_ref[...] + _dot(pr.astype(BF16), v2)
    m_ref[...] = m_new

    wscale = n_idx ** -0.5 * d_idx ** -0.5
    qi = qi_ref[0]
    w = wi_ref[0] * wscale
    for k, ip_ref in enumerate(ip_refs):
        si = jnp.maximum(_dot(qi, ip_ref[0, 0]), 0.0) * w
        sc_ref[0, k:k + 1, :] = jnp.sum(si, axis=0, keepdims=True)

    @pl.when(p == n_steps - 1)
    def _():
        kn = jnp.concatenate([kn_ref[0], kn_ref[0]], axis=0)
        vn = jnp.concatenate([vn_ref[0], vn_ref[0]], axis=0)
        s_new = jnp.sum(qs * kn, axis=1, keepdims=True) * scale
        m_o = m_ref[...]
        m_n = jnp.maximum(m_o, s_new)
        a = jnp.exp(m_o - m_n)
        p_new = jnp.exp(s_new - m_n)
        l_fin = a * l_ref[...] + p_new
        o_all = (a * acc_ref[...] + p_new * vn) / l_fin
        lam = _lambda(lq_ref, lk_ref, lam_init)
        o = o_all[:n_heads] - lam * o_all[n_heads:]
        o_ref[0] = (_rms(o, g_ref[...]) * (1.0 - lam_init)).astype(o_ref.dtype)
        kin = in_ref[0]
        s_in = jnp.maximum(jnp.sum(qi.astype(F32) * kin, axis=1, keepdims=True), 0.0) * w
        scn_ref[0] = jnp.broadcast_to(jnp.sum(s_in, axis=0, keepdims=True), (1, LANES))


def sample_attention(page_table, layer, cache_k, cache_v, cache_i_t, qa, qi, wi, k_new, v_new, i_new, lq, lk, g,
                     *, n_heads, dqk, lam_init, n_idx, d_idx):
    db, n_pages = page_table.shape
    width = qa.shape[1]
    page = cache_k.shape[2]
    hd = 2 * dqk
    rows = 2 * n_heads
    assert n_heads & (n_heads - 1) == 0
    pt = page_table.reshape(-1)
    group = SAMPLE_ATTN_PAGES if n_pages % SAMPLE_ATTN_PAGES == 0 else 1
    n_steps = n_pages // group

    def pool5(k):
        return lambda b, p, pt_ref: (layer, pt_ref[b * n_pages + p * group + k], 0, 0, 0)

    def pool4(k):
        return lambda b, p, pt_ref: (layer, pt_ref[b * n_pages + p * group + k], 0, 0)

    row3 = lambda b, p, pt_ref: (b, 0, 0)
    const2 = lambda b, p, pt_ref: (0, 0)
    grid_spec = pltpu.PrefetchScalarGridSpec(
        num_scalar_prefetch=1,
        grid=(db, n_steps),
        in_specs=[
            pl.BlockSpec(lq.shape, const2),
            pl.BlockSpec(lk.shape, const2),
            pl.BlockSpec((1, hd), const2),
            pl.BlockSpec((1, n_heads, hd), row3),
            pl.BlockSpec((1, n_idx, d_idx), row3),
            pl.BlockSpec((1, n_idx, 1), row3),
            *[pl.BlockSpec((1, 1, page, n_heads, hd), pool5(k)) for k in range(group)],
            *[pl.BlockSpec((1, 1, page, n_heads, hd), pool5(k)) for k in range(group)],
            *[pl.BlockSpec((1, 1, d_idx, page), pool4(k)) for k in range(group)],
            pl.BlockSpec((1, n_heads, hd), row3),
            pl.BlockSpec((1, n_heads, hd), row3),
            pl.BlockSpec((1, 1, d_idx), row3),
        ],
        out_specs=[
            pl.BlockSpec((1, n_heads, hd), row3),
            pl.BlockSpec((1, group, page), lambda b, p, pt_ref: (b * n_steps + p, 0, 0)),
            pl.BlockSpec((1, 1, LANES), row3),
        ],
        scratch_shapes=[pltpu.VMEM((rows, 1), F32), pltpu.VMEM((rows, 1), F32), pltpu.VMEM((rows, hd), F32)],
    )
    out, scores, score_new = pl.pallas_call(
        functools.partial(_sample_attn_kernel, n_steps=n_steps, group=group, n_heads=n_heads, dqk=dqk,
                          lam_init=lam_init, n_idx=n_idx, d_idx=d_idx),
        grid_spec=grid_spec,
        out_shape=[jax.ShapeDtypeStruct((db, n_heads, hd), F32),
                   jax.ShapeDtypeStruct((db * n_steps, group, page), F32),
                   jax.ShapeDtypeStruct((db, 1, LANES), F32)],
        compiler_params=_params("parallel", "arbitrary"),
        name="sample_attention",
    )(pt, lq, lk, g.reshape(1, hd), qa.reshape(db, n_heads, hd), qi.reshape(db, n_idx, d_idx),
      wi.reshape(db, n_idx, 1), *([cache_k] * group), *([cache_v] * group), *([cache_i_t] * group),
      k_new.reshape(db, n_heads, hd), v_new.reshape(db, n_heads, hd), i_new.reshape(db, 1, d_idx))
    return out.reshape(db, width), scores.reshape(db, n_pages, page), score_new


def _count_all(mask):
    c = jnp.sum(mask.astype(jnp.int32), axis=1, keepdims=True)
    return jnp.sum(c, axis=0, keepdims=True)


def _sample_dsa_kernel(pt_ref, sc_ref, scn_ref, qb_ref, *rest, n_pages, group, page, n_heads, n_kv, dh, topk):
    kp_refs, vp_refs = rest[:group], rest[group:2 * group]
    kn_ref, vn_ref, o_ref, key_ref, keyn_ref, t_ref, cut_ref, selx_ref, m_ref, l_ref, acc_ref = rest[2 * group:]
    p = pl.program_id(1)
    rep = n_heads // n_kv
    scale = dh ** -0.5
    past = n_pages * page
    cols = page * n_kv
    kv_shift = n_kv.bit_length() - 1
    rep_shift = rep.bit_length() - 1

    @pl.when(p == 0)
    def _():
        m_ref[...] = jnp.full(m_ref.shape, -jnp.inf, F32)
        l_ref[...] = jnp.zeros(l_ref.shape, F32)
        acc_ref[...] = jnp.zeros(acc_ref.shape, F32)
        key_ref[...] = _order_key(sc_ref[0])
        keyn_ref[...] = _order_key(scn_ref[0])
        key_new = keyn_ref[:, 0:1]

        def cnt(t):
            return _count_all(key_ref[...] >= t) + (key_new >= t).astype(jnp.int32)

        t0 = jnp.full((1, 1), INT_MIN, jnp.int32)
        zero = jnp.zeros((1, 1), jnp.int32)
        t0 = jnp.where(cnt(zero) >= topk, zero, t0)

        def body(it, t):
            cand = t | (jnp.int32(1) << (30 - it))
            return jnp.where(cnt(cand) >= topk, cand, t)

        t = lax.fori_loop(0, 31, body, t0)
        t_ref[...] = t
        key = key_ref[...]
        n_gt = _count_all(key > t) + (key_new > t).astype(jnp.int32)
        need = topk - n_gt
        pos = (lax.broadcasted_iota(jnp.int32, key.shape, 0) * page
               + lax.broadcasted_iota(jnp.int32, key.shape, 1))
        n_bits = int(math.ceil(math.log2(past + 1)))

        def body2(it, x):
            cand = x | (jnp.int32(1) << (n_bits - 1 - it))
            c = _count_all((key_ref[...] == t) & (pos < cand)) + ((key_new == t) & (past < cand)).astype(jnp.int32)
            return jnp.where(c < need, cand, x)

        cut = lax.fori_loop(0, n_bits, body2, jnp.zeros((1, 1), jnp.int32))
        cut_ref[...] = cut
        sel = (key > t) | ((key == t) & (pos <= cut))
        e_t = lax.broadcasted_iota(jnp.int32, (page, cols), 0)
        e_c = lax.broadcasted_iota(jnp.int32, (page, cols), 1)
        spread = jnp.where((e_c >> kv_shift) == e_t, 1.0, 0.0).astype(BF16)
        selx_ref[...] = _dot(jnp.where(sel, 1.0, 0.0).astype(BF16), spread)

    t = t_ref[...]
    cut = cut_ref[...]
    q = qb_ref[0]

    k2 = jnp.concatenate([r[0, 0].astype(BF16) for r in kp_refs], axis=0)
    v2 = jnp.concatenate([r[0, 0].astype(BF16) for r in vp_refs], axis=0)
    sel_cols = jnp.concatenate(
        [jnp.broadcast_to(selx_ref[pl.ds(p * group + k, 1), :], (n_heads, cols)) for k in range(group)], axis=1)
    r_id = lax.broadcasted_iota(jnp.int32, (n_heads, group * cols), 0)
    c_id = lax.broadcasted_iota(jnp.int32, (n_heads, group * cols), 1)
    keep = (sel_cols > 0.5) & ((c_id & (n_kv - 1)) == (r_id >> rep_shift))
    s = _dot_nt(q.astype(BF16), k2) * scale
    s = jnp.where(keep, s, -jnp.inf)
    m_old = m_ref[...]
    m_new = jnp.maximum(m_old, jnp.max(s, axis=1, keepdims=True))
    m_safe = jnp.where(m_new == -jnp.inf, 0.0, m_new)
    alpha = jnp.exp(m_old - m_safe)
    pr = jnp.exp(s - m_safe)
    l_ref[...] = alpha * l_ref[...] + jnp.sum(pr, axis=1, keepdims=True)
    acc_ref[...] = alpha * acc_ref[...] + _dot(pr.astype(BF16), v2)
    m_ref[...] = m_new

    @pl.when(p == n_pages // group - 1)
    def _():
        key_new = keyn_ref[:, 0:1]
        sel_new = (key_new > t) | ((key_new == t) & (past <= cut))
        h_id = lax.broadcasted_iota(jnp.int32, (n_heads, dh), 0) >> rep_shift
        kn = jnp.zeros((n_heads, dh), F32)
        vn = jnp.zeros((n_heads, dh), F32)
        for g in range(n_kv):
            kn = jnp.where(h_id == g, kn_ref[0, g:g + 1, :], kn)
            vn = jnp.where(h_id == g, vn_ref[0, g:g + 1, :], vn)
        s_new = jnp.sum(q.astype(F32) * kn, axis=1, keepdims=True) * scale
        s_new = jnp.where(sel_new, s_new, -jnp.inf)
        m_o = m_ref[...]
        m_n = jnp.maximum(m_o, s_new)
        m_s = jnp.where(m_n == -jnp.inf, 0.0, m_n)
        a = jnp.exp(m_o - m_s)
        p_new = jnp.exp(s_new - m_s)
        l_fin = a * l_ref[...] + p_new
        o_ref[0] = ((a * acc_ref[...] + p_new * vn) / l_fin).astype(o_ref.dtype)


def sample_dsa(page_table, layer, cache_k, cache_v, scores, score_new, qb, k_new, v_new,
               *, n_heads, n_kv, dh, topk):
    db, n_pages = page_table.shape
    cols = cache_k.shape[2]
    page = cols // n_kv
    rep = n_heads // n_kv
    assert n_kv & (n_kv - 1) == 0 and rep & (rep - 1) == 0
    pt = page_table.reshape(-1)

    group = SAMPLE_DSA_PAGES if n_pages % SAMPLE_DSA_PAGES == 0 else 1

    def pool_map(k):
        return lambda b, p, pt_ref: (layer, pt_ref[b * n_pages + p * group + k], 0, 0)

    row3 = lambda b, p, pt_ref: (b, 0, 0)
    grid_spec = pltpu.PrefetchScalarGridSpec(
        num_scalar_prefetch=1,
        grid=(db, n_pages // group),
        in_specs=[
            pl.BlockSpec((1, n_pages, page), row3),
            pl.BlockSpec((1, 1, LANES), row3),
            pl.BlockSpec((1, n_heads, dh), row3),
            *[pl.BlockSpec((1, 1, cols, dh), pool_map(k)) for k in range(group)],
            *[pl.BlockSpec((1, 1, cols, dh), pool_map(k)) for k in range(group)],
            pl.BlockSpec((1, n_kv, dh), row3),
            pl.BlockSpec((1, n_kv, dh), row3),
        ],
        out_specs=pl.BlockSpec((1, n_heads, dh), row3),
        scratch_shapes=[pltpu.VMEM((n_pages, page), jnp.int32), pltpu.VMEM((1, LANES), jnp.int32),
                        pltpu.VMEM((1, 1), jnp.int32), pltpu.VMEM((1, 1), jnp.int32),
                        pltpu.VMEM((n_pages, cols), F32),
                        pltpu.VMEM((n_heads, 1), F32), pltpu.VMEM((n_heads, 1), F32),
                        pltpu.VMEM((n_heads, dh), F32)],
    )
    out = pl.pallas_call(
        functools.partial(_sample_dsa_kernel, n_pages=n_pages, group=group, page=page, n_heads=n_heads,
                          n_kv=n_kv, dh=dh, topk=topk),
        grid_spec=grid_spec,
        out_shape=jax.ShapeDtypeStruct((db, n_heads, dh), F32),
        compiler_params=_params("parallel", "arbitrary"),
        name="sample_dsa",
    )(pt, scores, score_new, qb.reshape(db, n_heads, dh), *([cache_k] * group), *([cache_v] * group),
      k_new.reshape(db, n_kv, dh), v_new.reshape(db, n_kv, dh))
    return out.reshape(db, n_heads * dh)


def kernel(x_prompt, x_sample, cache_a_k, cache_a_v, cache_b_k, cache_b_v, cache_idx_k, page_table, w_in, lambda_q, lambda_k, subln_g, w_branch_a, w_branch_b, w_out, norm_mix_g, norm_ffn_g, w_dense_gate, w_dense_up, w_dense_down, w_router, w_exp_gate, w_exp_up, w_exp_down, norm_final_g):
    batch, seq, d_model = x_prompt.shape
    db, dec_seq, _ = x_sample.shape
    assert dec_seq == 1, "one new token per sample row"
    depth, n_pool, page, n_ha, a_width = cache_a_k.shape
    assert page == PAGE_SIZE
    dqk = a_width // 2
    dv = cache_a_v.shape[4]
    n_kv, dh = cache_b_k.shape[3:]
    d_idx = cache_idx_k.shape[3]
    n_hb = w_branch_b.shape[1] // dh
    n_in = w_in.shape[2]
    wa, wva, wqb, wkb = n_ha * 2 * dqk, n_ha * dv, n_hb * dh, n_kv * dh
    n_idx = (n_in - (2 * wa + wva + wqb + 2 * wkb + d_idx + 2 * d_model)) // (d_idx + 1)
    assert dv == 2 * dqk == LANES and dh == LANES and 2 * d_idx == LANES
    n_pages = page_table.shape[1]
    past_len = n_pages * page
    topk_p = min(TOPK_MAX, seq // 4)
    topk_s = min(TOPK_MAX, (past_len + dec_seq) // 4)
    n_exp = w_router.shape[2]

    offs = [0]
    for wdt in (wa, wa, wva, wqb, wkb, wkb, n_idx * d_idx, d_idx, n_idx, d_model, d_model):
        offs.append(offs[-1] + wdt)
    o_qa, o_ka, o_va, o_qb, o_kb, o_vb, o_qi, o_ki, o_wi, o_ga, o_gb, _ = offs

    pos_p = jnp.arange(seq, dtype=F32)
    pos_s = jnp.full((db,), float(past_len), F32)
    tabs = {}
    for name, pos in (("p", pos_p), ("s", pos_s)):
        tabs[name, "a"] = rope_table(pos, dqk)
        tabs[name, "b"] = rope_table(pos, dh)
        tabs[name, "i"] = rope_table(pos, d_idx, valid_lanes=d_idx)

    n_groups = -(-n_in // LANES)
    kind = {}
    for start, stop, name in ((o_qa, o_va, "a"), (o_qb, o_vb, "b"), (o_qi, o_ki, "a"), (o_ki, o_ki + LANES, "i")):
        assert start % LANES == 0 and stop % LANES == 0
        for g in range(start // LANES, stop // LANES):
            kind[g] = name
    identity = jnp.stack([jnp.ones((db, LANES), F32), jnp.zeros((db, LANES), F32), jnp.zeros((db, LANES), F32)])
    tab_s = jnp.concatenate([tabs["s", kind[g]][0] if g in kind else identity for g in range(n_groups)], axis=2)
    halves_s = tuple(tabs["s", kind[g]][1] if g in kind else 0 for g in range(n_groups))

    ck_b = cache_b_k.reshape(depth, n_pool, page * n_kv, dh)
    cv_b = cache_b_v.reshape(depth, n_pool, page * n_kv, dh)
    ci_t = jnp.swapaxes(cache_idx_k, 2, 3)
    w_in_t = jnp.swapaxes(w_in, 1, 2)

    xp = x_prompt.reshape(batch * seq, d_model)
    xs = x_sample.reshape(db * dec_seq, d_model)
    rows = {"p": [[] for _ in range(5)], "s": [[] for _ in range(5)]}

    for l in range(depth):
        lam_init = 0.8 - 0.6 * math.exp(-0.3 * l)
        lq, lk = lambda_q[l], lambda_k[l]
        wl = w_in_t[l].astype(BF16)
        w_kiwi = jnp.pad(wl[o_ki:o_ga], ((0, LANES - d_idx - n_idx), (0, 0)))
        mix_w = {"p": dict(pa=w_branch_a[l].astype(BF16), pb=w_branch_b[l].astype(BF16), o=w_out[l].astype(BF16)),
                 "s": dict(pa=w_branch_a[l], pb=w_branch_b[l], o=w_out[l])}

        def in_proj_prompt(x):
            xn = rmsnorm(x, norm_mix_g[l], BF16)
            ta, ha = tabs["p", "a"]
            tb, hb = tabs["p", "b"]
            ti, hi = tabs["p", "i"]
            z = {}
            z["qa"], = project(xn, wl[o_qa:o_ka], [BF16], ta, ha)
            z["ka32"], z["ka"] = project(xn, wl[o_ka:o_va], [F32, BF16], ta, ha)
            z["va32"], z["va"] = project(xn, wl[o_va:o_qb], [F32, BF16])
            z["qb"], = project(xn, wl[o_qb:o_kb], [BF16], tb, hb)
            z["kb32"], z["kb"] = project(xn, wl[o_kb:o_vb], [F32, BF16], tb, hb)
            z["vb32"], z["vb"] = project(xn, wl[o_vb:o_qi], [F32, BF16])
            z["qi"], = project(xn, wl[o_qi:o_ki], [BF16], ta, ha)
            z["kiwi"], = project(xn, w_kiwi, [F32], ti, hi)
            z["ga"], = project(xn, wl[o_ga:o_gb], [F32])
            z["gb"], = project(xn, wl[o_gb:], [F32])
            return z

        def in_proj_sample(x):
            xn = rmsnorm(x, norm_mix_g[l], F32)
            zz = sample_project(xn, w_in_t, l, tab_s, halves_s)
            z = {name: zz[:, a:b] for name, a, b in (
                ("qa", o_qa, o_ka), ("ka32", o_ka, o_va), ("va32", o_va, o_qb), ("qb", o_qb, o_kb),
                ("kb32", o_kb, o_vb), ("vb32", o_vb, o_qi), ("qi", o_qi, o_ki), ("kiwi", o_ki, o_ki + LANES),
                ("ga", o_ga, o_gb), ("gb", o_gb, n_in))}
            return z

        def mix_out(x, z, oa, ob, grp):
            w = mix_w[grp]
            u = merge_branches(oa, ob, w["pa"], w["pb"], z["ga"], z["gb"])
            return matmul_residual(u, w["o"], x)

        zp = in_proj_prompt(xp)
        oa = diff_attention_prompt(zp["qa"], zp["ka"], zp["va"].T, lq, lk, subln_g[l], batch=batch, seq=seq,
                                   n_heads=n_ha, dqk=dqk, lam_init=lam_init)
        ob = dsa_prompt(zp["qi"], zp["kiwi"], zp["qb"], zp["kb"], zp["vb"], batch=batch, seq=seq, n_idx=n_idx,
                        d_idx=d_idx, n_heads=n_hb, n_kv=n_kv, dh=dh, topk=topk_p)
        xp = mix_out(xp, zp, oa, ob, "p")

        zs = in_proj_sample(xs)
        ki_s = zs["kiwi"][:, :d_idx]
        wi_s = zs["kiwi"][:, d_idx:d_idx + n_idx]
        oa, scores, score_new = sample_attention(
            page_table, l, cache_a_k, cache_a_v, ci_t, zs["qa"], zs["qi"], wi_s, zs["ka32"], zs["va32"], ki_s,
            lq, lk, subln_g[l], n_heads=n_ha, dqk=dqk, lam_init=lam_init, n_idx=n_idx, d_idx=d_idx)
        ob = sample_dsa(page_table, l, ck_b, cv_b, scores, score_new, zs["qb"], zs["kb32"], zs["vb32"],
                        n_heads=n_hb, n_kv=n_kv, dh=dh, topk=topk_s)
        xs = mix_out(xs, zs, oa, ob, "s")

        for grp, z, bt, tt in (("p", zp, batch, seq), ("s", zs, db, dec_seq)):
            rows[grp][0].append(z["ka32"].reshape(bt, tt, n_ha, 2 * dqk))
            rows[grp][1].append(z["va32"].reshape(bt, tt, n_ha, dv))
            rows[grp][2].append(z["kb32"].reshape(bt, tt, n_kv, dh))
            rows[grp][3].append(z["vb32"].reshape(bt, tt, n_kv, dh))
            rows[grp][4].append(z["kiwi"][:, :d_idx].reshape(bt, tt, d_idx))

        i = l // 2
        if l % 2 == 0:
            hp = rmsnorm(xp, norm_ffn_g[l], BF16)
            hs = rmsnorm(xs, norm_ffn_g[l], F32)
            xp = matmul_residual(gate_up(hp, w_dense_gate[i], w_dense_up[i]), w_dense_down[i].astype(BF16), xp)
            xs = matmul_residual(gate_up(hs, w_dense_gate[i], w_dense_up[i]), w_dense_down[i], xs)
        else:
            xp, xs = moe_ffn(xp, xs, norm_ffn_g[l], w_router[i], w_exp_gate[i], w_exp_up[i], w_exp_down[i])

    y_prompt = rmsnorm(xp, norm_final_g, F32).reshape(batch, seq, d_model)
    y_sample = rmsnorm(xs, norm_final_g, F32).reshape(db, dec_seq, d_model)
    outs_p = [jnp.stack(r, axis=0) for r in rows["p"]]
    outs_s = [jnp.stack(r, axis=0) for r in rows["s"]]
    return (y_prompt, y_sample, *outs_p, *outs_s)
```

```python
import functools
import math

import jax
import jax.numpy as jnp
from jax import lax
from jax.experimental import pallas as pl
from jax.experimental.pallas import tpu as pltpu

LANES = 128
VMEM_LIMIT_BYTES = 56 * 1024 * 1024
ROPE_THETA = 500000.0
ROPE_FRAC = 4
RMS_EPS = 1e-6
TOPK_MAX = 256
PAGE_SIZE = 128
TOP_K_EXPERTS = 2
INT_MIN = -2 ** 31
LOG2_E = math.log2(math.e)
F32 = jnp.float32
BF16 = jnp.bfloat16


def _pick(n, pref, mult):
    if n <= pref:
        return n
    t = (pref // mult) * mult
    while t >= mult:
        if n % t == 0:
            return t
        t -= mult
    return n


def _params(*sem):
    return pltpu.CompilerParams(dimension_semantics=sem, vmem_limit_bytes=VMEM_LIMIT_BYTES)


def _precision(a, b):
    return lax.Precision.HIGHEST if a.dtype == F32 and b.dtype == F32 else None


def _dot(a, b):
    return jnp.dot(a, b, preferred_element_type=F32, precision=_precision(a, b))


def _dot_nt(a, b):
    return lax.dot_general(a, b, (((1,), (1,)), ((), ())), preferred_element_type=F32,
                           precision=_precision(a, b))


def _rms(x, g):
    return x * lax.rsqrt(jnp.mean(x * x, axis=-1, keepdims=True) + RMS_EPS) * g


def _rmsnorm_kernel(x_ref, g_ref, o_ref):
    o_ref[...] = _rms(x_ref[...], g_ref[...]).astype(o_ref.dtype)


def rmsnorm(x, g, out_dtype):
    m, d = x.shape
    tm = _pick(m, 512, 8)
    return pl.pallas_call(
        _rmsnorm_kernel,
        grid=(m // tm,),
        in_specs=[pl.BlockSpec((tm, d), lambda i: (i, 0)), pl.BlockSpec((1, d), lambda i: (0, 0))],
        out_specs=pl.BlockSpec((tm, d), lambda i: (i, 0)),
        out_shape=jax.ShapeDtypeStruct((m, d), out_dtype),
        compiler_params=_params("parallel"),
        name="rmsnorm",
    )(x, g.reshape(1, d))


def _proj_kernel(*refs, half, tn):
    if half:
        x_ref, w_ref, tab_ref, *o_refs = refs
    else:
        x_ref, w_ref, *o_refs = refs
    z = _dot_nt(x_ref[...], w_ref[...])
    if not half:
        for o in o_refs:
            o[...] = z.astype(o.dtype)
        return
    c, s_lo, s_hi = tab_ref[0], tab_ref[1], tab_ref[2]
    for g in range(tn // LANES):
        zg = z[:, g * LANES:(g + 1) * LANES]
        r = zg * c + pltpu.roll(zg, LANES - half, 1) * s_lo + pltpu.roll(zg, half, 1) * s_hi
        for o in o_refs:
            o[:, g * LANES:(g + 1) * LANES] = r.astype(o.dtype)


def project(xn, w, out_dtypes, tab=None, half=0, tn_pref=1024):
    m, k = xn.shape
    n = w.shape[0]
    p_rows = m if tab is None else tab.shape[1]
    assert m % p_rows == 0
    tm = _pick(p_rows, 1024, 8)
    tn = _pick(n, tn_pref, LANES)
    in_specs = [pl.BlockSpec((tm, k), lambda i, j: (i, 0)), pl.BlockSpec((tn, k), lambda i, j: (j, 0))]
    args = [xn, w]
    if tab is not None:
        nblk = p_rows // tm
        in_specs.append(pl.BlockSpec((3, tm, LANES), lambda i, j: (0, i % nblk, 0)))
        args.append(tab)
    outs = pl.pallas_call(
        functools.partial(_proj_kernel, half=half if tab is not None else 0, tn=tn),
        grid=(m // tm, n // tn),
        in_specs=in_specs,
        out_specs=[pl.BlockSpec((tm, tn), lambda i, j: (i, j)) for _ in out_dtypes],
        out_shape=[jax.ShapeDtypeStruct((m, n), dt) for dt in out_dtypes],
        compiler_params=_params("parallel", "arbitrary"),
        name="in_proj",
    )(*args)
    return outs


def _sample_proj_kernel(x_ref, w_ref, tab_ref, o_ref, *, n_in, halves):
    k = pl.program_id(0)

    @pl.when(k == 0)
    def _():
        o_ref[...] = jnp.zeros(o_ref.shape, o_ref.dtype)

    o_ref[:, :n_in] += _dot_nt(x_ref[...], w_ref[0])

    @pl.when(k == pl.num_programs(0) - 1)
    def _():
        for g, half in enumerate(halves):
            if half:
                cols = slice(g * LANES, (g + 1) * LANES)
                zg = o_ref[:, cols]
                o_ref[:, cols] = (zg * tab_ref[0, :, cols] + pltpu.roll(zg, LANES - half, 1) * tab_ref[1, :, cols]
                                  + pltpu.roll(zg, half, 1) * tab_ref[2, :, cols])


def sample_project(xn, w_in_t, layer, tab, halves):
    rows, k = xn.shape
    n_in = w_in_t.shape[1]
    width = LANES * len(halves)
    tk = _pick(k, 256, 8)
    return pl.pallas_call(
        functools.partial(_sample_proj_kernel, n_in=n_in, halves=halves),
        grid=(k // tk,),
        in_specs=[
            pl.BlockSpec((rows, tk), lambda kk: (0, kk)),
            pl.BlockSpec((1, n_in, tk), lambda kk: (layer, 0, kk)),
            pl.BlockSpec((3, rows, width), lambda kk: (0, 0, 0)),
        ],
        out_specs=pl.BlockSpec((rows, width), lambda kk: (0, 0)),
        out_shape=jax.ShapeDtypeStruct((rows, width), F32),
        compiler_params=_params("arbitrary"),
        name="sample_in_proj",
    )(xn, w_in_t, tab)


def rope_table(pos, head_dim, valid_lanes=LANES):
    rot = head_dim // ROPE_FRAC
    half = rot // 2
    inv_freq = jnp.power(ROPE_THETA, -jnp.arange(half, dtype=F32) * 2.0 / rot)
    ang = pos[:, None] * inv_freq[None, :]
    cos, sin = jnp.cos(ang), jnp.sin(ang)
    n = pos.shape[0]
    ones = jnp.ones((n, head_dim - rot), F32)
    zeros_h = jnp.zeros((n, half), F32)
    zeros_r = jnp.zeros((n, head_dim - rot), F32)
    c = jnp.concatenate([cos, cos, ones], axis=1)
    s_lo = jnp.concatenate([-sin, zeros_h, zeros_r], axis=1)
    s_hi = jnp.concatenate([zeros_h, sin, zeros_r], axis=1)
    reps = LANES // head_dim
    tabs = [jnp.tile(t, (1, reps)) for t in (c, s_lo, s_hi)]
    if valid_lanes < LANES:
        lane = jnp.arange(LANES)[None, :]
        tabs = [jnp.where(lane < valid_lanes, tabs[0], 1.0), jnp.where(lane < valid_lanes, tabs[1], 0.0),
                jnp.where(lane < valid_lanes, tabs[2], 0.0)]
    return jnp.stack(tabs, axis=0), half


def _lambda(lq_ref, lk_ref, lam_init):
    lq = lq_ref[...].astype(F32)
    lk = lk_ref[...].astype(F32)
    prod = lq * lk
    return (jnp.exp(jnp.sum(prod[0:1, :], axis=1, keepdims=True))
            - jnp.exp(jnp.sum(prod[1:2, :], axis=1, keepdims=True)) + lam_init)


def _diffattn_kernel(lq_ref, lk_ref, g_ref, q_ref, k_ref, vt_ref, o_ref, m_ref, l_ref, acc_ref,
                     *, tq, dqk, lam_init):
    i = pl.program_id(2)
    c_exp = dqk ** -0.5 * LOG2_E
    q = q_ref[...].astype(F32)
    lane = lax.broadcasted_iota(jnp.int32, q.shape, 1)
    qs = jnp.concatenate([jnp.where(lane < dqk, q, 0.0), jnp.where(lane >= dqk, q, 0.0)], axis=0)
    qs = qs.astype(q_ref.dtype)
    m_ref[...] = jnp.full(m_ref.shape, -jnp.inf, F32)
    l_ref[...] = jnp.zeros(l_ref.shape, F32)
    acc_ref[...] = jnp.zeros(acc_ref.shape, F32)

    def step(j, masked):
        start = pl.multiple_of(j * tq, tq)
        kt = k_ref[pl.ds(start, tq), :]
        vt = vt_ref[:, pl.ds(start, tq)]
        s = _dot_nt(kt, qs) * c_exp
        if masked:
            key = lax.broadcasted_iota(jnp.int32, s.shape, 0)
            qry = lax.broadcasted_iota(jnp.int32, s.shape, 1) & (tq - 1)
            s = jnp.where(key <= qry, s, -jnp.inf)
        m_old = m_ref[...]
        m_new = jnp.maximum(m_old, jnp.max(s, axis=0, keepdims=True))
        alpha = jnp.exp2(m_old - m_new)
        p = jnp.exp2(s - m_new)
        l_ref[...] = alpha * l_ref[...] + jnp.sum(p, axis=0, keepdims=True)
        acc_ref[...] = alpha * acc_ref[...] + _dot(vt, p.astype(vt.dtype))
        m_ref[...] = m_new

    def body(j, carry):
        step(j, False)
        return carry

    lax.fori_loop(0, i, body, 0)
    step(i, True)

    lam = _lambda(lq_ref, lk_ref, lam_init)
    o_all = acc_ref[...] / l_ref[...]
    o = o_all[:, :tq] - lam * o_all[:, tq:]
    inv = lax.rsqrt(jnp.mean(o * o, axis=0, keepdims=True) + RMS_EPS)
    y = o * inv * g_ref[...] * (1.0 - lam_init)
    o_ref[...] = y.T.astype(o_ref.dtype)


def diff_attention_prompt(q, k, v_t, lq, lk, g, *, batch, seq, n_heads, dqk, lam_init):
    m, width = q.shape
    hd = width // n_heads
    tq = _pick(seq, 512, LANES)
    assert tq & (tq - 1) == 0
    nq = seq // tq
    return pl.pallas_call(
        functools.partial(_diffattn_kernel, tq=tq, dqk=dqk, lam_init=lam_init),
        grid=(batch, n_heads, nq),
        in_specs=[
            pl.BlockSpec(lq.shape, lambda b, h, i: (0, 0)),
            pl.BlockSpec(lk.shape, lambda b, h, i: (0, 0)),
            pl.BlockSpec((hd, 1), lambda b, h, i: (0, 0)),
            pl.BlockSpec((tq, hd), lambda b, h, i: (b * nq + i, h)),
            pl.BlockSpec((seq, hd), lambda b, h, i: (b, h)),
            pl.BlockSpec((hd, seq), lambda b, h, i: (h, b)),
        ],
        out_specs=pl.BlockSpec((tq, hd), lambda b, h, i: (b * nq + i, h)),
        out_shape=jax.ShapeDtypeStruct((m, width), BF16),
        scratch_shapes=[pltpu.VMEM((1, 2 * tq), F32), pltpu.VMEM((1, 2 * tq), F32), pltpu.VMEM((hd, 2 * tq), F32)],
        compiler_params=_params("parallel", "parallel", "arbitrary"),
        name="diff_attn_prompt",
    )(lq, lk, g.reshape(hd, 1), q, k, v_t)


def _order_key(score):
    score = jnp.where(score == 0.0, 0.0, score)
    bits = lax.bitcast_convert_type(score, jnp.int32)
    return jnp.where(bits < 0, bits ^ jnp.int32(0x7FFFFFFF), bits)


def _count_rows(mask):
    return jnp.sum(mask.astype(jnp.int32), axis=1, keepdims=True)


def _kth_largest_key(load_keys, shape, k, count_fn):
    t0 = jnp.full(shape, INT_MIN, jnp.int32)
    zero = jnp.zeros(shape, jnp.int32)
    t0 = jnp.where(count_fn(load_keys() >= zero) >= k, zero, t0)

    def body(it, t):
        cand = t | (jnp.int32(1) << (30 - it))
        return jnp.where(count_fn(load_keys() >= cand) >= k, cand, t)

    return lax.fori_loop(0, 31, body, t0)


def _tie_cut(load_eq_pos, shape, need, n_bits, count_fn):
    def body(it, x):
        cand = x | (jnp.int32(1) << (n_bits - 1 - it))
        eq, pos = load_eq_pos()
        return jnp.where(count_fn(eq & (pos < cand)) < need, cand, x)

    return lax.fori_loop(0, n_bits, body, jnp.zeros(shape, jnp.int32))


def _dsa_prompt_kernel(qi_ref, kiwi_k_ref, kiwi_q_ref, qb_ref, kb_ref, vb_ref, o_ref,
                       kdup_ref, wb_ref, score_ref, key_ref, bias_ref, cut_ref,
                       *, tq, seq, extents, n_idx, d_idx, n_heads, n_kv, dh, topk):
    i = pl.program_id(1)

    @pl.when(i == 0)
    def _():
        kf = kiwi_k_ref[...]
        lane = lax.broadcasted_iota(jnp.int32, kf.shape, 1)
        klo = jnp.where(lane < d_idx, kf, 0.0)
        kdup_ref[...] = (klo + pltpu.roll(klo, d_idx, 1)).astype(kdup_ref.dtype)

    wscale = n_idx ** -0.5 * d_idx ** -0.5
    c_exp = dh ** -0.5 * LOG2_E
    rep_shift = (n_heads // n_kv).bit_length() - 1

    w_all = kiwi_q_ref[...] * wscale
    for h in range(n_idx):
        wb_ref[h] = jnp.broadcast_to(w_all[:, d_idx + h:d_idx + h + 1], (tq, LANES))

    def window(j, width):
        return pl.ds(pl.multiple_of(j * width, width), width)

    def run(ncol):
        score_ref[:, :ncol] = jnp.zeros((tq, ncol), F32)

        def head_pair(j, carry):
            grp = qi_ref[:, window(j, LANES)].astype(F32)
            lane = lax.broadcasted_iota(jnp.int32, grp.shape, 1)
            for half in range(2):
                keep = (lane < d_idx) if half == 0 else (lane >= d_idx)
                qh = jnp.where(keep, grp, 0.0).astype(qi_ref.dtype)
                w = wb_ref[2 * j + half][:, 0:1]
                score_ref[:, :ncol] += jnp.maximum(_dot_nt(qh, kdup_ref[:ncol, :]), 0.0) * w
            return carry

        lax.fori_loop(0, n_idx // 2, head_pair, 0)

        row = i * tq + lax.broadcasted_iota(jnp.int32, (tq, ncol), 0)
        col = lax.broadcasted_iota(jnp.int32, (tq, ncol), 1)
        causal = col <= row
        key_ref[:, :ncol] = jnp.where(causal, _order_key(score_ref[:, :ncol]), INT_MIN)

        t = _kth_largest_key(lambda: key_ref[:, :ncol], (tq, 1), topk, _count_rows)
        key = key_ref[:, :ncol]
        eq = key == t
        need = topk - _count_rows(key > t)
        tie = (_count_rows(eq) > need) & (t > INT_MIN)
        cut_ref[...] = jnp.full((tq, 1), ncol, jnp.int32)

        @pl.when(jnp.max(tie.astype(jnp.int32)) > 0)
        def _():
            def load():
                return key_ref[:, :ncol] == t, lax.broadcasted_iota(jnp.int32, (tq, ncol), 1)
            cut_ref[...] = _tie_cut(load, (tq, 1), need, int(math.log2(seq)), _count_rows)

        sel = ((key > t) | (eq & (col <= cut_ref[...]))) & causal
        bias_ref[:, :ncol] = jnp.where(sel, 0.0, -jnp.inf)

        def head(h):
            g = h >> rep_shift
            s = _dot_nt(qb_ref[:, window(h, dh)], kb_ref[:ncol, window(g, dh)]) * c_exp + bias_ref[:, :ncol]
            p = jnp.exp2(s - jnp.max(s, axis=1, keepdims=True))
            l = jnp.sum(p, axis=1, keepdims=True)
            o = _dot(p.astype(vb_ref.dtype), vb_ref[:ncol, window(g, dh)]) / l
            o_ref[:, window(h, dh)] = o.astype(o_ref.dtype)

        def head_group(t, carry):
            for r in range(DSA_HEADS_PER_TRIP):
                head(DSA_HEADS_PER_TRIP * t + r)
            return carry

        lax.fori_loop(0, n_heads // DSA_HEADS_PER_TRIP, head_group, 0)

    lo = 0
    for ncol in extents:
        @pl.when(((i + 1) * tq > lo) & ((i + 1) * tq <= ncol))
        def _(ncol=ncol):
            run(ncol)
        lo = ncol


def dsa_prompt(qi, kiwi, qb, kb, vb, *, batch, seq, n_idx, d_idx, n_heads, n_kv, dh, topk):
    m = qi.shape[0]
    rep = n_heads // n_kv
    assert 2 * d_idx == LANES and dh == LANES and seq & (seq - 1) == 0 and rep & (rep - 1) == 0 and n_idx % 2 == 0
    tq = _pick(seq, 256, LANES)
    nq = seq // tq
    n_ext = min(4, nq)
    extents = tuple(seq * (c + 1) // n_ext for c in range(n_ext))
    return pl.pallas_call(
        functools.partial(_dsa_prompt_kernel, tq=tq, seq=seq, extents=extents, n_idx=n_idx, d_idx=d_idx,
                          n_heads=n_heads,
                          n_kv=n_kv, dh=dh, topk=topk),
        grid=(batch, nq),
        in_specs=[
            pl.BlockSpec((tq, qi.shape[1]), lambda b, i: (b * nq + i, 0)),
            pl.BlockSpec((seq, LANES), lambda b, i: (b, 0)),
            pl.BlockSpec((tq, LANES), lambda b, i: (b * nq + i, 0)),
            pl.BlockSpec((tq, qb.shape[1]), lambda b, i: (b * nq + i, 0)),
            pl.BlockSpec((seq, kb.shape[1]), lambda b, i: (b, 0)),
            pl.BlockSpec((seq, vb.shape[1]), lambda b, i: (b, 0)),
        ],
        out_specs=pl.BlockSpec((tq, qb.shape[1]), lambda b, i: (b * nq + i, 0)),
        out_shape=jax.ShapeDtypeStruct((m, qb.shape[1]), BF16),
        scratch_shapes=[pltpu.VMEM((seq, LANES), BF16), pltpu.VMEM((n_idx, tq, LANES), F32),
                        pltpu.VMEM((tq, seq), F32), pltpu.VMEM((tq, seq), jnp.int32),
                        pltpu.VMEM((tq, seq), F32), pltpu.VMEM((tq, 1), jnp.int32)],
        compiler_params=_params("parallel", "arbitrary"),
        name="dsa_prompt",
    )(qi, kiwi, kiwi, qb, kb, vb)


def _merge_kernel(oa_ref, ob_ref, wa_ref, wb_ref, ga_ref, gb_ref, o_ref):
    ya = _dot(oa_ref[...], wa_ref[...])
    yb = _dot(ob_ref[...], wb_ref[...])
    u = jax.nn.sigmoid(ga_ref[...]) * ya + jax.nn.sigmoid(gb_ref[...]) * yb
    o_ref[...] = u.astype(o_ref.dtype)


def merge_branches(oa, ob, w_pa, w_pb, ga, gb):
    m, ka = oa.shape
    kb = ob.shape[1]
    n = w_pa.shape[1]
    tm = _pick(m, 1024, 8)
    tn = _pick(n, 512, LANES)
    return pl.pallas_call(
        _merge_kernel,
        grid=(m // tm, n // tn),
        in_specs=[
            pl.BlockSpec((tm, ka), lambda i, j: (i, 0)),
            pl.BlockSpec((tm, kb), lambda i, j: (i, 0)),
            pl.BlockSpec((ka, tn), lambda i, j: (0, j)),
            pl.BlockSpec((kb, tn), lambda i, j: (0, j)),
            pl.BlockSpec((tm, tn), lambda i, j: (i, j)),
            pl.BlockSpec((tm, tn), lambda i, j: (i, j)),
        ],
        out_specs=pl.BlockSpec((tm, tn), lambda i, j: (i, j)),
        out_shape=jax.ShapeDtypeStruct((m, n), oa.dtype),
        compiler_params=_params("parallel", "arbitrary"),
        name="merge_branches",
    )(oa, ob, w_pa, w_pb, ga, gb)


def _matmul_res_kernel(a_ref, w_ref, r_ref, o_ref):
    o_ref[...] = r_ref[...] + _dot(a_ref[...], w_ref[...])


def matmul_residual(a, w, res):
    m, k = a.shape
    n = w.shape[1]
    tm = _pick(m, 1024, 8)
    tn = _pick(n, 512, LANES)
    return pl.pallas_call(
        _matmul_res_kernel,
        grid=(m // tm, n // tn),
        in_specs=[
            pl.BlockSpec((tm, k), lambda i, j: (i, 0)),
            pl.BlockSpec((k, tn), lambda i, j: (0, j)),
            pl.BlockSpec((tm, tn), lambda i, j: (i, j)),
        ],
        out_specs=pl.BlockSpec((tm, tn), lambda i, j: (i, j)),
        out_shape=jax.ShapeDtypeStruct((m, n), F32),
        compiler_params=_params("parallel", "arbitrary"),
        name="matmul_residual",
    )(a, w, res)


def _gateup_kernel(x_ref, wg_ref, wu_ref, o_ref, *w_cast):
    x = x_ref[...]
    if w_cast:
        wg_c, wu_c = w_cast

        @pl.when(pl.program_id(1) == 0)
        def _():
            wg_c[...] = wg_ref[...].astype(wg_c.dtype)
            wu_c[...] = wu_ref[...].astype(wu_c.dtype)

        wg_ref, wu_ref = wg_c, wu_c
    gate = _dot(x, wg_ref[...])
    up = _dot(x, wu_ref[...])
    o_ref[...] = (jax.nn.silu(gate) * up).astype(o_ref.dtype)


def gate_up(xn, wg, wu):
    m, k = xn.shape
    n = wg.shape[1]
    tm = _pick(m, 1024, 8)
    tn = _pick(n, 512, LANES)
    scratch = [] if wg.dtype == xn.dtype else [pltpu.VMEM((k, tn), xn.dtype), pltpu.VMEM((k, tn), xn.dtype)]
    return pl.pallas_call(
        _gateup_kernel,
        grid=(n // tn, m // tm),
        in_specs=[
            pl.BlockSpec((tm, k), lambda j, i: (i, 0)),
            pl.BlockSpec((k, tn), lambda j, i: (0, j)),
            pl.BlockSpec((k, tn), lambda j, i: (0, j)),
        ],
        out_specs=pl.BlockSpec((tm, tn), lambda j, i: (i, j)),
        out_shape=jax.ShapeDtypeStruct((m, n), xn.dtype),
        scratch_shapes=scratch,
        compiler_params=_params("parallel", "arbitrary"),
        name="ffn_gate_up",
    )(xn, wg, wu)


def _router_kernel(x_ref, g_ref, wr_ref, base_ref, hn_ref, gate_ref, route_ref, cnt_ref, run_ref, *, n_exp):
    @pl.when(pl.program_id(0) == 0)
    def _():
        run_ref[...] = base_ref[...]

    hn = _rms(x_ref[...], g_ref[...])
    hn_ref[...] = hn
    tm = hn.shape[0]
    lane = lax.broadcasted_iota(jnp.int32, (tm, LANES), 1)
    logits = jnp.full((tm, LANES), -jnp.inf, F32)
    for e in range(n_exp):
        le = jnp.sum(hn * wr_ref[e:e + 1, :], axis=1, keepdims=True)
        logits = jnp.where(lane == e, le, logits)
    v1 = jnp.max(logits, axis=1, keepdims=True)
    i1 = jnp.min(jnp.where(logits == v1, lane, LANES), axis=1, keepdims=True)
    rest = jnp.where(lane == i1, -jnp.inf, logits)
    v2 = jnp.max(rest, axis=1, keepdims=True)
    i2 = jnp.min(jnp.where(rest == v2, lane, LANES), axis=1, keepdims=True)
    e2 = jnp.exp(v2 - v1)
    den = 1.0 + e2
    gate_ref[...] = jnp.where(lane == 0, 1.0 / den, jnp.where(lane == 1, e2 / den, 0.0))

    onehot = jnp.where((lane == i1) | (lane == i2), 1.0, 0.0)
    r = lax.broadcasted_iota(jnp.int32, (tm, tm), 0)
    c = lax.broadcasted_iota(jnp.int32, (tm, tm), 1)
    before = _dot(jnp.where(c < r, 1.0, 0.0).astype(BF16), onehot.astype(BF16)) + run_ref[...]
    r1 = jnp.sum(jnp.where(lane == i1, before, 0.0), axis=1, keepdims=True).astype(jnp.int32)
    r2 = jnp.sum(jnp.where(lane == i2, before, 0.0), axis=1, keepdims=True).astype(jnp.int32)
    route_ref[...] = jnp.where(lane == 0, i1, jnp.where(lane == 1, i2, jnp.where(lane == 2, r1,
                               jnp.where(lane == 3, r2, 0))))
    run_ref[...] = run_ref[...] + jnp.sum(onehot, axis=0, keepdims=True)
    cnt_ref[...] = run_ref[...]


def moe_router(x, g, w_router, base_counts):
    m, d = x.shape
    n_exp = w_router.shape[1]
    assert TOP_K_EXPERTS == 2 and n_exp <= LANES
    tm = _pick(m, 256, 8)
    row = lambda i: (i, 0)
    fixed = lambda i: (0, 0)
    return pl.pallas_call(
        functools.partial(_router_kernel, n_exp=n_exp),
        grid=(m // tm,),
        in_specs=[
            pl.BlockSpec((tm, d), row),
            pl.BlockSpec((1, d), fixed),
            pl.BlockSpec((n_exp, d), fixed),
            pl.BlockSpec((1, LANES), fixed),
        ],
        out_specs=[pl.BlockSpec((tm, d), row), pl.BlockSpec((tm, LANES), row), pl.BlockSpec((tm, LANES), row),
                   pl.BlockSpec((1, LANES), fixed)],
        out_shape=[jax.ShapeDtypeStruct((m, d), F32), jax.ShapeDtypeStruct((m, LANES), F32),
                   jax.ShapeDtypeStruct((m, LANES), jnp.int32), jax.ShapeDtypeStruct((1, LANES), F32)],
        scratch_shapes=[pltpu.VMEM((1, LANES), F32)],
        compiler_params=_params("arbitrary"),
        name="moe_router",
    )(x, g.reshape(1, d), w_router.T, base_counts)


def _row_copy(src_ref, src_row, dst_ref, dst_row, sem):
    return pltpu.make_async_copy(src_ref.at[pl.ds(src_row, 1)], dst_ref.at[pl.ds(dst_row, 1)], sem)


def _dispatch_kernel(pos1_ref, pos2_ref, x_ref, xs_in_ref, xs_ref, sem, *, tm):
    del xs_in_ref
    base = pl.program_id(0) * tm

    def start(r, carry):
        _row_copy(x_ref, r, xs_ref, pos1_ref[base + r], sem).start()
        _row_copy(x_ref, r, xs_ref, pos2_ref[base + r], sem).start()
        return carry

    def wait(r, carry):
        _row_copy(x_ref, 0, xs_ref, 0, sem).wait()
        _row_copy(x_ref, 0, xs_ref, 0, sem).wait()
        return carry

    lax.fori_loop(0, tm, start, 0)
    lax.fori_loop(0, tm, wait, 0)


def moe_dispatch(hn, pos1, pos2, x_sorted):
    m, d = hn.shape
    tm = _pick(m, 256, 8)
    grid_spec = pltpu.PrefetchScalarGridSpec(
        num_scalar_prefetch=2,
        grid=(m // tm,),
        in_specs=[pl.BlockSpec((tm, d), lambda i, p1, p2: (i, 0)), pl.BlockSpec(memory_space=pl.ANY)],
        out_specs=pl.BlockSpec(memory_space=pl.ANY),
        scratch_shapes=[pltpu.SemaphoreType.DMA(())],
    )
    return pl.pallas_call(
        functools.partial(_dispatch_kernel, tm=tm),
        grid_spec=grid_spec,
        out_shape=jax.ShapeDtypeStruct(x_sorted.shape, x_sorted.dtype),
        input_output_aliases={3: 0},
        compiler_params=_params("arbitrary"),
        name="moe_dispatch",
    )(pos1, pos2, hn, x_sorted)


def _combine_kernel(pos1_ref, pos2_ref, x_ref, gate_ref, ys_ref, o_ref, buf_ref, sem, *, tm):
    base = pl.program_id(0) * tm

    def start(r, carry):
        _row_copy(ys_ref, pos1_ref[base + r], buf_ref.at[0], r, sem).start()
        _row_copy(ys_ref, pos2_ref[base + r], buf_ref.at[1], r, sem).start()
        return carry

    def wait(r, carry):
        _row_copy(ys_ref, 0, buf_ref.at[0], 0, sem).wait()
        _row_copy(ys_ref, 0, buf_ref.at[1], 0, sem).wait()
        return carry

    lax.fori_loop(0, tm, start, 0)
    lax.fori_loop(0, tm, wait, 0)
    gate = gate_ref[...]
    o_ref[...] = x_ref[...] + gate[:, 0:1] * buf_ref[0] + gate[:, 1:2] * buf_ref[1]


def moe_combine(x, gates, pos1, pos2, y_sorted):
    m, d = x.shape
    tm = _pick(m, 256, 8)
    row = lambda i, p1, p2: (i, 0)
    grid_spec = pltpu.PrefetchScalarGridSpec(
        num_scalar_prefetch=2,
        grid=(m // tm,),
        in_specs=[pl.BlockSpec((tm, d), row), pl.BlockSpec((tm, LANES), row), pl.BlockSpec(memory_space=pl.ANY)],
        out_specs=pl.BlockSpec((tm, d), row),
        scratch_shapes=[pltpu.VMEM((2, tm, d), F32), pltpu.SemaphoreType.DMA(())],
    )
    return pl.pallas_call(
        functools.partial(_combine_kernel, tm=tm),
        grid_spec=grid_spec,
        out_shape=jax.ShapeDtypeStruct((m, d), F32),
        compiler_params=_params("arbitrary"),
        name="moe_combine",
    )(pos1, pos2, x, gates, y_sorted)


def _expert_gateup_kernel(te_ref, nu_ref, x_ref, wg_ref, wu_ref, o_ref, wg_c, wu_c):
    i = pl.program_id(1)
    used = i < nu_ref[0]
    fresh = (i == 0) | (te_ref[i] != te_ref[jnp.maximum(i - 1, 0)])

    @pl.when(used & fresh)
    def _():
        wg_c[...] = wg_ref[0].astype(wg_c.dtype)
        wu_c[...] = wu_ref[0].astype(wu_c.dtype)

    @pl.when(used)
    def _():
        x = x_ref[...]
        gate = _dot(x, wg_c[...])
        up = _dot(x, wu_c[...])
        o_ref[...] = (jax.nn.silu(gate) * up).astype(o_ref.dtype)

    @pl.when(jnp.logical_not(used))
    def _():
        o_ref[...] = jnp.zeros(o_ref.shape, o_ref.dtype)


def expert_gate_up(x_sorted, wg, wu, tile_expert, n_used, tm):
    p_rows, d = x_sorted.shape
    ff = wg.shape[2]
    tn = _pick(ff, 1024, LANES)

    def w_map(j, i, te, nu):
        return (te[i], 0, j)

    grid_spec = pltpu.PrefetchScalarGridSpec(
        num_scalar_prefetch=2,
        grid=(ff // tn, p_rows // tm),
        in_specs=[
            pl.BlockSpec((tm, d), lambda j, i, te, nu: (jnp.minimum(i, nu[0] - 1), 0)),
            pl.BlockSpec((1, d, tn), w_map),
            pl.BlockSpec((1, d, tn), w_map),
        ],
        out_specs=pl.BlockSpec((tm, tn), lambda j, i, te, nu: (i, j)),
        scratch_shapes=[pltpu.VMEM((d, tn), BF16), pltpu.VMEM((d, tn), BF16)],
    )
    return pl.pallas_call(
        _expert_gateup_kernel,
        grid_spec=grid_spec,
        out_shape=jax.ShapeDtypeStruct((p_rows, ff), BF16),
        compiler_params=_params("parallel", "arbitrary"),
        name="expert_gate_up",
    )(tile_expert, n_used, x_sorted, wg, wu)


def _expert_down_kernel(te_ref, nu_ref, h_ref, wd_ref, o_ref):
    @pl.when(pl.program_id(0) < nu_ref[0])
    def _():
        o_ref[...] = _dot(h_ref[...], wd_ref[0])

    @pl.when(pl.program_id(0) >= nu_ref[0])
    def _():
        o_ref[...] = jnp.zeros(o_ref.shape, o_ref.dtype)


def expert_down(h_sorted, wd, tile_expert, n_used, tm):
    p_rows, ff = h_sorted.shape
    d = wd.shape[2]
    tn = _pick(d, 512, LANES)
    nj = d // tn
    grid_spec = pltpu.PrefetchScalarGridSpec(
        num_scalar_prefetch=2,
        grid=(p_rows // tm, nj),
        in_specs=[
            pl.BlockSpec((tm, ff), lambda i, j, te, nu: (jnp.minimum(i, nu[0] - 1), 0)),
            pl.BlockSpec((1, ff, tn), lambda i, j, te, nu: (te[i], 0, jnp.where(i < nu[0], j, nj - 1))),
        ],
        out_specs=pl.BlockSpec((tm, tn), lambda i, j, te, nu: (i, j)),
    )
    return pl.pallas_call(
        _expert_down_kernel,
        grid_spec=grid_spec,
        out_shape=jax.ShapeDtypeStruct((p_rows, d), F32),
        compiler_params=_params("parallel", "arbitrary"),
        name="expert_down",
    )(tile_expert, n_used, h_sorted, wd)


MOE_ROW_TILE = 512
DSA_HEADS_PER_TRIP = 2
SAMPLE_ATTN_PAGES = 8
SAMPLE_DSA_PAGES = 16


def moe_ffn(xp, xs, g, w_router, wg, wu, wd):
    n_exp = w_router.shape[1]
    d = xp.shape[1]
    tm = MOE_ROW_TILE
    zeros = jnp.zeros((1, LANES), F32)
    hn_p, gate_p, route_p, cnt_p = moe_router(xp, g, w_router, zeros)
    hn_s, gate_s, route_s, cnt = moe_router(xs, g, w_router, cnt_p)

    counts = cnt[0, :n_exp].astype(jnp.int32)
    sizes = (counts + tm - 1) // tm * tm
    ends = jnp.cumsum(sizes)
    starts = ends - sizes
    n_assign = TOP_K_EXPERTS * (xp.shape[0] + xs.shape[0])
    n_tiles = (n_assign + n_exp * (tm - 1) + tm - 1) // tm
    tile_expert = jnp.minimum(jnp.sum(jnp.arange(n_tiles)[:, None] * tm >= ends[None, :], axis=1), n_exp - 1)
    tile_expert = tile_expert.astype(jnp.int32)
    n_used = (ends[-1:] // tm).astype(jnp.int32)

    def slots(route):
        return (jnp.take(starts, route[:, 0]) + route[:, 2], jnp.take(starts, route[:, 1]) + route[:, 3])

    p1_p, p2_p = slots(route_p)
    p1_s, p2_s = slots(route_s)
    x_sorted = jnp.zeros((n_tiles * tm, d), F32)
    x_sorted = moe_dispatch(hn_p, p1_p, p2_p, x_sorted)
    x_sorted = moe_dispatch(hn_s, p1_s, p2_s, x_sorted)
    h_sorted = expert_gate_up(x_sorted.astype(BF16), wg, wu, tile_expert, n_used, tm)
    y_sorted = expert_down(h_sorted, wd.astype(BF16), tile_expert, n_used, tm)
    return (moe_combine(xp, gate_p, p1_p, p2_p, y_sorted), moe_combine(xs, gate_s, p1_s, p2_s, y_sorted))


def _sample_attn_kernel(pt_ref, lq_ref, lk_ref, g_ref, qa_ref, qi_ref, wi_ref, *rest,
                        n_steps, group, n_heads, dqk, lam_init, n_idx, d_idx):
    kp_refs, vp_refs, ip_refs = rest[:group], rest[group:2 * group], rest[2 * group:3 * group]
    kn_ref, vn_ref, in_ref, o_ref, sc_ref, scn_ref, m_ref, l_ref, acc_ref = rest[3 * group:]
    p = pl.program_id(1)
    hd = 2 * dqk
    page = kp_refs[0].shape[2]
    cols = page * n_heads
    scale = dqk ** -0.5

    q = qa_ref[0].astype(F32)
    lane = lax.broadcasted_iota(jnp.int32, q.shape, 1)
    qs = jnp.concatenate([jnp.where(lane < dqk, q, 0.0), jnp.where(lane >= dqk, q, 0.0)], axis=0)

    @pl.when(p == 0)
    def _():
        m_ref[...] = jnp.full(m_ref.shape, -jnp.inf, F32)
        l_ref[...] = jnp.zeros(l_ref.shape, F32)
        acc_ref[...] = jnp.zeros(acc_ref.shape, F32)

    k2 = jnp.concatenate([r[0, 0].reshape(cols, hd).astype(BF16) for r in kp_refs], axis=0)
    v2 = jnp.concatenate([r[0, 0].reshape(cols, hd).astype(BF16) for r in vp_refs], axis=0)
    s = _dot_nt(qs.astype(BF16), k2) * scale
    r_id = lax.broadcasted_iota(jnp.int32, s.shape, 0) & (n_heads - 1)
    c_id = lax.broadcasted_iota(jnp.int32, s.shape, 1) & (n_heads - 1)
    s = jnp.where(r_id == c_id, s, -jnp.inf)
    m_old = m_ref[...]
    m_new = jnp.maximum(m_old, jnp.max(s, axis=1, keepdims=True))
    alpha = jnp.exp(m_old - m_new)
    pr = jnp.exp(s - m_new)
    l_ref[...] = alpha * l_ref[...] + jnp.sum(pr, axis=1, keepdims=True)
    acc_ref[...] = alpha * acc_ref[...] + _dot(pr.astype(BF16), v2)
    m_ref[...] = m_new

    wscale = n_idx ** -0.5 * d_idx ** -0.5
    qi = qi_ref[0]
    w = wi_ref[0] * wscale
    for k, ip_ref in enumerate(ip_refs):
        si = jnp.maximum(_dot(qi, ip_ref[0, 0]), 0.0) * w
        sc_ref[0, k:k + 1, :] = jnp.sum(si, axis=0, keepdims=True)

    @pl.when(p == n_steps - 1)
    def _():
        kn = jnp.concatenate([kn_ref[0], kn_ref[0]], axis=0)
        vn = jnp.concatenate([vn_ref[0], vn_ref[0]], axis=0)
        s_new = jnp.sum(qs * kn, axis=1, keepdims=True) * scale
        m_o = m_ref[...]
        m_n = jnp.maximum(m_o, s_new)
        a = jnp.exp(m_o - m_n)
        p_new = jnp.exp(s_new - m_n)
        l_fin = a * l_ref[...] + p_new
        o_all = (a * acc_ref[...] + p_new * vn) / l_fin
        lam = _lambda(lq_ref, lk_ref, lam_init)
        o = o_all[:n_heads] - lam * o_all[n_heads:]
        o_ref[0] = (_rms(o, g_ref[...]) * (1.0 - lam_init)).astype(o_ref.dtype)
        kin = in_ref[0]
        s_in = jnp.maximum(jnp.sum(qi.astype(F32) * kin, axis=1, keepdims=True), 0.0) * w
        scn_ref[0] = jnp.broadcast_to(jnp.sum(s_in, axis=0, keepdims=True), (1, LANES))


def sample_attention(page_table, layer, cache_k, cache_v, cache_i_t, qa, qi, wi, k_new, v_new, i_new, lq, lk, g,
                     *, n_heads, dqk, lam_init, n_idx, d_idx):
    db, n_pages = page_table.shape
    width = qa.shape[1]
    page = cache_k.shape[2]
    hd = 2 * dqk
    rows = 2 * n_heads
    assert n_heads & (n_heads - 1) == 0
    pt = page_table.reshape(-1)
    group = SAMPLE_ATTN_PAGES if n_pages % SAMPLE_ATTN_PAGES == 0 else 1
    n_steps = n_pages // group

    def pool5(k):
        return lambda b, p, pt_ref: (layer, pt_ref[b * n_pages + p * group + k], 0, 0, 0)

    def pool4(k):
        return lambda b, p, pt_ref: (layer, pt_ref[b * n_pages + p * group + k], 0, 0)

    row3 = lambda b, p, pt_ref: (b, 0, 0)
    const2 = lambda b, p, pt_ref: (0, 0)
    grid_spec = pltpu.PrefetchScalarGridSpec(
        num_scalar_prefetch=1,
        grid=(db, n_steps),
        in_specs=[
            pl.BlockSpec(lq.shape, const2),
            pl.BlockSpec(lk.shape, const2),
            pl.BlockSpec((1, hd), const2),
            pl.BlockSpec((1, n_heads, hd), row3),
            pl.BlockSpec((1, n_idx, d_idx), row3),
            pl.BlockSpec((1, n_idx, 1), row3),
            *[pl.BlockSpec((1, 1, page, n_heads, hd), pool5(k)) for k in range(group)],
            *[pl.BlockSpec((1, 1, page, n_heads, hd), pool5(k)) for k in range(group)],
            *[pl.BlockSpec((1, 1, d_idx, page), pool4(k)) for k in range(group)],
            pl.BlockSpec((1, n_heads, hd), row3),
            pl.BlockSpec((1, n_heads, hd), row3),
            pl.BlockSpec((1, 1, d_idx), row3),
        ],
        out_specs=[
            pl.BlockSpec((1, n_heads, hd), row3),
            pl.BlockSpec((1, group, page), lambda b, p, pt_ref: (b * n_steps + p, 0, 0)),
            pl.BlockSpec((1, 1, LANES), row3),
        ],
        scratch_shapes=[pltpu.VMEM((rows, 1), F32), pltpu.VMEM((rows, 1), F32), pltpu.VMEM((rows, hd), F32)],
    )
    out, scores, score_new = pl.pallas_call(
        functools.partial(_sample_attn_kernel, n_steps=n_steps, group=group, n_heads=n_heads, dqk=dqk,
                          lam_init=lam_init, n_idx=n_idx, d_idx=d_idx),
        grid_spec=grid_spec,
        out_shape=[jax.ShapeDtypeStruct((db, n_heads, hd), F32),
                   jax.ShapeDtypeStruct((db * n_steps, group, page), F32),
                   jax.ShapeDtypeStruct((db, 1, LANES), F32)],
        compiler_params=_params("parallel", "arbitrary"),
        name="sample_attention",
    )(pt, lq, lk, g.reshape(1, hd), qa.reshape(db, n_heads, hd), qi.reshape(db, n_idx, d_idx),
      wi.reshape(db, n_idx, 1), *([cache_k] * group), *([cache_v] * group), *([cache_i_t] * group),
      k_new.reshape(db, n_heads, hd), v_new.reshape(db, n_heads, hd), i_new.reshape(db, 1, d_idx))
    return out.reshape(db, width), scores.reshape(db, n_pages, page), score_new


def _count_all(mask):
    c = jnp.sum(mask.astype(jnp.int32), axis=1, keepdims=True)
    return jnp.sum(c, axis=0, keepdims=True)


def _sample_dsa_kernel(pt_ref, sc_ref, scn_ref, qb_ref, *rest, n_pages, group, page, n_heads, n_kv, dh, topk):
    kp_refs, vp_refs = rest[:group], rest[group:2 * group]
    kn_ref, vn_ref, o_ref, key_ref, keyn_ref, t_ref, cut_ref, selx_ref, m_ref, l_ref, acc_ref = rest[2 * group:]
    p = pl.program_id(1)
    rep = n_heads // n_kv
    scale = dh ** -0.5
    past = n_pages * page
    cols = page * n_kv
    kv_shift = n_kv.bit_length() - 1
    rep_shift = rep.bit_length() - 1

    @pl.when(p == 0)
    def _():
        m_ref[...] = jnp.full(m_ref.shape, -jnp.inf, F32)
        l_ref[...] = jnp.zeros(l_ref.shape, F32)
        acc_ref[...] = jnp.zeros(acc_ref.shape, F32)
        key_ref[...] = _order_key(sc_ref[0])
        keyn_ref[...] = _order_key(scn_ref[0])
        key_new = keyn_ref[:, 0:1]

        def cnt(t):
            return _count_all(key_ref[...] >= t) + (key_new >= t).astype(jnp.int32)

        t0 = jnp.full((1, 1), INT_MIN, jnp.int32)
        zero = jnp.zeros((1, 1), jnp.int32)
        t0 = jnp.where(cnt(zero) >= topk, zero, t0)

        def body(it, t):
            cand = t | (jnp.int32(1) << (30 - it))
            return jnp.where(cnt(cand) >= topk, cand, t)

        t = lax.fori_loop(0, 31, body, t0)
        t_ref[...] = t
        key = key_ref[...]
        n_gt = _count_all(key > t) + (key_new > t).astype(jnp.int32)
        need = topk - n_gt
        pos = (lax.broadcasted_iota(jnp.int32, key.shape, 0) * page
               + lax.broadcasted_iota(jnp.int32, key.shape, 1))
        n_bits = int(math.ceil(math.log2(past + 1)))

        def body2(it, x):
            cand = x | (jnp.int32(1) << (n_bits - 1 - it))
            c = _count_all((key_ref[...] == t) & (pos < cand)) + ((key_new == t) & (past < cand)).astype(jnp.int32)
            return jnp.where(c < need, cand, x)

        cut = lax.fori_loop(0, n_bits, body2, jnp.zeros((1, 1), jnp.int32))
        cut_ref[...] = cut
        sel = (key > t) | ((key == t) & (pos <= cut))
        e_t = lax.broadcasted_iota(jnp.int32, (page, cols), 0)
        e_c = lax.broadcasted_iota(jnp.int32, (page, cols), 1)
        spread = jnp.where((e_c >> kv_shift) == e_t, 1.0, 0.0).astype(BF16)
        selx_ref[...] = _dot(jnp.where(sel, 1.0, 0.0).astype(BF16), spread)

    t = t_ref[...]
    cut = cut_ref[...]
    q = qb_ref[0]

    k2 = jnp.concatenate([r[0, 0].astype(BF16) for r in kp_refs], axis=0)
    v2 = jnp.concatenate([r[0, 0].astype(BF16) for r in vp_refs], axis=0)
    sel_cols = jnp.concatenate(
        [jnp.broadcast_to(selx_ref[pl.ds(p * group + k, 1), :], (n_heads, cols)) for k in range(group)], axis=1)
    r_id = lax.broadcasted_iota(jnp.int32, (n_heads, group * cols), 0)
    c_id = lax.broadcasted_iota(jnp.int32, (n_heads, group * cols), 1)
    keep = (sel_cols > 0.5) & ((c_id & (n_kv - 1)) == (r_id >> rep_shift))
    s = _dot_nt(q.astype(BF16), k2) * scale
    s = jnp.where(keep, s, -jnp.inf)
    m_old = m_ref[...]
    m_new = jnp.maximum(m_old, jnp.max(s, axis=1, keepdims=True))
    m_safe = jnp.where(m_new == -jnp.inf, 0.0, m_new)
    alpha = jnp.exp(m_old - m_safe)
    pr = jnp.exp(s - m_safe)
    l_ref[...] = alpha * l_ref[...] + jnp.sum(pr, axis=1, keepdims=True)
    acc_ref[...] = alpha * acc_ref[...] + _dot(pr.astype(BF16), v2)
    m_ref[...] = m_new

    @pl.when(p == n_pages // group - 1)
    def _():
        key_new = keyn_ref[:, 0:1]
        sel_new = (key_new > t) | ((key_new == t) & (past <= cut))
        h_id = lax.broadcasted_iota(jnp.int32, (n_heads, dh), 0) >> rep_shift
        kn = jnp.zeros((n_heads, dh), F32)
        vn = jnp.zeros((n_heads, dh), F32)
        for g in range(n_kv):
            kn = jnp.where(h_id == g, kn_ref[0, g:g + 1, :], kn)
            vn = jnp.where(h_id == g, vn_ref[0, g:g + 1, :], vn)
        s_new = jnp.sum(q.astype(F32) * kn, axis=1, keepdims=True) * scale
        s_new = jnp.where(sel_new, s_new, -jnp.inf)
        m_o = m_ref[...]
        m_n = jnp.maximum(m_o, s_new)
        m_s = jnp.where(m_n == -jnp.inf, 0.0, m_n)
        a = jnp.exp(m_o - m_s)
        p_new = jnp.exp(s_new - m_s)
        l_fin = a * l_ref[...] + p_new
        o_ref[0] = ((a * acc_ref[...] + p_new * vn) / l_fin).astype(o_ref.dtype)


def sample_dsa(page_table, layer, cache_k, cache_v, scores, score_new, qb, k_new, v_new,
               *, n_heads, n_kv, dh, topk):
    db, n_pages = page_table.shape
    cols = cache_k.shape[2]
    page = cols // n_kv
    rep = n_heads // n_kv
    assert n_kv & (n_kv - 1) == 0 and rep & (rep - 1) == 0
    pt = page_table.reshape(-1)

    group = SAMPLE_DSA_PAGES if n_pages % SAMPLE_DSA_PAGES == 0 else 1

    def pool_map(k):
        return lambda b, p, pt_ref: (layer, pt_ref[b * n_pages + p * group + k], 0, 0)

    row3 = lambda b, p, pt_ref: (b, 0, 0)
    grid_spec = pltpu.PrefetchScalarGridSpec(
        num_scalar_prefetch=1,
        grid=(db, n_pages // group),
        in_specs=[
            pl.BlockSpec((1, n_pages, page), row3),
            pl.BlockSpec((1, 1, LANES), row3),
            pl.BlockSpec((1, n_heads, dh), row3),
            *[pl.BlockSpec((1, 1, cols, dh), pool_map(k)) for k in range(group)],
            *[pl.BlockSpec((1, 1, cols, dh), pool_map(k)) for k in range(group)],
            pl.BlockSpec((1, n_kv, dh), row3),
            pl.BlockSpec((1, n_kv, dh), row3),
        ],
        out_specs=pl.BlockSpec((1, n_heads, dh), row3),
        scratch_shapes=[pltpu.VMEM((n_pages, page), jnp.int32), pltpu.VMEM((1, LANES), jnp.int32),
                        pltpu.VMEM((1, 1), jnp.int32), pltpu.VMEM((1, 1), jnp.int32),
                        pltpu.VMEM((n_pages, cols), F32),
                        pltpu.VMEM((n_heads, 1), F32), pltpu.VMEM((n_heads, 1), F32),
                        pltpu.VMEM((n_heads, dh), F32)],
    )
    out = pl.pallas_call(
        functools.partial(_sample_dsa_kernel, n_pages=n_pages, group=group, page=page, n_heads=n_heads,
                          n_kv=n_kv, dh=dh, topk=topk),
        grid_spec=grid_spec,
        out_shape=jax.ShapeDtypeStruct((db, n_heads, dh), F32),
        compiler_params=_params("parallel", "arbitrary"),
        name="sample_dsa",
    )(pt, scores, score_new, qb.reshape(db, n_heads, dh), *([cache_k] * group), *([cache_v] * group),
      k_new.reshape(db, n_kv, dh), v_new.reshape(db, n_kv, dh))
    return out.reshape(db, n_heads * dh)


def kernel(x_prompt, x_sample, cache_a_k, cache_a_v, cache_b_k, cache_b_v, cache_idx_k, page_table, w_in, lambda_q, lambda_k, subln_g, w_branch_a, w_branch_b, w_out, norm_mix_g, norm_ffn_g, w_dense_gate, w_dense_up, w_dense_down, w_router, w_exp_gate, w_exp_up, w_exp_down, norm_final_g):
    batch, seq, d_model = x_prompt.shape
    db, dec_seq, _ = x_sample.shape
    assert dec_seq == 1, "one new token per sample row"
    depth, n_pool, page, n_ha, a_width = cache_a_k.shape
    assert page == PAGE_SIZE
    dqk = a_width // 2
    dv = cache_a_v.shape[4]
    n_kv, dh = cache_b_k.shape[3:]
    d_idx = cache_idx_k.shape[3]
    n_hb = w_branch_b.shape[1] // dh
    n_in = w_in.shape[2]
    wa, wva, wqb, wkb = n_ha * 2 * dqk, n_ha * dv, n_hb * dh, n_kv * dh
    n_idx = (n_in - (2 * wa + wva + wqb + 2 * wkb + d_idx + 2 * d_model)) // (d_idx + 1)
    assert dv == 2 * dqk == LANES and dh == LANES and 2 * d_idx == LANES
    n_pages = page_table.shape[1]
    past_len = n_pages * page
    topk_p = min(TOPK_MAX, seq // 4)
    topk_s = min(TOPK_MAX, (past_len + dec_seq) // 4)
    n_exp = w_router.shape[2]

    offs = [0]
    for wdt in (wa, wa, wva, wqb, wkb, wkb, n_idx * d_idx, d_idx, n_idx, d_model, d_model):
        offs.append(offs[-1] + wdt)
    o_qa, o_ka, o_va, o_qb, o_kb, o_vb, o_qi, o_ki, o_wi, o_ga, o_gb, _ = offs

    pos_p = jnp.arange(seq, dtype=F32)
    pos_s = jnp.full((db,), float(past_len), F32)
    tabs = {}
    for name, pos in (("p", pos_p), ("s", pos_s)):
        tabs[name, "a"] = rope_table(pos, dqk)
        tabs[name, "b"] = rope_table(pos, dh)
        tabs[name, "i"] = rope_table(pos, d_idx, valid_lanes=d_idx)

    n_groups = -(-n_in // LANES)
    kind = {}
    for start, stop, name in ((o_qa, o_va, "a"), (o_qb, o_vb, "b"), (o_qi, o_ki, "a"), (o_ki, o_ki + LANES, "i")):
        assert start % LANES == 0 and stop % LANES == 0
        for g in range(start // LANES, stop // LANES):
            kind[g] = name
    identity = jnp.stack([jnp.ones((db, LANES), F32), jnp.zeros((db, LANES), F32), jnp.zeros((db, LANES), F32)])
    tab_s = jnp.concatenate([tabs["s", kind[g]][0] if g in kind else identity for g in range(n_groups)], axis=2)
    halves_s = tuple(tabs["s", kind[g]][1] if g in kind else 0 for g in range(n_groups))

    ck_b = cache_b_k.reshape(depth, n_pool, page * n_kv, dh)
    cv_b = cache_b_v.reshape(depth, n_pool, page * n_kv, dh)
    ci_t = jnp.swapaxes(cache_idx_k, 2, 3)
    w_in_t = jnp.swapaxes(w_in, 1, 2)

    xp = x_prompt.reshape(batch * seq, d_model)
    xs = x_sample.reshape(db * dec_seq, d_model)
    rows = {"p": [[] for _ in range(5)], "s": [[] for _ in range(5)]}

    for l in range(depth):
        lam_init = 0.8 - 0.6 * math.exp(-0.3 * l)
        lq, lk = lambda_q[l], lambda_k[l]
        wl = w_in_t[l].astype(BF16)
        w_kiwi = jnp.pad(wl[o_ki:o_ga], ((0, LANES - d_idx - n_idx), (0, 0)))
        mix_w = {"p": dict(pa=w_branch_a[l].astype(BF16), pb=w_branch_b[l].astype(BF16), o=w_out[l].astype(BF16)),
                 "s": dict(pa=w_branch_a[l], pb=w_branch_b[l], o=w_out[l])}

        def in_proj_prompt(x):
            xn = rmsnorm(x, norm_mix_g[l], BF16)
            ta, ha = tabs["p", "a"]
            tb, hb = tabs["p", "b"]
            ti, hi = tabs["p", "i"]
            z = {}
            z["qa"], = project(xn, wl[o_qa:o_ka], [BF16], ta, ha)
            z["ka32"], z["ka"] = project(xn, wl[o_ka:o_va], [F32, BF16], ta, ha)
            z["va32"], z["va"] = project(xn, wl[o_va:o_qb], [F32, BF16])
            z["qb"], = project(xn, wl[o_qb:o_kb], [BF16], tb, hb)
            z["kb32"], z["kb"] = project(xn, wl[o_kb:o_vb], [F32, BF16], tb, hb)
            z["vb32"], z["vb"] = project(xn, wl[o_vb:o_qi], [F32, BF16])
            z["qi"], = project(xn, wl[o_qi:o_ki], [BF16], ta, ha)
            z["kiwi"], = project(xn, w_kiwi, [F32], ti, hi)
            z["ga"], = project(xn, wl[o_ga:o_gb], [F32])
            z["gb"], = project(xn, wl[o_gb:], [F32])
            return z

        def in_proj_sample(x):
            xn = rmsnorm(x, norm_mix_g[l], F32)
            zz = sample_project(xn, w_in_t, l, tab_s, halves_s)
            z = {name: zz[:, a:b] for name, a, b in (
                ("qa", o_qa, o_ka), ("ka32", o_ka, o_va), ("va32", o_va, o_qb), ("qb", o_qb, o_kb),
                ("kb32", o_kb, o_vb), ("vb32", o_vb, o_qi), ("qi", o_qi, o_ki), ("kiwi", o_ki, o_ki + LANES),
                ("ga", o_ga, o_gb), ("gb", o_gb, n_in))}
            return z

        def mix_out(x, z, oa, ob, grp):
            w = mix_w[grp]
            u = merge_branches(oa, ob, w["pa"], w["pb"], z["ga"], z["gb"])
            return matmul_residual(u, w["o"], x)

        zp = in_proj_prompt(xp)
        oa = diff_attention_prompt(zp["qa"], zp["ka"], zp["va"].T, lq, lk, subln_g[l], batch=batch, seq=seq,
                                   n_heads=n_ha, dqk=dqk, lam_init=lam_init)
        ob = dsa_prompt(zp["qi"], zp["kiwi"], zp["qb"], zp["kb"], zp["vb"], batch=batch, seq=seq, n_idx=n_idx,
                        d_idx=d_idx, n_heads=n_hb, n_kv=n_kv, dh=dh, topk=topk_p)
        xp = mix_out(xp, zp, oa, ob, "p")

        zs = in_proj_sample(xs)
        ki_s = zs["kiwi"][:, :d_idx]
        wi_s = zs["kiwi"][:, d_idx:d_idx + n_idx]
        oa, scores, score_new = sample_attention(
            page_table, l, cache_a_k, cache_a_v, ci_t, zs["qa"], zs["qi"], wi_s, zs["ka32"], zs["va32"], ki_s,
            lq, lk, subln_g[l], n_heads=n_ha, dqk=dqk, lam_init=lam_init, n_idx=n_idx, d_idx=d_idx)
        ob = sample_dsa(page_table, l, ck_b, cv_b, scores, score_new, zs["qb"], zs["kb32"], zs["vb32"],
                        n_heads=n_hb, n_kv=n_kv, dh=dh, topk=topk_s)
        xs = mix_out(xs, zs, oa, ob, "s")

        for grp, z, bt, tt in (("p", zp, batch, seq), ("s", zs, db, dec_seq)):
            rows[grp][0].append(z["ka32"].reshape(bt, tt, n_ha, 2 * dqk))
            rows[grp][1].append(z["va32"].reshape(bt, tt, n_ha, dv))
            rows[grp][2].append(z["kb32"].reshape(bt, tt, n_kv, dh))
            rows[grp][3].append(z["vb32"].reshape(bt, tt, n_kv, dh))
            rows[grp][4].append(z["kiwi"][:, :d_idx].reshape(bt, tt, d_idx))

        i = l // 2
        if l % 2 == 0:
            hp = rmsnorm(xp, norm_ffn_g[l], BF16)
            hs = rmsnorm(xs, norm_ffn_g[l], F32)
            xp = matmul_residual(gate_up(hp, w_dense_gate[i], w_dense_up[i]), w_dense_down[i].astype(BF16), xp)
            xs = matmul_residual(gate_up(hs, w_dense_gate[i], w_dense_up[i]), w_dense_down[i], xs)
        else:
            xp, xs = moe_ffn(xp, xs, norm_ffn_g[l], w_router[i], w_exp_gate[i], w_exp_up[i], w_exp_down[i])

    y_prompt = rmsnorm(xp, norm_final_g, F32).reshape(batch, seq, d_model)
    y_sample = rmsnorm(xs, norm_final_g, F32).reshape(db, dec_seq, d_model)
    outs_p = [jnp.stack(r, axis=0) for r in rows["p"]]
    outs_s = [jnp.stack(r, axis=0) for r in rows["s"]]
    return (y_prompt, y_sample, *outs_p, *outs_s)
```

```python
import functools
import math

import jax
import jax.numpy as jnp
from jax import lax
from jax.experimental import pallas as pl
from jax.experimental.pallas import tpu as pltpu

LANES = 128
VMEM_LIMIT_BYTES = 56 * 1024 * 1024
ROPE_THETA = 500000.0
ROPE_FRAC = 4
RMS_EPS = 1e-6
TOPK_MAX = 256
PAGE_SIZE = 128
TOP_K_EXPERTS = 2
INT_MIN = -2 ** 31
LOG2_E = math.log2(math.e)
F32 = jnp.float32
BF16 = jnp.bfloat16


def _pick(n, pref, mult):
    if n <= pref:
        return n
    t = (pref // mult) * mult
    while t >= mult:
        if n % t == 0:
            return t
        t -= mult
    return n


def _params(*sem):
    return pltpu.CompilerParams(dimension_semantics=sem, vmem_limit_bytes=VMEM_LIMIT_BYTES)


def _precision(a, b):
    return lax.Precision.HIGHEST if a.dtype == F32 and b.dtype == F32 else None


def _dot(a, b):
    return jnp.dot(a, b, preferred_element_type=F32, precision=_precision(a, b))


def _dot_nt(a, b):
    return lax.dot_general(a, b, (((1,), (1,)), ((), ())), preferred_element_type=F32,
                           precision=_precision(a, b))


def _rms(x, g):
    return x * lax.rsqrt(jnp.mean(x * x, axis=-1, keepdims=True) + RMS_EPS) * g


def _rmsnorm_kernel(x_ref, g_ref, o_ref):
    o_ref[...] = _rms(x_ref[...], g_ref[...]).astype(o_ref.dtype)


def rmsnorm(x, g, out_dtype):
    m, d = x.shape
    tm = _pick(m, 512, 8)
    return pl.pallas_call(
        _rmsnorm_kernel,
        grid=(m // tm,),
        in_specs=[pl.BlockSpec((tm, d), lambda i: (i, 0)), pl.BlockSpec((1, d), lambda i: (0, 0))],
        out_specs=pl.BlockSpec((tm, d), lambda i: (i, 0)),
        out_shape=jax.ShapeDtypeStruct((m, d), out_dtype),
        compiler_params=_params("parallel"),
        name="rmsnorm",
    )(x, g.reshape(1, d))


def _proj_kernel(*refs, half, tn):
    if half:
        x_ref, w_ref, tab_ref, *o_refs = refs
    else:
        x_ref, w_ref, *o_refs = refs
    z = _dot_nt(x_ref[...], w_ref[...])
    if not half:
        for o in o_refs:
            o[...] = z.astype(o.dtype)
        return
    c, s_lo, s_hi = tab_ref[0], tab_ref[1], tab_ref[2]
    for g in range(tn // LANES):
        zg = z[:, g * LANES:(g + 1) * LANES]
        r = zg * c + pltpu.roll(zg, LANES - half, 1) * s_lo + pltpu.roll(zg, half, 1) * s_hi
        for o in o_refs:
            o[:, g * LANES:(g + 1) * LANES] = r.astype(o.dtype)


def project(xn, w, out_dtypes, tab=None, half=0, tn_pref=1024):
    m, k = xn.shape
    n = w.shape[0]
    p_rows = m if tab is None else tab.shape[1]
    assert m % p_rows == 0
    tm = _pick(p_rows, 1024, 8)
    tn = _pick(n, tn_pref, LANES)
    in_specs = [pl.BlockSpec((tm, k), lambda i, j: (i, 0)), pl.BlockSpec((tn, k), lambda i, j: (j, 0))]
    args = [xn, w]
    if tab is not None:
        nblk = p_rows // tm
        in_specs.append(pl.BlockSpec((3, tm, LANES), lambda i, j: (0, i % nblk, 0)))
        args.append(tab)
    outs = pl.pallas_call(
        functools.partial(_proj_kernel, half=half if tab is not None else 0, tn=tn),
        grid=(m // tm, n // tn),
        in_specs=in_specs,
        out_specs=[pl.BlockSpec((tm, tn), lambda i, j: (i, j)) for _ in out_dtypes],
        out_shape=[jax.ShapeDtypeStruct((m, n), dt) for dt in out_dtypes],
        compiler_params=_params("parallel", "arbitrary"),
        name="in_proj",
    )(*args)
    return outs


def _sample_proj_kernel(x_ref, w_ref, tab_ref, o_ref, *, n_in, halves):
    k = pl.program_id(0)

    @pl.when(k == 0)
    def _():
        o_ref[...] = jnp.zeros(o_ref.shape, o_ref.dtype)

    o_ref[:, :n_in] += _dot_nt(x_ref[...], w_ref[0])

    @pl.when(k == pl.num_programs(0) - 1)
    def _():
        for g, half in enumerate(halves):
            if half:
                cols = slice(g * LANES, (g + 1) * LANES)
                zg = o_ref[:, cols]
                o_ref[:, cols] = (zg * tab_ref[0, :, cols] + pltpu.roll(zg, LANES - half, 1) * tab_ref[1, :, cols]
                                  + pltpu.roll(zg, half, 1) * tab_ref[2, :, cols])


def sample_project(xn, w_in_t, layer, tab, halves):
    rows, k = xn.shape
    n_in = w_in_t.shape[1]
    width = LANES * len(halves)
    tk = _pick(k, 256, 8)
    return pl.pallas_call(
        functools.partial(_sample_proj_kernel, n_in=n_in, halves=halves),
        grid=(k // tk,),
        in_specs=[
            pl.BlockSpec((rows, tk), lambda kk: (0, kk)),
            pl.BlockSpec((1, n_in, tk), lambda kk: (layer, 0, kk)),
            pl.BlockSpec((3, rows, width), lambda kk: (0, 0, 0)),
        ],
        out_specs=pl.BlockSpec((rows, width), lambda kk: (0, 0)),
        out_shape=jax.ShapeDtypeStruct((rows, width), F32),
        compiler_params=_params("arbitrary"),
        name="sample_in_proj",
    )(xn, w_in_t, tab)


def rope_table(pos, head_dim, valid_lanes=LANES):
    rot = head_dim // ROPE_FRAC
    half = rot // 2
    inv_freq = jnp.power(ROPE_THETA, -jnp.arange(half, dtype=F32) * 2.0 / rot)
    ang = pos[:, None] * inv_freq[None, :]
    cos, sin = jnp.cos(ang), jnp.sin(ang)
    n = pos.shape[0]
    ones = jnp.ones((n, head_dim - rot), F32)
    zeros_h = jnp.zeros((n, half), F32)
    zeros_r = jnp.zeros((n, head_dim - rot), F32)
    c = jnp.concatenate([cos, cos, ones], axis=1)
    s_lo = jnp.concatenate([-sin, zeros_h, zeros_r], axis=1)
    s_hi = jnp.concatenate([zeros_h, sin, zeros_r], axis=1)
    reps = LANES // head_dim
    tabs = [jnp.tile(t, (1, reps)) for t in (c, s_lo, s_hi)]
    if valid_lanes < LANES:
        lane = jnp.arange(LANES)[None, :]
        tabs = [jnp.where(lane < valid_lanes, tabs[0], 1.0), jnp.where(lane < valid_lanes, tabs[1], 0.0),
                jnp.where(lane < valid_lanes, tabs[2], 0.0)]
    return jnp.stack(tabs, axis=0), half


def _lambda(lq_ref, lk_ref, lam_init):
    lq = lq_ref[...].astype(F32)
    lk = lk_ref[...].astype(F32)
    prod = lq * lk
    return (jnp.exp(jnp.sum(prod[0:1, :], axis=1, keepdims=True))
            - jnp.exp(jnp.sum(prod[1:2, :], axis=1, keepdims=True)) + lam_init)


def _diffattn_kernel(lq_ref, lk_ref, g_ref, q_ref, k_ref, vt_ref, o_ref, m_ref, l_ref, acc_ref,
                     *, tq, dqk, lam_init):
    i = pl.program_id(2)
    c_exp = dqk ** -0.5 * LOG2_E
    q = q_ref[...].astype(F32)
    lane = lax.broadcasted_iota(jnp.int32, q.shape, 1)
    qs = jnp.concatenate([jnp.where(lane < dqk, q, 0.0), jnp.where(lane >= dqk, q, 0.0)], axis=0)
    qs = qs.astype(q_ref.dtype)
    m_ref[...] = jnp.full(m_ref.shape, -jnp.inf, F32)
    l_ref[...] = jnp.zeros(l_ref.shape, F32)
    acc_ref[...] = jnp.zeros(acc_ref.shape, F32)

    def step(j, masked):
        start = pl.multiple_of(j * tq, tq)
        kt = k_ref[pl.ds(start, tq), :]
        vt = vt_ref[:, pl.ds(start, tq)]
        s = _dot_nt(kt, qs) * c_exp
        if masked:
            key = lax.broadcasted_iota(jnp.int32, s.shape, 0)
            qry = lax.broadcasted_iota(jnp.int32, s.shape, 1) & (tq - 1)
            s = jnp.where(key <= qry, s, -jnp.inf)
        m_old = m_ref[...]
        m_new = jnp.maximum(m_old, jnp.max(s, axis=0, keepdims=True))
        alpha = jnp.exp2(m_old - m_new)
        p = jnp.exp2(s - m_new)
        l_ref[...] = alpha * l_ref[...] + jnp.sum(p, axis=0, keepdims=True)
        acc_ref[...] = alpha * acc_ref[...] + _dot(vt, p.astype(vt.dtype))
        m_ref[...] = m_new

    def body(j, carry):
        step(j, False)
        return carry

    lax.fori_loop(0, i, body, 0)
    step(i, True)

    lam = _lambda(lq_ref, lk_ref, lam_init)
    o_all = acc_ref[...] / l_ref[...]
    o = o_all[:, :tq] - lam * o_all[:, tq:]
    inv = lax.rsqrt(jnp.mean(o * o, axis=0, keepdims=True) + RMS_EPS)
    y = o * inv * g_ref[...] * (1.0 - lam_init)
    o_ref[...] = y.T.astype(o_ref.dtype)


def diff_attention_prompt(q, k, v_t, lq, lk, g, *, batch, seq, n_heads, dqk, lam_init):
    m, width = q.shape
    hd = width // n_heads
    tq = _pick(seq, 512, LANES)
    assert tq & (tq - 1) == 0
    nq = seq // tq
    return pl.pallas_call(
        functools.partial(_diffattn_kernel, tq=tq, dqk=dqk, lam_init=lam_init),
        grid=(batch, n_heads, nq),
        in_specs=[
            pl.BlockSpec(lq.shape, lambda b, h, i: (0, 0)),
            pl.BlockSpec(lk.shape, lambda b, h, i: (0, 0)),
            pl.BlockSpec((hd, 1), lambda b, h, i: (0, 0)),
            pl.BlockSpec((tq, hd), lambda b, h, i: (b * nq + i, h)),
            pl.BlockSpec((seq, hd), lambda b, h, i: (b, h)),
            pl.BlockSpec((hd, seq), lambda b, h, i: (h, b)),
        ],
        out_specs=pl.BlockSpec((tq, hd), lambda b, h, i: (b * nq + i, h)),
        out_shape=jax.ShapeDtypeStruct((m, width), BF16),
        scratch_shapes=[pltpu.VMEM((1, 2 * tq), F32), pltpu.VMEM((1, 2 * tq), F32), pltpu.VMEM((hd, 2 * tq), F32)],
        compiler_params=_params("parallel", "parallel", "arbitrary"),
        name="diff_attn_prompt",
    )(lq, lk, g.reshape(hd, 1), q, k, v_t)


def _order_key(score):
    score = jnp.where(score == 0.0, 0.0, score)
    bits = lax.bitcast_convert_type(score, jnp.int32)
    return jnp.where(bits < 0, bits ^ jnp.int32(0x7FFFFFFF), bits)


def _count_rows(mask):
    return jnp.sum(mask.astype(jnp.int32), axis=1, keepdims=True)


def _kth_largest_key(load_keys, shape, k, count_fn):
    t0 = jnp.full(shape, INT_MIN, jnp.int32)
    zero = jnp.zeros(shape, jnp.int32)
    t0 = jnp.where(count_fn(load_keys() >= zero) >= k, zero, t0)

    def body(it, t):
        cand = t | (jnp.int32(1) << (30 - it))
        return jnp.where(count_fn(load_keys() >= cand) >= k, cand, t)

    return lax.fori_loop(0, 31, body, t0)


def _tie_cut(load_eq_pos, shape, need, n_bits, count_fn):
    def body(it, x):
        cand = x | (jnp.int32(1) << (n_bits - 1 - it))
        eq, pos = load_eq_pos()
        return jnp.where(count_fn(eq & (pos < cand)) < need, cand, x)

    return lax.fori_loop(0, n_bits, body, jnp.zeros(shape, jnp.int32))


def _dsa_prompt_kernel(qi_ref, kiwi_k_ref, kiwi_q_ref, qb_ref, kb_ref, vb_ref, o_ref,
                       kdup_ref, wb_ref, score_ref, key_ref, bias_ref, cut_ref,
                       *, tq, seq, extents, n_idx, d_idx, n_heads, n_kv, dh, topk):
    i = pl.program_id(1)

    @pl.when(i == 0)
    def _():
        kf = kiwi_k_ref[...]
        lane = lax.broadcasted_iota(jnp.int32, kf.shape, 1)
        klo = jnp.where(lane < d_idx, kf, 0.0)
        kdup_ref[...] = (klo + pltpu.roll(klo, d_idx, 1)).astype(kdup_ref.dtype)

    wscale = n_idx ** -0.5 * d_idx ** -0.5
    c_exp = dh ** -0.5 * LOG2_E
    rep_shift = (n_heads // n_kv).bit_length() - 1

    w_all = kiwi_q_ref[...] * wscale
    for h in range(n_idx):
        wb_ref[h] = jnp.broadcast_to(w_all[:, d_idx + h:d_idx + h + 1], (tq, LANES))

    def window(j, width):
        return pl.ds(pl.multiple_of(j * width, width), width)

    def run(ncol):
        score_ref[:, :ncol] = jnp.zeros((tq, ncol), F32)

        def head_pairs(t, carry):
            part = None
            for r in range(IDX_PAIRS_PER_TRIP):
                j = IDX_PAIRS_PER_TRIP * t + r
                grp = qi_ref[:, window(j, LANES)].astype(F32)
                lane = lax.broadcasted_iota(jnp.int32, grp.shape, 1)
                for half in range(2):
                    keep = (lane < d_idx) if half == 0 else (lane >= d_idx)
                    qh = jnp.where(keep, grp, 0.0).astype(qi_ref.dtype)
                    w = wb_ref[2 * j + half][:, 0:1]
                    term = jnp.maximum(_dot_nt(qh, kdup_ref[:ncol, :]), 0.0) * w
                    part = term if part is None else part + term
            score_ref[:, :ncol] += part
            return carry

        lax.fori_loop(0, n_idx // (2 * IDX_PAIRS_PER_TRIP), head_pairs, 0)

        row = i * tq + lax.broadcasted_iota(jnp.int32, (tq, ncol), 0)
        col = lax.broadcasted_iota(jnp.int32, (tq, ncol), 1)
        causal = col <= row
        key_ref[:, :ncol] = jnp.where(causal, _order_key(score_ref[:, :ncol]), INT_MIN)

        t = _kth_largest_key(lambda: key_ref[:, :ncol], (tq, 1), topk, _count_rows)
        key = key_ref[:, :ncol]
        eq = key == t
        need = topk - _count_rows(key > t)
        tie = (_count_rows(eq) > need) & (t > INT_MIN)
        cut_ref[...] = jnp.full((tq, 1), ncol, jnp.int32)

        @pl.when(jnp.max(tie.astype(jnp.int32)) > 0)
        def _():
            def load():
                return key_ref[:, :ncol] == t, lax.broadcasted_iota(jnp.int32, (tq, ncol), 1)
            cut_ref[...] = _tie_cut(load, (tq, 1), need, int(math.log2(seq)), _count_rows)

        sel = ((key > t) | (eq & (col <= cut_ref[...]))) & causal
        bias_ref[:, :ncol] = jnp.where(sel, 0.0, -jnp.inf)

        def head(h):
            g = h >> rep_shift
            s = _dot_nt(qb_ref[:, window(h, dh)], kb_ref[:ncol, window(g, dh)]) * c_exp + bias_ref[:, :ncol]
            p = jnp.exp2(s - jnp.max(s, axis=1, keepdims=True))
            l = jnp.sum(p, axis=1, keepdims=True)
            o = _dot(p.astype(vb_ref.dtype), vb_ref[:ncol, window(g, dh)]) / l
            o_ref[:, window(h, dh)] = o.astype(o_ref.dtype)

        def head_group(t, carry):
            for r in range(DSA_HEADS_PER_TRIP):
                head(DSA_HEADS_PER_TRIP * t + r)
            return carry

        lax.fori_loop(0, n_heads // DSA_HEADS_PER_TRIP, head_group, 0)

    lo = 0
    for ncol in extents:
        @pl.when(((i + 1) * tq > lo) & ((i + 1) * tq <= ncol))
        def _(ncol=ncol):
            run(ncol)
        lo = ncol


def dsa_prompt(qi, kiwi, qb, kb, vb, *, batch, seq, n_idx, d_idx, n_heads, n_kv, dh, topk):
    m = qi.shape[0]
    rep = n_heads // n_kv
    assert 2 * d_idx == LANES and dh == LANES and seq & (seq - 1) == 0 and rep & (rep - 1) == 0 and n_idx % 2 == 0
    tq = _pick(seq, 256, LANES)
    nq = seq // tq
    n_ext = min(4, nq)
    extents = tuple(seq * (c + 1) // n_ext for c in range(n_ext))
    return pl.pallas_call(
        functools.partial(_dsa_prompt_kernel, tq=tq, seq=seq, extents=extents, n_idx=n_idx, d_idx=d_idx,
                          n_heads=n_heads,
                          n_kv=n_kv, dh=dh, topk=topk),
        grid=(batch, nq),
        in_specs=[
            pl.BlockSpec((tq, qi.shape[1]), lambda b, i: (b * nq + i, 0)),
            pl.BlockSpec((seq, LANES), lambda b, i: (b, 0)),
            pl.BlockSpec((tq, LANES), lambda b, i: (b * nq + i, 0)),
            pl.BlockSpec((tq, qb.shape[1]), lambda b, i: (b * nq + i, 0)),
            pl.BlockSpec((seq, kb.shape[1]), lambda b, i: (b, 0)),
            pl.BlockSpec((seq, vb.shape[1]), lambda b, i: (b, 0)),
        ],
        out_specs=pl.BlockSpec((tq, qb.shape[1]), lambda b, i: (b * nq + i, 0)),
        out_shape=jax.ShapeDtypeStruct((m, qb.shape[1]), BF16),
        scratch_shapes=[pltpu.VMEM((seq, LANES), BF16), pltpu.VMEM((n_idx, tq, LANES), F32),
                        pltpu.VMEM((tq, seq), F32), pltpu.VMEM((tq, seq), jnp.int32),
                        pltpu.VMEM((tq, seq), F32), pltpu.VMEM((tq, 1), jnp.int32)],
        compiler_params=_params("parallel", "arbitrary"),
        name="dsa_prompt",
    )(qi, kiwi, kiwi, qb, kb, vb)


def _merge_kernel(oa_ref, ob_ref, wa_ref, wb_ref, ga_ref, gb_ref, o_ref):
    ya = _dot(oa_ref[...], wa_ref[...])
    yb = _dot(ob_ref[...], wb_ref[...])
    u = jax.nn.sigmoid(ga_ref[...]) * ya + jax.nn.sigmoid(gb_ref[...]) * yb
    o_ref[...] = u.astype(o_ref.dtype)


def merge_branches(oa, ob, w_pa, w_pb, ga, gb):
    m, ka = oa.shape
    kb = ob.shape[1]
    n = w_pa.shape[1]
    tm = _pick(m, 1024, 8)
    tn = _pick(n, 512, LANES)
    return pl.pallas_call(
        _merge_kernel,
        grid=(m // tm, n // tn),
        in_specs=[
            pl.BlockSpec((tm, ka), lambda i, j: (i, 0)),
            pl.BlockSpec((tm, kb), lambda i, j: (i, 0)),
            pl.BlockSpec((ka, tn), lambda i, j: (0, j)),
            pl.BlockSpec((kb, tn), lambda i, j: (0, j)),
            pl.BlockSpec((tm, tn), lambda i, j: (i, j)),
            pl.BlockSpec((tm, tn), lambda i, j: (i, j)),
        ],
        out_specs=pl.BlockSpec((tm, tn), lambda i, j: (i, j)),
        out_shape=jax.ShapeDtypeStruct((m, n), oa.dtype),
        compiler_params=_params("parallel", "arbitrary"),
        name="merge_branches",
    )(oa, ob, w_pa, w_pb, ga, gb)


def _matmul_res_kernel(a_ref, w_ref, r_ref, o_ref):
    o_ref[...] = r_ref[...] + _dot(a_ref[...], w_ref[...])


def matmul_residual(a, w, res):
    m, k = a.shape
    n = w.shape[1]
    tm = _pick(m, 1024, 8)
    tn = _pick(n, 512, LANES)
    return pl.pallas_call(
        _matmul_res_kernel,
        grid=(m // tm, n // tn),
        in_specs=[
            pl.BlockSpec((tm, k), lambda i, j: (i, 0)),
            pl.BlockSpec((k, tn), lambda i, j: (0, j)),
            pl.BlockSpec((tm, tn), lambda i, j: (i, j)),
        ],
        out_specs=pl.BlockSpec((tm, tn), lambda i, j: (i, j)),
        out_shape=jax.ShapeDtypeStruct((m, n), F32),
        compiler_params=_params("parallel", "arbitrary"),
        name="matmul_residual",
    )(a, w, res)


def _gateup_kernel(x_ref, wg_ref, wu_ref, o_ref, *w_cast):
    x = x_ref[...]
    if w_cast:
        wg_c, wu_c = w_cast

        @pl.when(pl.program_id(1) == 0)
        def _():
            wg_c[...] = wg_ref[...].astype(wg_c.dtype)
            wu_c[...] = wu_ref[...].astype(wu_c.dtype)

        wg_ref, wu_ref = wg_c, wu_c
    gate = _dot(x, wg_ref[...])
    up = _dot(x, wu_ref[...])
    o_ref[...] = (jax.nn.silu(gate) * up).astype(o_ref.dtype)


def gate_up(xn, wg, wu):
    m, k = xn.shape
    n = wg.shape[1]
    tm = _pick(m, 1024, 8)
    tn = _pick(n, 512, LANES)
    scratch = [] if wg.dtype == xn.dtype else [pltpu.VMEM((k, tn), xn.dtype), pltpu.VMEM((k, tn), xn.dtype)]
    return pl.pallas_call(
        _gateup_kernel,
        grid=(n // tn, m // tm),
        in_specs=[
            pl.BlockSpec((tm, k), lambda j, i: (i, 0)),
            pl.BlockSpec((k, tn), lambda j, i: (0, j)),
            pl.BlockSpec((k, tn), lambda j, i: (0, j)),
        ],
        out_specs=pl.BlockSpec((tm, tn), lambda j, i: (i, j)),
        out_shape=jax.ShapeDtypeStruct((m, n), xn.dtype),
        scratch_shapes=scratch,
        compiler_params=_params("parallel", "arbitrary"),
        name="ffn_gate_up",
    )(xn, wg, wu)


def _router_kernel(x_ref, g_ref, wr_ref, base_ref, hn_ref, gate_ref, route_ref, cnt_ref, run_ref, *, n_exp):
    @pl.when(pl.program_id(0) == 0)
    def _():
        run_ref[...] = base_ref[...]

    hn = _rms(x_ref[...], g_ref[...])
    hn_ref[...] = hn
    tm = hn.shape[0]
    lane = lax.broadcasted_iota(jnp.int32, (tm, LANES), 1)
    logits = jnp.full((tm, LANES), -jnp.inf, F32)
    for e in range(n_exp):
        le = jnp.sum(hn * wr_ref[e:e + 1, :], axis=1, keepdims=True)
        logits = jnp.where(lane == e, le, logits)
    v1 = jnp.max(logits, axis=1, keepdims=True)
    i1 = jnp.min(jnp.where(logits == v1, lane, LANES), axis=1, keepdims=True)
    rest = jnp.where(lane == i1, -jnp.inf, logits)
    v2 = jnp.max(rest, axis=1, keepdims=True)
    i2 = jnp.min(jnp.where(rest == v2, lane, LANES), axis=1, keepdims=True)
    e2 = jnp.exp(v2 - v1)
    den = 1.0 + e2
    gate_ref[...] = jnp.where(lane == 0, 1.0 / den, jnp.where(lane == 1, e2 / den, 0.0))

    onehot = jnp.where((lane == i1) | (lane == i2), 1.0, 0.0)
    r = lax.broadcasted_iota(jnp.int32, (tm, tm), 0)
    c = lax.broadcasted_iota(jnp.int32, (tm, tm), 1)
    before = _dot(jnp.where(c < r, 1.0, 0.0).astype(BF16), onehot.astype(BF16)) + run_ref[...]
    r1 = jnp.sum(jnp.where(lane == i1, before, 0.0), axis=1, keepdims=True).astype(jnp.int32)
    r2 = jnp.sum(jnp.where(lane == i2, before, 0.0), axis=1, keepdims=True).astype(jnp.int32)
    route_ref[...] = jnp.where(lane == 0, i1, jnp.where(lane == 1, i2, jnp.where(lane == 2, r1,
                               jnp.where(lane == 3, r2, 0))))
    run_ref[...] = run_ref[...] + jnp.sum(onehot, axis=0, keepdims=True)
    cnt_ref[...] = run_ref[...]


def moe_router(x, g, w_router, base_counts):
    m, d = x.shape
    n_exp = w_router.shape[1]
    assert TOP_K_EXPERTS == 2 and n_exp <= LANES
    tm = _pick(m, 256, 8)
    row = lambda i: (i, 0)
    fixed = lambda i: (0, 0)
    return pl.pallas_call(
        functools.partial(_router_kernel, n_exp=n_exp),
        grid=(m // tm,),
        in_specs=[
            pl.BlockSpec((tm, d), row),
            pl.BlockSpec((1, d), fixed),
            pl.BlockSpec((n_exp, d), fixed),
            pl.BlockSpec((1, LANES), fixed),
        ],
        out_specs=[pl.BlockSpec((tm, d), row), pl.BlockSpec((tm, LANES), row), pl.BlockSpec((tm, LANES), row),
                   pl.BlockSpec((1, LANES), fixed)],
        out_shape=[jax.ShapeDtypeStruct((m, d), F32), jax.ShapeDtypeStruct((m, LANES), F32),
                   jax.ShapeDtypeStruct((m, LANES), jnp.int32), jax.ShapeDtypeStruct((1, LANES), F32)],
        scratch_shapes=[pltpu.VMEM((1, LANES), F32)],
        compiler_params=_params("arbitrary"),
        name="moe_router",
    )(x, g.reshape(1, d), w_router.T, base_counts)


def _row_copy(src_ref, src_row, dst_ref, dst_row, sem):
    return pltpu.make_async_copy(src_ref.at[pl.ds(src_row, 1)], dst_ref.at[pl.ds(dst_row, 1)], sem)


def _dispatch_kernel(pos1_ref, pos2_ref, x_ref, xs_in_ref, xs_ref, sem, *, tm):
    del xs_in_ref
    base = pl.program_id(0) * tm

    def start(r, carry):
        _row_copy(x_ref, r, xs_ref, pos1_ref[base + r], sem).start()
        _row_copy(x_ref, r, xs_ref, pos2_ref[base + r], sem).start()
        return carry

    def wait(r, carry):
        _row_copy(x_ref, 0, xs_ref, 0, sem).wait()
        _row_copy(x_ref, 0, xs_ref, 0, sem).wait()
        return carry

    lax.fori_loop(0, tm, start, 0)
    lax.fori_loop(0, tm, wait, 0)


def moe_dispatch(hn, pos1, pos2, x_sorted):
    m, d = hn.shape
    tm = _pick(m, 256, 8)
    grid_spec = pltpu.PrefetchScalarGridSpec(
        num_scalar_prefetch=2,
        grid=(m // tm,),
        in_specs=[pl.BlockSpec((tm, d), lambda i, p1, p2: (i, 0)), pl.BlockSpec(memory_space=pl.ANY)],
        out_specs=pl.BlockSpec(memory_space=pl.ANY),
        scratch_shapes=[pltpu.SemaphoreType.DMA(())],
    )
    return pl.pallas_call(
        functools.partial(_dispatch_kernel, tm=tm),
        grid_spec=grid_spec,
        out_shape=jax.ShapeDtypeStruct(x_sorted.shape, x_sorted.dtype),
        input_output_aliases={3: 0},
        compiler_params=_params("arbitrary"),
        name="moe_dispatch",
    )(pos1, pos2, hn, x_sorted)


def _combine_kernel(pos1_ref, pos2_ref, x_ref, gate_ref, ys_ref, o_ref, buf_ref, sem, *, tm):
    base = pl.program_id(0) * tm

    def start(r, carry):
        _row_copy(ys_ref, pos1_ref[base + r], buf_ref.at[0], r, sem).start()
        _row_copy(ys_ref, pos2_ref[base + r], buf_ref.at[1], r, sem).start()
        return carry

    def wait(r, carry):
        _row_copy(ys_ref, 0, buf_ref.at[0], 0, sem).wait()
        _row_copy(ys_ref, 0, buf_ref.at[1], 0, sem).wait()
        return carry

    lax.fori_loop(0, tm, start, 0)
    lax.fori_loop(0, tm, wait, 0)
    gate = gate_ref[...]
    o_ref[...] = x_ref[...] + gate[:, 0:1] * buf_ref[0] + gate[:, 1:2] * buf_ref[1]


def moe_combine(x, gates, pos1, pos2, y_sorted):
    m, d = x.shape
    tm = _pick(m, 256, 8)
    row = lambda i, p1, p2: (i, 0)
    grid_spec = pltpu.PrefetchScalarGridSpec(
        num_scalar_prefetch=2,
        grid=(m // tm,),
        in_specs=[pl.BlockSpec((tm, d), row), pl.BlockSpec((tm, LANES), row), pl.BlockSpec(memory_space=pl.ANY)],
        out_specs=pl.BlockSpec((tm, d), row),
        scratch_shapes=[pltpu.VMEM((2, tm, d), F32), pltpu.SemaphoreType.DMA(())],
    )
    return pl.pallas_call(
        functools.partial(_combine_kernel, tm=tm),
        grid_spec=grid_spec,
        out_shape=jax.ShapeDtypeStruct((m, d), F32),
        compiler_params=_params("arbitrary"),
        name="moe_combine",
    )(pos1, pos2, x, gates, y_sorted)


def _expert_gateup_kernel(te_ref, nu_ref, x_ref, wg_ref, wu_ref, o_ref, wg_c, wu_c):
    i = pl.program_id(1)
    used = i < nu_ref[0]
    fresh = (i == 0) | (te_ref[i] != te_ref[jnp.maximum(i - 1, 0)])

    @pl.when(used & fresh)
    def _():
        wg_c[...] = wg_ref[0].astype(wg_c.dtype)
        wu_c[...] = wu_ref[0].astype(wu_c.dtype)

    @pl.when(used)
    def _():
        x = x_ref[...]
        gate = _dot(x, wg_c[...])
        up = _dot(x, wu_c[...])
        o_ref[...] = (jax.nn.silu(gate) * up).astype(o_ref.dtype)

    @pl.when(jnp.logical_not(used))
    def _():
        o_ref[...] = jnp.zeros(o_ref.shape, o_ref.dtype)


def expert_gate_up(x_sorted, wg, wu, tile_expert, n_used, tm):
    p_rows, d = x_sorted.shape
    ff = wg.shape[2]
    tn = _pick(ff, 1024, LANES)

    def w_map(j, i, te, nu):
        return (te[i], 0, j)

    grid_spec = pltpu.PrefetchScalarGridSpec(
        num_scalar_prefetch=2,
        grid=(ff // tn, p_rows // tm),
        in_specs=[
            pl.BlockSpec((tm, d), lambda j, i, te, nu: (jnp.minimum(i, nu[0] - 1), 0)),
            pl.BlockSpec((1, d, tn), w_map),
            pl.BlockSpec((1, d, tn), w_map),
        ],
        out_specs=pl.BlockSpec((tm, tn), lambda j, i, te, nu: (i, j)),
        scratch_shapes=[pltpu.VMEM((d, tn), BF16), pltpu.VMEM((d, tn), BF16)],
    )
    return pl.pallas_call(
        _expert_gateup_kernel,
        grid_spec=grid_spec,
        out_shape=jax.ShapeDtypeStruct((p_rows, ff), BF16),
        compiler_params=_params("parallel", "arbitrary"),
        name="expert_gate_up",
    )(tile_expert, n_used, x_sorted, wg, wu)


def _expert_down_kernel(te_ref, nu_ref, h_ref, wd_ref, o_ref):
    @pl.when(pl.program_id(0) < nu_ref[0])
    def _():
        o_ref[...] = _dot(h_ref[...], wd_ref[0])

    @pl.when(pl.program_id(0) >= nu_ref[0])
    def _():
        o_ref[...] = jnp.zeros(o_ref.shape, o_ref.dtype)


def expert_down(h_sorted, wd, tile_expert, n_used, tm):
    p_rows, ff = h_sorted.shape
    d = wd.shape[2]
    tn = _pick(d, 512, LANES)
    nj = d // tn
    grid_spec = pltpu.PrefetchScalarGridSpec(
        num_scalar_prefetch=2,
        grid=(p_rows // tm, nj),
        in_specs=[
            pl.BlockSpec((tm, ff), lambda i, j, te, nu: (jnp.minimum(i, nu[0] - 1), 0)),
            pl.BlockSpec((1, ff, tn), lambda i, j, te, nu: (te[i], 0, jnp.where(i < nu[0], j, nj - 1))),
        ],
        out_specs=pl.BlockSpec((tm, tn), lambda i, j, te, nu: (i, j)),
    )
    return pl.pallas_call(
        _expert_down_kernel,
        grid_spec=grid_spec,
        out_shape=jax.ShapeDtypeStruct((p_rows, d), F32),
        compiler_params=_params("parallel", "arbitrary"),
        name="expert_down",
    )(tile_expert, n_used, h_sorted, wd)


MOE_ROW_TILE = 512
IDX_PAIRS_PER_TRIP = 2
DSA_HEADS_PER_TRIP = 4
SAMPLE_ATTN_PAGES = 8
SAMPLE_DSA_PAGES = 16


def moe_ffn(xp, xs, g, w_router, wg, wu, wd):
    n_exp = w_router.shape[1]
    d = xp.shape[1]
    tm = MOE_ROW_TILE
    zeros = jnp.zeros((1, LANES), F32)
    hn_p, gate_p, route_p, cnt_p = moe_router(xp, g, w_router, zeros)
    hn_s, gate_s, route_s, cnt = moe_router(xs, g, w_router, cnt_p)

    counts = cnt[0, :n_exp].astype(jnp.int32)
    sizes = (counts + tm - 1) // tm * tm
    ends = jnp.cumsum(sizes)
    starts = ends - sizes
    n_assign = TOP_K_EXPERTS * (xp.shape[0] + xs.shape[0])
    n_tiles = (n_assign + n_exp * (tm - 1) + tm - 1) // tm
    tile_expert = jnp.minimum(jnp.sum(jnp.arange(n_tiles)[:, None] * tm >= ends[None, :], axis=1), n_exp - 1)
    tile_expert = tile_expert.astype(jnp.int32)
    n_used = (ends[-1:] // tm).astype(jnp.int32)

    def slots(route):
        return (jnp.take(starts, route[:, 0]) + route[:, 2], jnp.take(starts, route[:, 1]) + route[:, 3])

    p1_p, p2_p = slots(route_p)
    p1_s, p2_s = slots(route_s)
    x_sorted = jnp.zeros((n_tiles * tm, d), F32)
    x_sorted = moe_dispatch(hn_p, p1_p, p2_p, x_sorted)
    x_sorted = moe_dispatch(hn_s, p1_s, p2_s, x_sorted)
    h_sorted = expert_gate_up(x_sorted.astype(BF16), wg, wu, tile_expert, n_used, tm)
    y_sorted = expert_down(h_sorted, wd.astype(BF16), tile_expert, n_used, tm)
    return (moe_combine(xp, gate_p, p1_p, p2_p, y_sorted), moe_combine(xs, gate_s, p1_s, p2_s, y_sorted))


def _sample_attn_kernel(pt_ref, lq_ref, lk_ref, g_ref, qa_ref, qi_ref, wi_ref, *rest,
                        n_steps, group, n_heads, dqk, lam_init, n_idx, d_idx):
    kp_refs, vp_refs, ip_refs = rest[:group], rest[group:2 * group], rest[2 * group:3 * group]
    kn_ref, vn_ref, in_ref, o_ref, sc_ref, scn_ref, m_ref, l_ref, acc_ref = rest[3 * group:]
    p = pl.program_id(1)
    hd = 2 * dqk
    page = kp_refs[0].shape[2]
    cols = page * n_heads
    scale = dqk ** -0.5

    q = qa_ref[0].astype(F32)
    lane = lax.broadcasted_iota(jnp.int32, q.shape, 1)
    qs = jnp.concatenate([jnp.where(lane < dqk, q, 0.0), jnp.where(lane >= dqk, q, 0.0)], axis=0)

    @pl.when(p == 0)
    def _():
        m_ref[...] = jnp.full(m_ref.shape, -jnp.inf, F32)
        l_ref[...] = jnp.zeros(l_ref.shape, F32)
        acc_ref[...] = jnp.zeros(acc_ref.shape, F32)

    k2 = jnp.concatenate([r[0, 0].reshape(cols, hd).astype(BF16) for r in kp_refs], axis=0)
    v2 = jnp.concatenate([r[0, 0].reshape(cols, hd).astype(BF16) for r in vp_refs], axis=0)
    s = _dot_nt(qs.astype(BF16), k2) * scale
    r_id = lax.broadcasted_iota(jnp.int32, s.shape, 0) & (n_heads - 1)
    c_id = lax.broadcasted_iota(jnp.int32, s.shape, 1) & (n_heads - 1)
    s = jnp.where(r_id == c_id, s, -jnp.inf)
    m_old = m_ref[...]
    m_new = jnp.maximum(m_old, jnp.max(s, axis=1, keepdims=True))
    alpha = jnp.exp(m_old - m_new)
    pr = jnp.exp(s - m_new)
    l_ref[...] = alpha * l_ref[...] + jnp.sum(pr, axis=1, keepdims=True)
    acc_ref[...] = alpha * acc_ref[...] + _dot(pr.astype(BF16), v2)
    m_ref[...] = m_new

    wscale = n_idx ** -0.5 * d_idx ** -0.5
    qi = qi_ref[0]
    w = wi_ref[0] * wscale
    for k, ip_ref in enumerate(ip_refs):
        si = jnp.maximum(_dot(qi, ip_ref[0, 0]), 0.0) * w
        sc_ref[0, k:k + 1, :] = jnp.sum(si, axis=0, keepdims=True)

    @pl.when(p == n_steps - 1)
    def _():
        kn = jnp.concatenate([kn_ref[0], kn_ref[0]], axis=0)
        vn = jnp.concatenate([vn_ref[0], vn_ref[0]], axis=0)
        s_new = jnp.sum(qs * kn, axis=1, keepdims=True) * scale
        m_o = m_ref[...]
        m_n = jnp.maximum(m_o, s_new)
        a = jnp.exp(m_o - m_n)
        p_new = jnp.exp(s_new - m_n)
        l_fin = a * l_ref[...] + p_new
        o_all = (a * acc_ref[...] + p_new * vn) / l_fin
        lam = _lambda(lq_ref, lk_ref, lam_init)
        o = o_all[:n_heads] - lam * o_all[n_heads:]
        o_ref[0] = (_rms(o, g_ref[...]) * (1.0 - lam_init)).astype(o_ref.dtype)
        kin = in_ref[0]
        s_in = jnp.maximum(jnp.sum(qi.astype(F32) * kin, axis=1, keepdims=True), 0.0) * w
        scn_ref[0] = jnp.broadcast_to(jnp.sum(s_in, axis=0, keepdims=True), (1, LANES))


def sample_attention(page_table, layer, cache_k, cache_v, cache_i_t, qa, qi, wi, k_new, v_new, i_new, lq, lk, g,
                     *, n_heads, dqk, lam_init, n_idx, d_idx):
    db, n_pages = page_table.shape
    width = qa.shape[1]
    page = cache_k.shape[2]
    hd = 2 * dqk
    rows = 2 * n_heads
    assert n_heads & (n_heads - 1) == 0
    pt = page_table.reshape(-1)
    group = SAMPLE_ATTN_PAGES if n_pages % SAMPLE_ATTN_PAGES == 0 else 1
    n_steps = n_pages // group

    def pool5(k):
        return lambda b, p, pt_ref: (layer, pt_ref[b * n_pages + p * group + k], 0, 0, 0)

    def pool4(k):
        return lambda b, p, pt_ref: (layer, pt_ref[b * n_pages + p * group + k], 0, 0)

    row3 = lambda b, p, pt_ref: (b, 0, 0)
    const2 = lambda b, p, pt_ref: (0, 0)
    grid_spec = pltpu.PrefetchScalarGridSpec(
        num_scalar_prefetch=1,
        grid=(db, n_steps),
        in_specs=[
            pl.BlockSpec(lq.shape, const2),
            pl.BlockSpec(lk.shape, const2),
            pl.BlockSpec((1, hd), const2),
            pl.BlockSpec((1, n_heads, hd), row3),
            pl.BlockSpec((1, n_idx, d_idx), row3),
            pl.BlockSpec((1, n_idx, 1), row3),
            *[pl.BlockSpec((1, 1, page, n_heads, hd), pool5(k)) for k in range(group)],
            *[pl.BlockSpec((1, 1, page, n_heads, hd), pool5(k)) for k in range(group)],
            *[pl.BlockSpec((1, 1, d_idx, page), pool4(k)) for k in range(group)],
            pl.BlockSpec((1, n_heads, hd), row3),
            pl.BlockSpec((1, n_heads, hd), row3),
            pl.BlockSpec((1, 1, d_idx), row3),
        ],
        out_specs=[
            pl.BlockSpec((1, n_heads, hd), row3),
            pl.BlockSpec((1, group, page), lambda b, p, pt_ref: (b * n_steps + p, 0, 0)),
            pl.BlockSpec((1, 1, LANES), row3),
        ],
        scratch_shapes=[pltpu.VMEM((rows, 1), F32), pltpu.VMEM((rows, 1), F32), pltpu.VMEM((rows, hd), F32)],
    )
    out, scores, score_new = pl.pallas_call(
        functools.partial(_sample_attn_kernel, n_steps=n_steps, group=group, n_heads=n_heads, dqk=dqk,
                          lam_init=lam_init, n_idx=n_idx, d_idx=d_idx),
        grid_spec=grid_spec,
        out_shape=[jax.ShapeDtypeStruct((db, n_heads, hd), F32),
                   jax.ShapeDtypeStruct((db * n_steps, group, page), F32),
                   jax.ShapeDtypeStruct((db, 1, LANES), F32)],
        compiler_params=_params("parallel", "arbitrary"),
        name="sample_attention",
    )(pt, lq, lk, g.reshape(1, hd), qa.reshape(db, n_heads, hd), qi.reshape(db, n_idx, d_idx),
      wi.reshape(db, n_idx, 1), *([cache_k] * group), *([cache_v] * group), *([cache_i_t] * group),
      k_new.reshape(db, n_heads, hd), v_new.reshape(db, n_heads, hd), i_new.reshape(db, 1, d_idx))
    return out.reshape(db, width), scores.reshape(db, n_pages, page), score_new


def _count_all(mask):
    c = jnp.sum(mask.astype(jnp.int32), axis=1, keepdims=True)
    return jnp.sum(c, axis=0, keepdims=True)


def _sample_dsa_kernel(pt_ref, sc_ref, scn_ref, qb_ref, *rest, n_pages, group, page, n_heads, n_kv, dh, topk):
    kp_refs, vp_refs = rest[:group], rest[group:2 * group]
    kn_ref, vn_ref, o_ref, key_ref, keyn_ref, t_ref, cut_ref, selx_ref, m_ref, l_ref, acc_ref = rest[2 * group:]
    p = pl.program_id(1)
    rep = n_heads // n_kv
    scale = dh ** -0.5
    past = n_pages * page
    cols = page * n_kv
    kv_shift = n_kv.bit_length() - 1
    rep_shift = rep.bit_length() - 1

    @pl.when(p == 0)
    def _():
        m_ref[...] = jnp.full(m_ref.shape, -jnp.inf, F32)
        l_ref[...] = jnp.zeros(l_ref.shape, F32)
        acc_ref[...] = jnp.zeros(acc_ref.shape, F32)
        key_ref[...] = _order_key(sc_ref[0])
        keyn_ref[...] = _order_key(scn_ref[0])
        key_new = keyn_ref[:, 0:1]

        def cnt(t):
            return _count_all(key_ref[...] >= t) + (key_new >= t).astype(jnp.int32)

        t0 = jnp.full((1, 1), INT_MIN, jnp.int32)
        zero = jnp.zeros((1, 1), jnp.int32)
        t0 = jnp.where(cnt(zero) >= topk, zero, t0)

        def body(it, t):
            cand = t | (jnp.int32(1) << (30 - it))
            return jnp.where(cnt(cand) >= topk, cand, t)

        t = lax.fori_loop(0, 31, body, t0)
        t_ref[...] = t
        key = key_ref[...]
        n_gt = _count_all(key > t) + (key_new > t).astype(jnp.int32)
        need = topk - n_gt
        pos = (lax.broadcasted_iota(jnp.int32, key.shape, 0) * page
               + lax.broadcasted_iota(jnp.int32, key.shape, 1))
        n_bits = int(math.ceil(math.log2(past + 1)))

        def body2(it, x):
            cand = x | (jnp.int32(1) << (n_bits - 1 - it))
            c = _count_all((key_ref[...] == t) & (pos < cand)) + ((key_new == t) & (past < cand)).astype(jnp.int32)
            return jnp.where(c < need, cand, x)

        cut = lax.fori_loop(0, n_bits, body2, jnp.zeros((1, 1), jnp.int32))
        cut_ref[...] = cut
        sel = (key > t) | ((key == t) & (pos <= cut))
        e_t = lax.broadcasted_iota(jnp.int32, (page, cols), 0)
        e_c = lax.broadcasted_iota(jnp.int32, (page, cols), 1)
        spread = jnp.where((e_c >> kv_shift) == e_t, 1.0, 0.0).astype(BF16)
        selx_ref[...] = _dot(jnp.where(sel, 1.0, 0.0).astype(BF16), spread)

    t = t_ref[...]
    cut = cut_ref[...]
    q = qb_ref[0]

    k2 = jnp.concatenate([r[0, 0].astype(BF16) for r in kp_refs], axis=0)
    v2 = jnp.concatenate([r[0, 0].astype(BF16) for r in vp_refs], axis=0)
    sel_cols = jnp.concatenate(
        [jnp.broadcast_to(selx_ref[pl.ds(p * group + k, 1), :], (n_heads, cols)) for k in range(group)], axis=1)
    r_id = lax.broadcasted_iota(jnp.int32, (n_heads, group * cols), 0)
    c_id = lax.broadcasted_iota(jnp.int32, (n_heads, group * cols), 1)
    keep = (sel_cols > 0.5) & ((c_id & (n_kv - 1)) == (r_id >> rep_shift))
    s = _dot_nt(q.astype(BF16), k2) * scale
    s = jnp.where(keep, s, -jnp.inf)
    m_old = m_ref[...]
    m_new = jnp.maximum(m_old, jnp.max(s, axis=1, keepdims=True))
    m_safe = jnp.where(m_new == -jnp.inf, 0.0, m_new)
    alpha = jnp.exp(m_old - m_safe)
    pr = jnp.exp(s - m_safe)
    l_ref[...] = alpha * l_ref[...] + jnp.sum(pr, axis=1, keepdims=True)
    acc_ref[...] = alpha * acc_ref[...] + _dot(pr.astype(BF16), v2)
    m_ref[...] = m_new

    @pl.when(p == n_pages // group - 1)
    def _():
        key_new = keyn_ref[:, 0:1]
        sel_new = (key_new > t) | ((key_new == t) & (past <= cut))
        h_id = lax.broadcasted_iota(jnp.int32, (n_heads, dh), 0) >> rep_shift
        kn = jnp.zeros((n_heads, dh), F32)
        vn = jnp.zeros((n_heads, dh), F32)
        for g in range(n_kv):
            kn = jnp.where(h_id == g, kn_ref[0, g:g + 1, :], kn)
            vn = jnp.where(h_id == g, vn_ref[0, g:g + 1, :], vn)
        s_new = jnp.sum(q.astype(F32) * kn, axis=1, keepdims=True) * scale
        s_new = jnp.where(sel_new, s_new, -jnp.inf)
        m_o = m_ref[...]
        m_n = jnp.maximum(m_o, s_new)
        m_s = jnp.where(m_n == -jnp.inf, 0.0, m_n)
        a = jnp.exp(m_o - m_s)
        p_new = jnp.exp(s_new - m_s)
        l_fin = a * l_ref[...] + p_new
        o_ref[0] = ((a * acc_ref[...] + p_new * vn) / l_fin).astype(o_ref.dtype)


def sample_dsa(page_table, layer, cache_k, cache_v, scores, score_new, qb, k_new, v_new,
               *, n_heads, n_kv, dh, topk):
    db, n_pages = page_table.shape
    cols = cache_k.shape[2]
    page = cols // n_kv
    rep = n_heads // n_kv
    assert n_kv & (n_kv - 1) == 0 and rep & (rep - 1) == 0
    pt = page_table.reshape(-1)

    group = SAMPLE_DSA_PAGES if n_pages % SAMPLE_DSA_PAGES == 0 else 1

    def pool_map(k):
        return lambda b, p, pt_ref: (layer, pt_ref[b * n_pages + p * group + k], 0, 0)

    row3 = lambda b, p, pt_ref: (b, 0, 0)
    grid_spec = pltpu.PrefetchScalarGridSpec(
        num_scalar_prefetch=1,
        grid=(db, n_pages // group),
        in_specs=[
            pl.BlockSpec((1, n_pages, page), row3),
            pl.BlockSpec((1, 1, LANES), row3),
            pl.BlockSpec((1, n_heads, dh), row3),
            *[pl.BlockSpec((1, 1, cols, dh), pool_map(k)) for k in range(group)],
            *[pl.BlockSpec((1, 1, cols, dh), pool_map(k)) for k in range(group)],
            pl.BlockSpec((1, n_kv, dh), row3),
            pl.BlockSpec((1, n_kv, dh), row3),
        ],
        out_specs=pl.BlockSpec((1, n_heads, dh), row3),
        scratch_shapes=[pltpu.VMEM((n_pages, page), jnp.int32), pltpu.VMEM((1, LANES), jnp.int32),
                        pltpu.VMEM((1, 1), jnp.int32), pltpu.VMEM((1, 1), jnp.int32),
                        pltpu.VMEM((n_pages, cols), F32),
                        pltpu.VMEM((n_heads, 1), F32), pltpu.VMEM((n_heads, 1), F32),
                        pltpu.VMEM((n_heads, dh), F32)],
    )
    out = pl.pallas_call(
        functools.partial(_sample_dsa_kernel, n_pages=n_pages, group=group, page=page, n_heads=n_heads,
                          n_kv=n_kv, dh=dh, topk=topk),
        grid_spec=grid_spec,
        out_shape=jax.ShapeDtypeStruct((db, n_heads, dh), F32),
        compiler_params=_params("parallel", "arbitrary"),
        name="sample_dsa",
    )(pt, scores, score_new, qb.reshape(db, n_heads, dh), *([cache_k] * group), *([cache_v] * group),
      k_new.reshape(db, n_kv, dh), v_new.reshape(db, n_kv, dh))
    return out.reshape(db, n_heads * dh)


def kernel(x_prompt, x_sample, cache_a_k, cache_a_v, cache_b_k, cache_b_v, cache_idx_k, page_table, w_in, lambda_q, lambda_k, subln_g, w_branch_a, w_branch_b, w_out, norm_mix_g, norm_ffn_g, w_dense_gate, w_dense_up, w_dense_down, w_router, w_exp_gate, w_exp_up, w_exp_down, norm_final_g):
    batch, seq, d_model = x_prompt.shape
    db, dec_seq, _ = x_sample.shape
    assert dec_seq == 1, "one new token per sample row"
    depth, n_pool, page, n_ha, a_width = cache_a_k.shape
    assert page == PAGE_SIZE
    dqk = a_width // 2
    dv = cache_a_v.shape[4]
    n_kv, dh = cache_b_k.shape[3:]
    d_idx = cache_idx_k.shape[3]
    n_hb = w_branch_b.shape[1] // dh
    n_in = w_in.shape[2]
    wa, wva, wqb, wkb = n_ha * 2 * dqk, n_ha * dv, n_hb * dh, n_kv * dh
    n_idx = (n_in - (2 * wa + wva + wqb + 2 * wkb + d_idx + 2 * d_model)) // (d_idx + 1)
    assert dv == 2 * dqk == LANES and dh == LANES and 2 * d_idx == LANES
    n_pages = page_table.shape[1]
    past_len = n_pages * page
    topk_p = min(TOPK_MAX, seq // 4)
    topk_s = min(TOPK_MAX, (past_len + dec_seq) // 4)
    n_exp = w_router.shape[2]

    offs = [0]
    for wdt in (wa, wa, wva, wqb, wkb, wkb, n_idx * d_idx, d_idx, n_idx, d_model, d_model):
        offs.append(offs[-1] + wdt)
    o_qa, o_ka, o_va, o_qb, o_kb, o_vb, o_qi, o_ki, o_wi, o_ga, o_gb, _ = offs

    pos_p = jnp.arange(seq, dtype=F32)
    pos_s = jnp.full((db,), float(past_len), F32)
    tabs = {}
    for name, pos in (("p", pos_p), ("s", pos_s)):
        tabs[name, "a"] = rope_table(pos, dqk)
        tabs[name, "b"] = rope_table(pos, dh)
        tabs[name, "i"] = rope_table(pos, d_idx, valid_lanes=d_idx)

    n_groups = -(-n_in // LANES)
    kind = {}
    for start, stop, name in ((o_qa, o_va, "a"), (o_qb, o_vb, "b"), (o_qi, o_ki, "a"), (o_ki, o_ki + LANES, "i")):
        assert start % LANES == 0 and stop % LANES == 0
        for g in range(start // LANES, stop // LANES):
            kind[g] = name
    identity = jnp.stack([jnp.ones((db, LANES), F32), jnp.zeros((db, LANES), F32), jnp.zeros((db, LANES), F32)])
    tab_s = jnp.concatenate([tabs["s", kind[g]][0] if g in kind else identity for g in range(n_groups)], axis=2)
    halves_s = tuple(tabs["s", kind[g]][1] if g in kind else 0 for g in range(n_groups))

    ck_b = cache_b_k.reshape(depth, n_pool, page * n_kv, dh)
    cv_b = cache_b_v.reshape(depth, n_pool, page * n_kv, dh)
    ci_t = jnp.swapaxes(cache_idx_k, 2, 3)
    w_in_t = jnp.swapaxes(w_in, 1, 2)

    xp = x_prompt.reshape(batch * seq, d_model)
    xs = x_sample.reshape(db * dec_seq, d_model)
    rows = {"p": [[] for _ in range(5)], "s": [[] for _ in range(5)]}

    for l in range(depth):
        lam_init = 0.8 - 0.6 * math.exp(-0.3 * l)
        lq, lk = lambda_q[l], lambda_k[l]
        wl = w_in_t[l].astype(BF16)
        w_kiwi = jnp.pad(wl[o_ki:o_ga], ((0, LANES - d_idx - n_idx), (0, 0)))
        mix_w = {"p": dict(pa=w_branch_a[l].astype(BF16), pb=w_branch_b[l].astype(BF16), o=w_out[l].astype(BF16)),
                 "s": dict(pa=w_branch_a[l], pb=w_branch_b[l], o=w_out[l])}

        def in_proj_prompt(x):
            xn = rmsnorm(x, norm_mix_g[l], BF16)
            ta, ha = tabs["p", "a"]
            tb, hb = tabs["p", "b"]
            ti, hi = tabs["p", "i"]
            z = {}
            z["qa"], = project(xn, wl[o_qa:o_ka], [BF16], ta, ha)
            z["ka32"], z["ka"] = project(xn, wl[o_ka:o_va], [F32, BF16], ta, ha)
            z["va32"], z["va"] = project(xn, wl[o_va:o_qb], [F32, BF16])
            z["qb"], = project(xn, wl[o_qb:o_kb], [BF16], tb, hb)
            z["kb32"], z["kb"] = project(xn, wl[o_kb:o_vb], [F32, BF16], tb, hb)
            z["vb32"], z["vb"] = project(xn, wl[o_vb:o_qi], [F32, BF16])
            z["qi"], = project(xn, wl[o_qi:o_ki], [BF16], ta, ha)
            z["kiwi"], = project(xn, w_kiwi, [F32], ti, hi)
            z["ga"], = project(xn, wl[o_ga:o_gb], [F32])
            z["gb"], = project(xn, wl[o_gb:], [F32])
            return z

        def in_proj_sample(x):
            xn = rmsnorm(x, norm_mix_g[l], F32)
            zz = sample_project(xn, w_in_t, l, tab_s, halves_s)
            z = {name: zz[:, a:b] for name, a, b in (
                ("qa", o_qa, o_ka), ("ka32", o_ka, o_va), ("va32", o_va, o_qb), ("qb", o_qb, o_kb),
                ("kb32", o_kb, o_vb), ("vb32", o_vb, o_qi), ("qi", o_qi, o_ki), ("kiwi", o_ki, o_ki + LANES),
                ("ga", o_ga, o_gb), ("gb", o_gb, n_in))}
            return z

        def mix_out(x, z, oa, ob, grp):
            w = mix_w[grp]
            u = merge_branches(oa, ob, w["pa"], w["pb"], z["ga"], z["gb"])
            return matmul_residual(u, w["o"], x)

        zp = in_proj_prompt(xp)
        oa = diff_attention_prompt(zp["qa"], zp["ka"], zp["va"].T, lq, lk, subln_g[l], batch=batch, seq=seq,
                                   n_heads=n_ha, dqk=dqk, lam_init=lam_init)
        ob = dsa_prompt(zp["qi"], zp["kiwi"], zp["qb"], zp["kb"], zp["vb"], batch=batch, seq=seq, n_idx=n_idx,
                        d_idx=d_idx, n_heads=n_hb, n_kv=n_kv, dh=dh, topk=topk_p)
        xp = mix_out(xp, zp, oa, ob, "p")

        zs = in_proj_sample(xs)
        ki_s = zs["kiwi"][:, :d_idx]
        wi_s = zs["kiwi"][:, d_idx:d_idx + n_idx]
        oa, scores, score_new = sample_attention(
            page_table, l, cache_a_k, cache_a_v, ci_t, zs["qa"], zs["qi"], wi_s, zs["ka32"], zs["va32"], ki_s,
            lq, lk, subln_g[l], n_heads=n_ha, dqk=dqk, lam_init=lam_init, n_idx=n_idx, d_idx=d_idx)
        ob = sample_dsa(page_table, l, ck_b, cv_b, scores, score_new, zs["qb"], zs["kb32"], zs["vb32"],
                        n_heads=n_hb, n_kv=n_kv, dh=dh, topk=topk_s)
        xs = mix_out(xs, zs, oa, ob, "s")

        for grp, z, bt, tt in (("p", zp, batch, seq), ("s", zs, db, dec_seq)):
            rows[grp][0].append(z["ka32"].reshape(bt, tt, n_ha, 2 * dqk))
            rows[grp][1].append(z["va32"].reshape(bt, tt, n_ha, dv))
            rows[grp][2].append(z["kb32"].reshape(bt, tt, n_kv, dh))
            rows[grp][3].append(z["vb32"].reshape(bt, tt, n_kv, dh))
            rows[grp][4].append(z["kiwi"][:, :d_idx].reshape(bt, tt, d_idx))

        i = l // 2
        if l % 2 == 0:
            hp = rmsnorm(xp, norm_ffn_g[l], BF16)
            hs = rmsnorm(xs, norm_ffn_g[l], F32)
            xp = matmul_residual(gate_up(hp, w_dense_gate[i], w_dense_up[i]), w_dense_down[i].astype(BF16), xp)
            xs = matmul_residual(gate_up(hs, w_dense_gate[i], w_dense_up[i]), w_dense_down[i], xs)
        else:
            xp, xs = moe_ffn(xp, xs, norm_ffn_g[l], w_router[i], w_exp_gate[i], w_exp_up[i], w_exp_down[i])

    y_prompt = rmsnorm(xp, norm_final_g, F32).reshape(batch, seq, d_model)
    y_sample = rmsnorm(xs, norm_final_g, F32).reshape(db, dec_seq, d_model)
    outs_p = [jnp.stack(r, axis=0) for r in rows["p"]]
    outs_s = [jnp.stack(r, axis=0) for r in rows["s"]]
    return (y_prompt, y_sample, *outs_p, *outs_s)
```

```python
import functools
import math

import jax
import jax.numpy as jnp
from jax import lax
from jax.experimental import pallas as pl
from jax.experimental.pallas import tpu as pltpu

LANES = 128
VMEM_LIMIT_BYTES = 56 * 1024 * 1024
ROPE_THETA = 500000.0
ROPE_FRAC = 4
RMS_EPS = 1e-6
TOPK_MAX = 256
PAGE_SIZE = 128
TOP_K_EXPERTS = 2
INT_MIN = -2 ** 31
LOG2_E = math.log2(math.e)
F32 = jnp.float32
BF16 = jnp.bfloat16


def _pick(n, pref, mult):
    if n <= pref:
        return n
    t = (pref // mult) * mult
    while t >= mult:
        if n % t == 0:
            return t
        t -= mult
    return n


def _params(*sem):
    return pltpu.CompilerParams(dimension_semantics=sem, vmem_limit_bytes=VMEM_LIMIT_BYTES)


def _precision(a, b):
    return lax.Precision.HIGHEST if a.dtype == F32 and b.dtype == F32 else None


def _dot(a, b):
    return jnp.dot(a, b, preferred_element_type=F32, precision=_precision(a, b))


def _dot_nt(a, b):
    return lax.dot_general(a, b, (((1,), (1,)), ((), ())), preferred_element_type=F32,
                           precision=_precision(a, b))


def _rms(x, g):
    return x * lax.rsqrt(jnp.mean(x * x, axis=-1, keepdims=True) + RMS_EPS) * g


def _rmsnorm_kernel(x_ref, g_ref, o_ref):
    o_ref[...] = _rms(x_ref[...], g_ref[...]).astype(o_ref.dtype)


def rmsnorm(x, g, out_dtype):
    m, d = x.shape
    tm = _pick(m, 512, 8)
    return pl.pallas_call(
        _rmsnorm_kernel,
        grid=(m // tm,),
        in_specs=[pl.BlockSpec((tm, d), lambda i: (i, 0)), pl.BlockSpec((1, d), lambda i: (0, 0))],
        out_specs=pl.BlockSpec((tm, d), lambda i: (i, 0)),
        out_shape=jax.ShapeDtypeStruct((m, d), out_dtype),
        compiler_params=_params("parallel"),
        name="rmsnorm",
    )(x, g.reshape(1, d))


def _proj_kernel(*refs, half, tn):
    if half:
        x_ref, w_ref, tab_ref, *o_refs = refs
    else:
        x_ref, w_ref, *o_refs = refs
    z = _dot_nt(x_ref[...], w_ref[...])
    if not half:
        for o in o_refs:
            o[...] = z.astype(o.dtype)
        return
    c, s_lo, s_hi = tab_ref[0], tab_ref[1], tab_ref[2]
    for g in range(tn // LANES):
        zg = z[:, g * LANES:(g + 1) * LANES]
        r = zg * c + pltpu.roll(zg, LANES - half, 1) * s_lo + pltpu.roll(zg, half, 1) * s_hi
        for o in o_refs:
            o[:, g * LANES:(g + 1) * LANES] = r.astype(o.dtype)


def project(xn, w, out_dtypes, tab=None, half=0, tn_pref=1024):
    m, k = xn.shape
    n = w.shape[0]
    p_rows = m if tab is None else tab.shape[1]
    assert m % p_rows == 0
    tm = _pick(p_rows, 1024, 8)
    tn = _pick(n, tn_pref, LANES)
    in_specs = [pl.BlockSpec((tm, k), lambda i, j: (i, 0)), pl.BlockSpec((tn, k), lambda i, j: (j, 0))]
    args = [xn, w]
    if tab is not None:
        nblk = p_rows // tm
        in_specs.append(pl.BlockSpec((3, tm, LANES), lambda i, j: (0, i % nblk, 0)))
        args.append(tab)
    outs = pl.pallas_call(
        functools.partial(_proj_kernel, half=half if tab is not None else 0, tn=tn),
        grid=(m // tm, n // tn),
        in_specs=in_specs,
        out_specs=[pl.BlockSpec((tm, tn), lambda i, j: (i, j)) for _ in out_dtypes],
        out_shape=[jax.ShapeDtypeStruct((m, n), dt) for dt in out_dtypes],
        compiler_params=_params("parallel", "arbitrary"),
        name="in_proj",
    )(*args)
    return outs


def _sample_proj_kernel(x_ref, w_ref, tab_ref, o_ref, *, n_in, halves):
    k = pl.program_id(0)

    @pl.when(k == 0)
    def _():
        o_ref[...] = jnp.zeros(o_ref.shape, o_ref.dtype)

    o_ref[:, :n_in] += _dot_nt(x_ref[...], w_ref[0])

    @pl.when(k == pl.num_programs(0) - 1)
    def _():
        for g, half in enumerate(halves):
            if half:
                cols = slice(g * LANES, (g + 1) * LANES)
                zg = o_ref[:, cols]
                o_ref[:, cols] = (zg * tab_ref[0, :, cols] + pltpu.roll(zg, LANES - half, 1) * tab_ref[1, :, cols]
                                  + pltpu.roll(zg, half, 1) * tab_ref[2, :, cols])


def sample_project(xn, w_in_t, layer, tab, halves):
    rows, k = xn.shape
    n_in = w_in_t.shape[1]
    width = LANES * len(halves)
    tk = _pick(k, 256, 8)
    return pl.pallas_call(
        functools.partial(_sample_proj_kernel, n_in=n_in, halves=halves),
        grid=(k // tk,),
        in_specs=[
            pl.BlockSpec((rows, tk), lambda kk: (0, kk)),
            pl.BlockSpec((1, n_in, tk), lambda kk: (layer, 0, kk)),
            pl.BlockSpec((3, rows, width), lambda kk: (0, 0, 0)),
        ],
        out_specs=pl.BlockSpec((rows, width), lambda kk: (0, 0)),
        out_shape=jax.ShapeDtypeStruct((rows, width), F32),
        compiler_params=_params("arbitrary"),
        name="sample_in_proj",
    )(xn, w_in_t, tab)


def rope_table(pos, head_dim, valid_lanes=LANES):
    rot = head_dim // ROPE_FRAC
    half = rot // 2
    inv_freq = jnp.power(ROPE_THETA, -jnp.arange(half, dtype=F32) * 2.0 / rot)
    ang = pos[:, None] * inv_freq[None, :]
    cos, sin = jnp.cos(ang), jnp.sin(ang)
    n = pos.shape[0]
    ones = jnp.ones((n, head_dim - rot), F32)
    zeros_h = jnp.zeros((n, half), F32)
    zeros_r = jnp.zeros((n, head_dim - rot), F32)
    c = jnp.concatenate([cos, cos, ones], axis=1)
    s_lo = jnp.concatenate([-sin, zeros_h, zeros_r], axis=1)
    s_hi = jnp.concatenate([zeros_h, sin, zeros_r], axis=1)
    reps = LANES // head_dim
    tabs = [jnp.tile(t, (1, reps)) for t in (c, s_lo, s_hi)]
    if valid_lanes < LANES:
        lane = jnp.arange(LANES)[None, :]
        tabs = [jnp.where(lane < valid_lanes, tabs[0], 1.0), jnp.where(lane < valid_lanes, tabs[1], 0.0),
                jnp.where(lane < valid_lanes, tabs[2], 0.0)]
    return jnp.stack(tabs, axis=0), half


def _lambda(lq_ref, lk_ref, lam_init):
    lq = lq_ref[...].astype(F32)
    lk = lk_ref[...].astype(F32)
    prod = lq * lk
    return (jnp.exp(jnp.sum(prod[0:1, :], axis=1, keepdims=True))
            - jnp.exp(jnp.sum(prod[1:2, :], axis=1, keepdims=True)) + lam_init)


def _diffattn_kernel(lq_ref, lk_ref, g_ref, q_ref, k_ref, vt_ref, o_ref, m_ref, l_ref, acc_ref,
                     *, tq, dqk, lam_init):
    i = pl.program_id(2)
    c_exp = dqk ** -0.5 * LOG2_E
    q = q_ref[...].astype(F32)
    lane = lax.broadcasted_iota(jnp.int32, q.shape, 1)
    qs = jnp.concatenate([jnp.where(lane < dqk, q, 0.0), jnp.where(lane >= dqk, q, 0.0)], axis=0)
    qs = qs.astype(q_ref.dtype)
    m_ref[...] = jnp.full(m_ref.shape, -jnp.inf, F32)
    l_ref[...] = jnp.zeros(l_ref.shape, F32)
    acc_ref[...] = jnp.zeros(acc_ref.shape, F32)

    def step(j, masked):
        start = pl.multiple_of(j * tq, tq)
        kt = k_ref[pl.ds(start, tq), :]
        vt = vt_ref[:, pl.ds(start, tq)]
        s = _dot_nt(kt, qs) * c_exp
        if masked:
            key = lax.broadcasted_iota(jnp.int32, s.shape, 0)
            qry = lax.broadcasted_iota(jnp.int32, s.shape, 1) & (tq - 1)
            s = jnp.where(key <= qry, s, -jnp.inf)
        m_old = m_ref[...]
        m_new = jnp.maximum(m_old, jnp.max(s, axis=0, keepdims=True))
        alpha = jnp.exp2(m_old - m_new)
        p = jnp.exp2(s - m_new)
        l_ref[...] = alpha * l_ref[...] + jnp.sum(p, axis=0, keepdims=True)
        acc_ref[...] = alpha * acc_ref[...] + _dot(vt, p.astype(vt.dtype))
        m_ref[...] = m_new

    def body(j, carry):
        step(j, False)
        return carry

    lax.fori_loop(0, i, body, 0)
    step(i, True)

    lam = _lambda(lq_ref, lk_ref, lam_init)
    o_all = acc_ref[...] / l_ref[...]
    o = o_all[:, :tq] - lam * o_all[:, tq:]
    inv = lax.rsqrt(jnp.mean(o * o, axis=0, keepdims=True) + RMS_EPS)
    y = o * inv * g_ref[...] * (1.0 - lam_init)
    o_ref[...] = y.T.astype(o_ref.dtype)


def diff_attention_prompt(q, k, v_t, lq, lk, g, *, batch, seq, n_heads, dqk, lam_init):
    m, width = q.shape
    hd = width // n_heads
    tq = _pick(seq, 512, LANES)
    assert tq & (tq - 1) == 0
    nq = seq // tq
    return pl.pallas_call(
        functools.partial(_diffattn_kernel, tq=tq, dqk=dqk, lam_init=lam_init),
        grid=(batch, n_heads, nq),
        in_specs=[
            pl.BlockSpec(lq.shape, lambda b, h, i: (0, 0)),
            pl.BlockSpec(lk.shape, lambda b, h, i: (0, 0)),
            pl.BlockSpec((hd, 1), lambda b, h, i: (0, 0)),
            pl.BlockSpec((tq, hd), lambda b, h, i: (b * nq + i, h)),
            pl.BlockSpec((seq, hd), lambda b, h, i: (b, h)),
            pl.BlockSpec((hd, seq), lambda b, h, i: (h, b)),
        ],
        out_specs=pl.BlockSpec((tq, hd), lambda b, h, i: (b * nq + i, h)),
        out_shape=jax.ShapeDtypeStruct((m, width), BF16),
        scratch_shapes=[pltpu.VMEM((1, 2 * tq), F32), pltpu.VMEM((1, 2 * tq), F32), pltpu.VMEM((hd, 2 * tq), F32)],
        compiler_params=_params("parallel", "parallel", "arbitrary"),
        name="diff_attn_prompt",
    )(lq, lk, g.reshape(hd, 1), q, k, v_t)


def _order_key(score):
    score = jnp.where(score == 0.0, 0.0, score)
    bits = lax.bitcast_convert_type(score, jnp.int32)
    return jnp.where(bits < 0, bits ^ jnp.int32(0x7FFFFFFF), bits)


def _count_rows(mask):
    return jnp.sum(mask.astype(jnp.int32), axis=1, keepdims=True)


def _kth_largest_key(load_keys, shape, k, count_fn):
    t0 = jnp.full(shape, INT_MIN, jnp.int32)
    zero = jnp.zeros(shape, jnp.int32)
    t0 = jnp.where(count_fn(load_keys() >= zero) >= k, zero, t0)

    def body(it, t):
        cand = t | (jnp.int32(1) << (30 - it))
        return jnp.where(count_fn(load_keys() >= cand) >= k, cand, t)

    return lax.fori_loop(0, 31, body, t0)


def _tie_cut(load_eq_pos, shape, need, n_bits, count_fn):
    def body(it, x):
        cand = x | (jnp.int32(1) << (n_bits - 1 - it))
        eq, pos = load_eq_pos()
        return jnp.where(count_fn(eq & (pos < cand)) < need, cand, x)

    return lax.fori_loop(0, n_bits, body, jnp.zeros(shape, jnp.int32))


def _dsa_prompt_kernel(qi_ref, kiwi_k_ref, kiwi_q_ref, qb_ref, kb_ref, vb_ref, o_ref,
                       kdup_ref, wb_ref, score_ref, key_ref, bias_ref, cut_ref,
                       *, tq, seq, extents, n_idx, d_idx, n_heads, n_kv, dh, topk):
    i = pl.program_id(1)

    @pl.when(i == 0)
    def _():
        kf = kiwi_k_ref[...]
        lane = lax.broadcasted_iota(jnp.int32, kf.shape, 1)
        klo = jnp.where(lane < d_idx, kf, 0.0)
        kdup_ref[...] = (klo + pltpu.roll(klo, d_idx, 1)).astype(kdup_ref.dtype)

    wscale = n_idx ** -0.5 * d_idx ** -0.5
    c_exp = dh ** -0.5 * LOG2_E
    rep_shift = (n_heads // n_kv).bit_length() - 1

    w_all = kiwi_q_ref[...] * wscale
    for h in range(n_idx):
        wb_ref[h] = jnp.broadcast_to(w_all[:, d_idx + h:d_idx + h + 1], (tq, LANES))

    def window(j, width):
        return pl.ds(pl.multiple_of(j * width, width), width)

    def run(ncol):
        score_ref[:, :ncol] = jnp.zeros((tq, ncol), F32)

        def head_pairs(t, carry):
            part = None
            for r in range(IDX_PAIRS_PER_TRIP):
                j = IDX_PAIRS_PER_TRIP * t + r
                grp = qi_ref[:, window(j, LANES)].astype(F32)
                lane = lax.broadcasted_iota(jnp.int32, grp.shape, 1)
                for half in range(2):
                    keep = (lane < d_idx) if half == 0 else (lane >= d_idx)
                    qh = jnp.where(keep, grp, 0.0).astype(qi_ref.dtype)
                    w = wb_ref[2 * j + half][:, 0:1]
                    term = jnp.maximum(_dot_nt(qh, kdup_ref[:ncol, :]), 0.0) * w
                    part = term if part is None else part + term
            score_ref[:, :ncol] += part
            return carry

        lax.fori_loop(0, n_idx // (2 * IDX_PAIRS_PER_TRIP), head_pairs, 0)

        row = i * tq + lax.broadcasted_iota(jnp.int32, (tq, ncol), 0)
        col = lax.broadcasted_iota(jnp.int32, (tq, ncol), 1)
        causal = col <= row
        key_ref[:, :ncol] = jnp.where(causal, _order_key(score_ref[:, :ncol]), INT_MIN)

        t = _kth_largest_key(lambda: key_ref[:, :ncol], (tq, 1), topk, _count_rows)
        key = key_ref[:, :ncol]
        eq = key == t
        need = topk - _count_rows(key > t)
        tie = (_count_rows(eq) > need) & (t > INT_MIN)
        cut_ref[...] = jnp.full((tq, 1), ncol, jnp.int32)

        @pl.when(jnp.max(tie.astype(jnp.int32)) > 0)
        def _():
            def load():
                return key_ref[:, :ncol] == t, lax.broadcasted_iota(jnp.int32, (tq, ncol), 1)
            cut_ref[...] = _tie_cut(load, (tq, 1), need, int(math.log2(seq)), _count_rows)

        sel = ((key > t) | (eq & (col <= cut_ref[...]))) & causal
        bias_ref[:, :ncol] = jnp.where(sel, 0.0, -jnp.inf)

        def head(h):
            g = h >> rep_shift
            s = _dot_nt(qb_ref[:, window(h, dh)], kb_ref[:ncol, window(g, dh)]) * c_exp + bias_ref[:, :ncol]
            p = jnp.exp2(s - jnp.max(s, axis=1, keepdims=True))
            l = jnp.sum(p, axis=1, keepdims=True)
            o = _dot(p.astype(vb_ref.dtype), vb_ref[:ncol, window(g, dh)]) / l
            o_ref[:, window(h, dh)] = o.astype(o_ref.dtype)

        def head_group(t, carry):
            for r in range(DSA_HEADS_PER_TRIP):
                head(DSA_HEADS_PER_TRIP * t + r)
            return carry

        lax.fori_loop(0, n_heads // DSA_HEADS_PER_TRIP, head_group, 0)

    lo = 0
    for ncol in extents:
        @pl.when(((i + 1) * tq > lo) & ((i + 1) * tq <= ncol))
        def _(ncol=ncol):
            run(ncol)
        lo = ncol


def dsa_prompt(qi, kiwi, qb, kb, vb, *, batch, seq, n_idx, d_idx, n_heads, n_kv, dh, topk):
    m = qi.shape[0]
    rep = n_heads // n_kv
    assert 2 * d_idx == LANES and dh == LANES and seq & (seq - 1) == 0 and rep & (rep - 1) == 0 and n_idx % 2 == 0
    tq = _pick(seq, 256, LANES)
    nq = seq // tq
    n_ext = min(4, nq)
    extents = tuple(seq * (c + 1) // n_ext for c in range(n_ext))
    return pl.pallas_call(
        functools.partial(_dsa_prompt_kernel, tq=tq, seq=seq, extents=extents, n_idx=n_idx, d_idx=d_idx,
                          n_heads=n_heads,
                          n_kv=n_kv, dh=dh, topk=topk),
        grid=(batch, nq),
        in_specs=[
            pl.BlockSpec((tq, qi.shape[1]), lambda b, i: (b * nq + i, 0)),
            pl.BlockSpec((seq, LANES), lambda b, i: (b, 0)),
            pl.BlockSpec((tq, LANES), lambda b, i: (b * nq + i, 0)),
            pl.BlockSpec((tq, qb.shape[1]), lambda b, i: (b * nq + i, 0)),
            pl.BlockSpec((seq, kb.shape[1]), lambda b, i: (b, 0)),
            pl.BlockSpec((seq, vb.shape[1]), lambda b, i: (b, 0)),
        ],
        out_specs=pl.BlockSpec((tq, qb.shape[1]), lambda b, i: (b * nq + i, 0)),
        out_shape=jax.ShapeDtypeStruct((m, qb.shape[1]), BF16),
        scratch_shapes=[pltpu.VMEM((seq, LANES), BF16), pltpu.VMEM((n_idx, tq, LANES), F32),
                        pltpu.VMEM((tq, seq), F32), pltpu.VMEM((tq, seq), jnp.int32),
                        pltpu.VMEM((tq, seq), F32), pltpu.VMEM((tq, 1), jnp.int32)],
        compiler_params=_params("parallel", "arbitrary"),
        name="dsa_prompt",
    )(qi, kiwi, kiwi, qb, kb, vb)


def _merge_kernel(oa_ref, ob_ref, wa_ref, wb_ref, ga_ref, gb_ref, o_ref):
    ya = _dot(oa_ref[...], wa_ref[...])
    yb = _dot(ob_ref[...], wb_ref[...])
    u = jax.nn.sigmoid(ga_ref[...]) * ya + jax.nn.sigmoid(gb_ref[...]) * yb
    o_ref[...] = u.astype(o_ref.dtype)


def merge_branches(oa, ob, w_pa, w_pb, ga, gb):
    m, ka = oa.shape
    kb = ob.shape[1]
    n = w_pa.shape[1]
    tm = _pick(m, 1024, 8)
    tn = _pick(n, 512, LANES)
    return pl.pallas_call(
        _merge_kernel,
        grid=(m // tm, n // tn),
        in_specs=[
            pl.BlockSpec((tm, ka), lambda i, j: (i, 0)),
            pl.BlockSpec((tm, kb), lambda i, j: (i, 0)),
            pl.BlockSpec((ka, tn), lambda i, j: (0, j)),
            pl.BlockSpec((kb, tn), lambda i, j: (0, j)),
            pl.BlockSpec((tm, tn), lambda i, j: (i, j)),
            pl.BlockSpec((tm, tn), lambda i, j: (i, j)),
        ],
        out_specs=pl.BlockSpec((tm, tn), lambda i, j: (i, j)),
        out_shape=jax.ShapeDtypeStruct((m, n), oa.dtype),
        compiler_params=_params("parallel", "arbitrary"),
        name="merge_branches",
    )(oa, ob, w_pa, w_pb, ga, gb)


def _matmul_res_kernel(a_ref, w_ref, r_ref, o_ref):
    o_ref[...] = r_ref[...] + _dot(a_ref[...], w_ref[...])


def matmul_residual(a, w, res):
    m, k = a.shape
    n = w.shape[1]
    tm = _pick(m, 1024, 8)
    tn = _pick(n, 512, LANES)
    return pl.pallas_call(
        _matmul_res_kernel,
        grid=(m // tm, n // tn),
        in_specs=[
            pl.BlockSpec((tm, k), lambda i, j: (i, 0)),
            pl.BlockSpec((k, tn), lambda i, j: (0, j)),
            pl.BlockSpec((tm, tn), lambda i, j: (i, j)),
        ],
        out_specs=pl.BlockSpec((tm, tn), lambda i, j: (i, j)),
        out_shape=jax.ShapeDtypeStruct((m, n), F32),
        compiler_params=_params("parallel", "arbitrary"),
        name="matmul_residual",
    )(a, w, res)


def _gateup_kernel(x_ref, wg_ref, wu_ref, o_ref, *w_cast):
    x = x_ref[...]
    if w_cast:
        wg_c, wu_c = w_cast

        @pl.when(pl.program_id(1) == 0)
        def _():
            wg_c[...] = wg_ref[...].astype(wg_c.dtype)
            wu_c[...] = wu_ref[...].astype(wu_c.dtype)

        wg_ref, wu_ref = wg_c, wu_c
    gate = _dot(x, wg_ref[...])
    up = _dot(x, wu_ref[...])
    o_ref[...] = (jax.nn.silu(gate) * up).astype(o_ref.dtype)


def gate_up(xn, wg, wu):
    m, k = xn.shape
    n = wg.shape[1]
    tm = _pick(m, 1024, 8)
    tn = _pick(n, 512, LANES)
    scratch = [] if wg.dtype == xn.dtype else [pltpu.VMEM((k, tn), xn.dtype), pltpu.VMEM((k, tn), xn.dtype)]
    return pl.pallas_call(
        _gateup_kernel,
        grid=(n // tn, m // tm),
        in_specs=[
            pl.BlockSpec((tm, k), lambda j, i: (i, 0)),
            pl.BlockSpec((k, tn), lambda j, i: (0, j)),
            pl.BlockSpec((k, tn), lambda j, i: (0, j)),
        ],
        out_specs=pl.BlockSpec((tm, tn), lambda j, i: (i, j)),
        out_shape=jax.ShapeDtypeStruct((m, n), xn.dtype),
        scratch_shapes=scratch,
        compiler_params=_params("parallel", "arbitrary"),
        name="ffn_gate_up",
    )(xn, wg, wu)


def _router_kernel(x_ref, g_ref, wr_ref, base_ref, hn_ref, gate_ref, route_ref, cnt_ref, run_ref, *, n_exp):
    @pl.when(pl.program_id(0) == 0)
    def _():
        run_ref[...] = base_ref[...]

    hn = _rms(x_ref[...], g_ref[...])
    hn_ref[...] = hn
    tm = hn.shape[0]
    lane = lax.broadcasted_iota(jnp.int32, (tm, LANES), 1)
    logits = jnp.full((tm, LANES), -jnp.inf, F32)
    for e in range(n_exp):
        le = jnp.sum(hn * wr_ref[e:e + 1, :], axis=1, keepdims=True)
        logits = jnp.where(lane == e, le, logits)
    v1 = jnp.max(logits, axis=1, keepdims=True)
    i1 = jnp.min(jnp.where(logits == v1, lane, LANES), axis=1, keepdims=True)
    rest = jnp.where(lane == i1, -jnp.inf, logits)
    v2 = jnp.max(rest, axis=1, keepdims=True)
    i2 = jnp.min(jnp.where(rest == v2, lane, LANES), axis=1, keepdims=True)
    e2 = jnp.exp(v2 - v1)
    den = 1.0 + e2
    gate_ref[...] = jnp.where(lane == 0, 1.0 / den, jnp.where(lane == 1, e2 / den, 0.0))

    onehot = jnp.where((lane == i1) | (lane == i2), 1.0, 0.0)
    r = lax.broadcasted_iota(jnp.int32, (tm, tm), 0)
    c = lax.broadcasted_iota(jnp.int32, (tm, tm), 1)
    before = _dot(jnp.where(c < r, 1.0, 0.0).astype(BF16), onehot.astype(BF16)) + run_ref[...]
    r1 = jnp.sum(jnp.where(lane == i1, before, 0.0), axis=1, keepdims=True).astype(jnp.int32)
    r2 = jnp.sum(jnp.where(lane == i2, before, 0.0), axis=1, keepdims=True).astype(jnp.int32)
    route_ref[...] = jnp.where(lane == 0, i1, jnp.where(lane == 1, i2, jnp.where(lane == 2, r1,
                               jnp.where(lane == 3, r2, 0))))
    run_ref[...] = run_ref[...] + jnp.sum(onehot, axis=0, keepdims=True)
    cnt_ref[...] = run_ref[...]


def moe_router(x, g, w_router, base_counts):
    m, d = x.shape
    n_exp = w_router.shape[1]
    assert TOP_K_EXPERTS == 2 and n_exp <= LANES
    tm = _pick(m, 256, 8)
    row = lambda i: (i, 0)
    fixed = lambda i: (0, 0)
    return pl.pallas_call(
        functools.partial(_router_kernel, n_exp=n_exp),
        grid=(m // tm,),
        in_specs=[
            pl.BlockSpec((tm, d), row),
            pl.BlockSpec((1, d), fixed),
            pl.BlockSpec((n_exp, d), fixed),
            pl.BlockSpec((1, LANES), fixed),
        ],
        out_specs=[pl.BlockSpec((tm, d), row), pl.BlockSpec((tm, LANES), row), pl.BlockSpec((tm, LANES), row),
                   pl.BlockSpec((1, LANES), fixed)],
        out_shape=[jax.ShapeDtypeStruct((m, d), F32), jax.ShapeDtypeStruct((m, LANES), F32),
                   jax.ShapeDtypeStruct((m, LANES), jnp.int32), jax.ShapeDtypeStruct((1, LANES), F32)],
        scratch_shapes=[pltpu.VMEM((1, LANES), F32)],
        compiler_params=_params("arbitrary"),
        name="moe_router",
    )(x, g.reshape(1, d), w_router.T, base_counts)


def _row_copy(src_ref, src_row, dst_ref, dst_row, sem):
    return pltpu.make_async_copy(src_ref.at[pl.ds(src_row, 1)], dst_ref.at[pl.ds(dst_row, 1)], sem)


def _dispatch_kernel(pos1_ref, pos2_ref, x_ref, xs_in_ref, xs_ref, sem, *, tm):
    del xs_in_ref
    base = pl.program_id(0) * tm

    def start(r, carry):
        _row_copy(x_ref, r, xs_ref, pos1_ref[base + r], sem).start(priority=0)
        _row_copy(x_ref, r, xs_ref, pos2_ref[base + r], sem).start(priority=1)
        return carry

    def wait(r, carry):
        _row_copy(x_ref, 0, xs_ref, 0, sem).wait()
        _row_copy(x_ref, 0, xs_ref, 0, sem).wait()
        return carry

    lax.fori_loop(0, tm, start, 0)
    lax.fori_loop(0, tm, wait, 0)


def moe_dispatch(hn, pos1, pos2, x_sorted):
    m, d = hn.shape
    tm = _pick(m, 256, 8)
    grid_spec = pltpu.PrefetchScalarGridSpec(
        num_scalar_prefetch=2,
        grid=(m // tm,),
        in_specs=[pl.BlockSpec((tm, d), lambda i, p1, p2: (i, 0)), pl.BlockSpec(memory_space=pl.ANY)],
        out_specs=pl.BlockSpec(memory_space=pl.ANY),
        scratch_shapes=[pltpu.SemaphoreType.DMA(())],
    )
    return pl.pallas_call(
        functools.partial(_dispatch_kernel, tm=tm),
        grid_spec=grid_spec,
        out_shape=jax.ShapeDtypeStruct(x_sorted.shape, x_sorted.dtype),
        input_output_aliases={3: 0},
        compiler_params=_params("arbitrary"),
        name="moe_dispatch",
    )(pos1, pos2, hn, x_sorted)


def _combine_kernel(pos1_ref, pos2_ref, x_ref, gate_ref, ys_ref, o_ref, buf_ref, sem, *, tm):
    base = pl.program_id(0) * tm

    def start(r, carry):
        _row_copy(ys_ref, pos1_ref[base + r], buf_ref.at[0], r, sem).start(priority=0)
        _row_copy(ys_ref, pos2_ref[base + r], buf_ref.at[1], r, sem).start(priority=1)
        return carry

    def wait(r, carry):
        _row_copy(ys_ref, 0, buf_ref.at[0], 0, sem).wait()
        _row_copy(ys_ref, 0, buf_ref.at[1], 0, sem).wait()
        return carry

    lax.fori_loop(0, tm, start, 0)
    lax.fori_loop(0, tm, wait, 0)
    gate = gate_ref[...]
    o_ref[...] = x_ref[...] + gate[:, 0:1] * buf_ref[0] + gate[:, 1:2] * buf_ref[1]


def moe_combine(x, gates, pos1, pos2, y_sorted):
    m, d = x.shape
    tm = _pick(m, 256, 8)
    row = lambda i, p1, p2: (i, 0)
    grid_spec = pltpu.PrefetchScalarGridSpec(
        num_scalar_prefetch=2,
        grid=(m // tm,),
        in_specs=[pl.BlockSpec((tm, d), row), pl.BlockSpec((tm, LANES), row), pl.BlockSpec(memory_space=pl.ANY)],
        out_specs=pl.BlockSpec((tm, d), row),
        scratch_shapes=[pltpu.VMEM((2, tm, d), F32), pltpu.SemaphoreType.DMA(())],
    )
    return pl.pallas_call(
        functools.partial(_combine_kernel, tm=tm),
        grid_spec=grid_spec,
        out_shape=jax.ShapeDtypeStruct((m, d), F32),
        compiler_params=_params("arbitrary"),
        name="moe_combine",
    )(pos1, pos2, x, gates, y_sorted)


def _expert_gateup_kernel(te_ref, nu_ref, x_ref, wg_ref, wu_ref, o_ref, wg_c, wu_c):
    i = pl.program_id(1)
    used = i < nu_ref[0]
    fresh = (i == 0) | (te_ref[i] != te_ref[jnp.maximum(i - 1, 0)])

    @pl.when(used & fresh)
    def _():
        wg_c[...] = wg_ref[0].astype(wg_c.dtype)
        wu_c[...] = wu_ref[0].astype(wu_c.dtype)

    @pl.when(used)
    def _():
        x = x_ref[...]
        gate = _dot(x, wg_c[...])
        up = _dot(x, wu_c[...])
        o_ref[...] = (jax.nn.silu(gate) * up).astype(o_ref.dtype)

    @pl.when(jnp.logical_not(used))
    def _():
        o_ref[...] = jnp.zeros(o_ref.shape, o_ref.dtype)


def expert_gate_up(x_sorted, wg, wu, tile_expert, n_used, tm):
    p_rows, d = x_sorted.shape
    ff = wg.shape[2]
    tn = _pick(ff, 1024, LANES)

    def w_map(j, i, te, nu):
        return (te[i], 0, j)

    grid_spec = pltpu.PrefetchScalarGridSpec(
        num_scalar_prefetch=2,
        grid=(ff // tn, p_rows // tm),
        in_specs=[
            pl.BlockSpec((tm, d), lambda j, i, te, nu: (jnp.minimum(i, nu[0] - 1), 0)),
            pl.BlockSpec((1, d, tn), w_map),
            pl.BlockSpec((1, d, tn), w_map),
        ],
        out_specs=pl.BlockSpec((tm, tn), lambda j, i, te, nu: (i, j)),
        scratch_shapes=[pltpu.VMEM((d, tn), BF16), pltpu.VMEM((d, tn), BF16)],
    )
    return pl.pallas_call(
        _expert_gateup_kernel,
        grid_spec=grid_spec,
        out_shape=jax.ShapeDtypeStruct((p_rows, ff), BF16),
        compiler_params=_params("parallel", "arbitrary"),
        name="expert_gate_up",
    )(tile_expert, n_used, x_sorted, wg, wu)


def _expert_down_kernel(te_ref, nu_ref, h_ref, wd_ref, o_ref):
    @pl.when(pl.program_id(0) < nu_ref[0])
    def _():
        o_ref[...] = _dot(h_ref[...], wd_ref[0])

    @pl.when(pl.program_id(0) >= nu_ref[0])
    def _():
        o_ref[...] = jnp.zeros(o_ref.shape, o_ref.dtype)


def expert_down(h_sorted, wd, tile_expert, n_used, tm):
    p_rows, ff = h_sorted.shape
    d = wd.shape[2]
    tn = _pick(d, 512, LANES)
    nj = d // tn
    grid_spec = pltpu.PrefetchScalarGridSpec(
        num_scalar_prefetch=2,
        grid=(p_rows // tm, nj),
        in_specs=[
            pl.BlockSpec((tm, ff), lambda i, j, te, nu: (jnp.minimum(i, nu[0] - 1), 0)),
            pl.BlockSpec((1, ff, tn), lambda i, j, te, nu: (te[i], 0, jnp.where(i < nu[0], j, nj - 1))),
        ],
        out_specs=pl.BlockSpec((tm, tn), lambda i, j, te, nu: (i, j)),
    )
    return pl.pallas_call(
        _expert_down_kernel,
        grid_spec=grid_spec,
        out_shape=jax.ShapeDtypeStruct((p_rows, d), F32),
        compiler_params=_params("parallel", "arbitrary"),
        name="expert_down",
    )(tile_expert, n_used, h_sorted, wd)


MOE_ROW_TILE = 512
IDX_PAIRS_PER_TRIP = 2
DSA_HEADS_PER_TRIP = 4
SAMPLE_ATTN_PAGES = 8
SAMPLE_DSA_PAGES = 16


def moe_ffn(xp, xs, g, w_router, wg, wu, wd):
    n_exp = w_router.shape[1]
    d = xp.shape[1]
    tm = MOE_ROW_TILE
    zeros = jnp.zeros((1, LANES), F32)
    hn_p, gate_p, route_p, cnt_p = moe_router(xp, g, w_router, zeros)
    hn_s, gate_s, route_s, cnt = moe_router(xs, g, w_router, cnt_p)

    counts = cnt[0, :n_exp].astype(jnp.int32)
    sizes = (counts + tm - 1) // tm * tm
    ends = jnp.cumsum(sizes)
    starts = ends - sizes
    n_assign = TOP_K_EXPERTS * (xp.shape[0] + xs.shape[0])
    n_tiles = (n_assign + n_exp * (tm - 1) + tm - 1) // tm
    tile_expert = jnp.minimum(jnp.sum(jnp.arange(n_tiles)[:, None] * tm >= ends[None, :], axis=1), n_exp - 1)
    tile_expert = tile_expert.astype(jnp.int32)
    n_used = (ends[-1:] // tm).astype(jnp.int32)

    def slots(route):
        return (jnp.take(starts, route[:, 0]) + route[:, 2], jnp.take(starts, route[:, 1]) + route[:, 3])

    p1_p, p2_p = slots(route_p)
    p1_s, p2_s = slots(route_s)
    x_sorted = jnp.zeros((n_tiles * tm, d), F32)
    x_sorted = moe_dispatch(hn_p, p1_p, p2_p, x_sorted)
    x_sorted = moe_dispatch(hn_s, p1_s, p2_s, x_sorted)
    h_sorted = expert_gate_up(x_sorted.astype(BF16), wg, wu, tile_expert, n_used, tm)
    y_sorted = expert_down(h_sorted, wd.astype(BF16), tile_expert, n_used, tm)
    return (moe_combine(xp, gate_p, p1_p, p2_p, y_sorted), moe_combine(xs, gate_s, p1_s, p2_s, y_sorted))


def _sample_attn_kernel(pt_ref, lq_ref, lk_ref, g_ref, qa_ref, qi_ref, wi_ref, *rest,
                        n_steps, group, n_heads, dqk, lam_init, n_idx, d_idx):
    kp_refs, vp_refs, ip_refs = rest[:group], rest[group:2 * group], rest[2 * group:3 * group]
    kn_ref, vn_ref, in_ref, o_ref, sc_ref, scn_ref, m_ref, l_ref, acc_ref = rest[3 * group:]
    p = pl.program_id(1)
    hd = 2 * dqk
    page = kp_refs[0].shape[2]
    cols = page * n_heads
    scale = dqk ** -0.5

    q = qa_ref[0].astype(F32)
    lane = lax.broadcasted_iota(jnp.int32, q.shape, 1)
    qs = jnp.concatenate([jnp.where(lane < dqk, q, 0.0), jnp.where(lane >= dqk, q, 0.0)], axis=0)

    @pl.when(p == 0)
    def _():
        m_ref[...] = jnp.full(m_ref.shape, -jnp.inf, F32)
        l_ref[...] = jnp.zeros(l_ref.shape, F32)
        acc_ref[...] = jnp.zeros(acc_ref.shape, F32)

    k2 = jnp.concatenate([r[0, 0].reshape(cols, hd).astype(BF16) for r in kp_refs], axis=0)
    v2 = jnp.concatenate([r[0, 0].reshape(cols, hd).astype(BF16) for r in vp_refs], axis=0)
    s = _dot_nt(qs.astype(BF16), k2) * scale
    r_id = lax.broadcasted_iota(jnp.int32, s.shape, 0) & (n_heads - 1)
    c_id = lax.broadcasted_iota(jnp.int32, s.shape, 1) & (n_heads - 1)
    s = jnp.where(r_id == c_id, s, -jnp.inf)
    m_old = m_ref[...]
    m_new = jnp.maximum(m_old, jnp.max(s, axis=1, keepdims=True))
    alpha = jnp.exp(m_old - m_new)
    pr = jnp.exp(s - m_new)
    l_ref[...] = alpha * l_ref[...] + jnp.sum(pr, axis=1, keepdims=True)
    acc_ref[...] = alpha * acc_ref[...] + _dot(pr.astype(BF16), v2)
    m_ref[...] = m_new

    wscale = n_idx ** -0.5 * d_idx ** -0.5
    qi = qi_ref[0]
    w = wi_ref[0] * wscale
    for k, ip_ref in enumerate(ip_refs):
        si = jnp.maximum(_dot(qi, ip_ref[0, 0]), 0.0) * w
        sc_ref[0, k:k + 1, :] = jnp.sum(si, axis=0, keepdims=True)

    @pl.when(p == n_steps - 1)
    def _():
        kn = jnp.concatenate([kn_ref[0], kn_ref[0]], axis=0)
        vn = jnp.concatenate([vn_ref[0], vn_ref[0]], axis=0)
        s_new = jnp.sum(qs * kn, axis=1, keepdims=True) * scale
        m_o = m_ref[...]
        m_n = jnp.maximum(m_o, s_new)
        a = jnp.exp(m_o - m_n)
        p_new = jnp.exp(s_new - m_n)
        l_fin = a * l_ref[...] + p_new
        o_all = (a * acc_ref[...] + p_new * vn) / l_fin
        lam = _lambda(lq_ref, lk_ref, lam_init)
        o = o_all[:n_heads] - lam * o_all[n_heads:]
        o_ref[0] = (_rms(o, g_ref[...]) * (1.0 - lam_init)).astype(o_ref.dtype)
        kin = in_ref[0]
        s_in = jnp.maximum(jnp.sum(qi.astype(F32) * kin, axis=1, keepdims=True), 0.0) * w
        scn_ref[0] = jnp.broadcast_to(jnp.sum(s_in, axis=0, keepdims=True), (1, LANES))


def sample_attention(page_table, layer, cache_k, cache_v, cache_i_t, qa, qi, wi, k_new, v_new, i_new, lq, lk, g,
                     *, n_heads, dqk, lam_init, n_idx, d_idx):
    db, n_pages = page_table.shape
    width = qa.shape[1]
    page = cache_k.shape[2]
    hd = 2 * dqk
    rows = 2 * n_heads
    assert n_heads & (n_heads - 1) == 0
    pt = page_table.reshape(-1)
    group = SAMPLE_ATTN_PAGES if n_pages % SAMPLE_ATTN_PAGES == 0 else 1
    n_steps = n_pages // group

    def pool5(k):
        return lambda b, p, pt_ref: (layer, pt_ref[b * n_pages + p * group + k], 0, 0, 0)

    def pool4(k):
        return lambda b, p, pt_ref: (layer, pt_ref[b * n_pages + p * group + k], 0, 0)

    row3 = lambda b, p, pt_ref: (b, 0, 0)
    const2 = lambda b, p, pt_ref: (0, 0)
    grid_spec = pltpu.PrefetchScalarGridSpec(
        num_scalar_prefetch=1,
        grid=(db, n_steps),
        in_specs=[
            pl.BlockSpec(lq.shape, const2),
            pl.BlockSpec(lk.shape, const2),
            pl.BlockSpec((1, hd), const2),
            pl.BlockSpec((1, n_heads, hd), row3),
            pl.BlockSpec((1, n_idx, d_idx), row3),
            pl.BlockSpec((1, n_idx, 1), row3),
            *[pl.BlockSpec((1, 1, page, n_heads, hd), pool5(k)) for k in range(group)],
            *[pl.BlockSpec((1, 1, page, n_heads, hd), pool5(k)) for k in range(group)],
            *[pl.BlockSpec((1, 1, d_idx, page), pool4(k)) for k in range(group)],
            pl.BlockSpec((1, n_heads, hd), row3),
            pl.BlockSpec((1, n_heads, hd), row3),
            pl.BlockSpec((1, 1, d_idx), row3),
        ],
        out_specs=[
            pl.BlockSpec((1, n_heads, hd), row3),
            pl.BlockSpec((1, group, page), lambda b, p, pt_ref: (b * n_steps + p, 0, 0)),
            pl.BlockSpec((1, 1, LANES), row3),
        ],
        scratch_shapes=[pltpu.VMEM((rows, 1), F32), pltpu.VMEM((rows, 1), F32), pltpu.VMEM((rows, hd), F32)],
    )
    out, scores, score_new = pl.pallas_call(
        functools.partial(_sample_attn_kernel, n_steps=n_steps, group=group, n_heads=n_heads, dqk=dqk,
                          lam_init=lam_init, n_idx=n_idx, d_idx=d_idx),
        grid_spec=grid_spec,
        out_shape=[jax.ShapeDtypeStruct((db, n_heads, hd), F32),
                   jax.ShapeDtypeStruct((db * n_steps, group, page), F32),
                   jax.ShapeDtypeStruct((db, 1, LANES), F32)],
        compiler_params=_params("parallel", "arbitrary"),
        name="sample_attention",
    )(pt, lq, lk, g.reshape(1, hd), qa.reshape(db, n_heads, hd), qi.reshape(db, n_idx, d_idx),
      wi.reshape(db, n_idx, 1), *([cache_k] * group), *([cache_v] * group), *([cache_i_t] * group),
      k_new.reshape(db, n_heads, hd), v_new.reshape(db, n_heads, hd), i_new.reshape(db, 1, d_idx))
    return out.reshape(db, width), scores.reshape(db, n_pages, page), score_new


def _count_all(mask):
    c = jnp.sum(mask.astype(jnp.int32), axis=1, keepdims=True)
    return jnp.sum(c, axis=0, keepdims=True)


def _sample_dsa_kernel(pt_ref, sc_ref, scn_ref, qb_ref, *rest, n_pages, group, page, n_heads, n_kv, dh, topk):
    kp_refs, vp_refs = rest[:group], rest[group:2 * group]
    kn_ref, vn_ref, o_ref, key_ref, keyn_ref, t_ref, cut_ref, selx_ref, m_ref, l_ref, acc_ref = rest[2 * group:]
    p = pl.program_id(1)
    rep = n_heads // n_kv
    scale = dh ** -0.5
    past = n_pages * page
    cols = page * n_kv
    kv_shift = n_kv.bit_length() - 1
    rep_shift = rep.bit_length() - 1

    @pl.when(p == 0)
    def _():
        m_ref[...] = jnp.full(m_ref.shape, -jnp.inf, F32)
        l_ref[...] = jnp.zeros(l_ref.shape, F32)
        acc_ref[...] = jnp.zeros(acc_ref.shape, F32)
        key_ref[...] = _order_key(sc_ref[0])
        keyn_ref[...] = _order_key(scn_ref[0])
        key_new = keyn_ref[:, 0:1]

        def cnt(t):
            return _count_all(key_ref[...] >= t) + (key_new >= t).astype(jnp.int32)

        t0 = jnp.full((1, 1), INT_MIN, jnp.int32)
        zero = jnp.zeros((1, 1), jnp.int32)
        t0 = jnp.where(cnt(zero) >= topk, zero, t0)

        def body(it, t):
            cand = t | (jnp.int32(1) << (30 - it))
            return jnp.where(cnt(cand) >= topk, cand, t)

        t = lax.fori_loop(0, 31, body, t0)
        t_ref[...] = t
        key = key_ref[...]
        n_gt = _count_all(key > t) + (key_new > t).astype(jnp.int32)
        need = topk - n_gt
        pos = (lax.broadcasted_iota(jnp.int32, key.shape, 0) * page
               + lax.broadcasted_iota(jnp.int32, key.shape, 1))
        n_bits = int(math.ceil(math.log2(past + 1)))

        def body2(it, x):
            cand = x | (jnp.int32(1) << (n_bits - 1 - it))
            c = _count_all((key_ref[...] == t) & (pos < cand)) + ((key_new == t) & (past < cand)).astype(jnp.int32)
            return jnp.where(c < need, cand, x)

        cut = lax.fori_loop(0, n_bits, body2, jnp.zeros((1, 1), jnp.int32))
        cut_ref[...] = cut
        sel = (key > t) | ((key == t) & (pos <= cut))
        e_t = lax.broadcasted_iota(jnp.int32, (page, cols), 0)
        e_c = lax.broadcasted_iota(jnp.int32, (page, cols), 1)
        spread = jnp.where((e_c >> kv_shift) == e_t, 1.0, 0.0).astype(BF16)
        selx_ref[...] = _dot(jnp.where(sel, 1.0, 0.0).astype(BF16), spread)

    t = t_ref[...]
    cut = cut_ref[...]
    q = qb_ref[0]

    k2 = jnp.concatenate([r[0, 0].astype(BF16) for r in kp_refs], axis=0)
    v2 = jnp.concatenate([r[0, 0].astype(BF16) for r in vp_refs], axis=0)
    sel_cols = jnp.concatenate(
        [jnp.broadcast_to(selx_ref[pl.ds(p * group + k, 1), :], (n_heads, cols)) for k in range(group)], axis=1)
    r_id = lax.broadcasted_iota(jnp.int32, (n_heads, group * cols), 0)
    c_id = lax.broadcasted_iota(jnp.int32, (n_heads, group * cols), 1)
    keep = (sel_cols > 0.5) & ((c_id & (n_kv - 1)) == (r_id >> rep_shift))
    s = _dot_nt(q.astype(BF16), k2) * scale
    s = jnp.where(keep, s, -jnp.inf)
    m_old = m_ref[...]
    m_new = jnp.maximum(m_old, jnp.max(s, axis=1, keepdims=True))
    m_safe = jnp.where(m_new == -jnp.inf, 0.0, m_new)
    alpha = jnp.exp(m_old - m_safe)
    pr = jnp.exp(s - m_safe)
    l_ref[...] = alpha * l_ref[...] + jnp.sum(pr, axis=1, keepdims=True)
    acc_ref[...] = alpha * acc_ref[...] + _dot(pr.astype(BF16), v2)
    m_ref[...] = m_new

    @pl.when(p == n_pages // group - 1)
    def _():
        key_new = keyn_ref[:, 0:1]
        sel_new = (key_new > t) | ((key_new == t) & (past <= cut))
        h_id = lax.broadcasted_iota(jnp.int32, (n_heads, dh), 0) >> rep_shift
        kn = jnp.zeros((n_heads, dh), F32)
        vn = jnp.zeros((n_heads, dh), F32)
        for g in range(n_kv):
            kn = jnp.where(h_id == g, kn_ref[0, g:g + 1, :], kn)
            vn = jnp.where(h_id == g, vn_ref[0, g:g + 1, :], vn)
        s_new = jnp.sum(q.astype(F32) * kn, axis=1, keepdims=True) * scale
        s_new = jnp.where(sel_new, s_new, -jnp.inf)
        m_o = m_ref[...]
        m_n = jnp.maximum(m_o, s_new)
        m_s = jnp.where(m_n == -jnp.inf, 0.0, m_n)
        a = jnp.exp(m_o - m_s)
        p_new = jnp.exp(s_new - m_s)
        l_fin = a * l_ref[...] + p_new
        o_ref[0] = ((a * acc_ref[...] + p_new * vn) / l_fin).astype(o_ref.dtype)


def sample_dsa(page_table, layer, cache_k, cache_v, scores, score_new, qb, k_new, v_new,
               *, n_heads, n_kv, dh, topk):
    db, n_pages = page_table.shape
    cols = cache_k.shape[2]
    page = cols // n_kv
    rep = n_heads // n_kv
    assert n_kv & (n_kv - 1) == 0 and rep & (rep - 1) == 0
    pt = page_table.reshape(-1)

    group = SAMPLE_DSA_PAGES if n_pages % SAMPLE_DSA_PAGES == 0 else 1

    def pool_map(k):
        return lambda b, p, pt_ref: (layer, pt_ref[b * n_pages + p * group + k], 0, 0)

    row3 = lambda b, p, pt_ref: (b, 0, 0)
    grid_spec = pltpu.PrefetchScalarGridSpec(
        num_scalar_prefetch=1,
        grid=(db, n_pages // group),
        in_specs=[
            pl.BlockSpec((1, n_pages, page), row3),
            pl.BlockSpec((1, 1, LANES), row3),
            pl.BlockSpec((1, n_heads, dh), row3),
            *[pl.BlockSpec((1, 1, cols, dh), pool_map(k)) for k in range(group)],
            *[pl.BlockSpec((1, 1, cols, dh), pool_map(k)) for k in range(group)],
            pl.BlockSpec((1, n_kv, dh), row3),
            pl.BlockSpec((1, n_kv, dh), row3),
        ],
        out_specs=pl.BlockSpec((1, n_heads, dh), row3),
        scratch_shapes=[pltpu.VMEM((n_pages, page), jnp.int32), pltpu.VMEM((1, LANES), jnp.int32),
                        pltpu.VMEM((1, 1), jnp.int32), pltpu.VMEM((1, 1), jnp.int32),
                        pltpu.VMEM((n_pages, cols), F32),
                        pltpu.VMEM((n_heads, 1), F32), pltpu.VMEM((n_heads, 1), F32),
                        pltpu.VMEM((n_heads, dh), F32)],
    )
    out = pl.pallas_call(
        functools.partial(_sample_dsa_kernel, n_pages=n_pages, group=group, page=page, n_heads=n_heads,
                          n_kv=n_kv, dh=dh, topk=topk),
        grid_spec=grid_spec,
        out_shape=jax.ShapeDtypeStruct((db, n_heads, dh), F32),
        compiler_params=_params("parallel", "arbitrary"),
        name="sample_dsa",
    )(pt, scores, score_new, qb.reshape(db, n_heads, dh), *([cache_k] * group), *([cache_v] * group),
      k_new.reshape(db, n_kv, dh), v_new.reshape(db, n_kv, dh))
    return out.reshape(db, n_heads * dh)


def kernel(x_prompt, x_sample, cache_a_k, cache_a_v, cache_b_k, cache_b_v, cache_idx_k, page_table, w_in, lambda_q, lambda_k, subln_g, w_branch_a, w_branch_b, w_out, norm_mix_g, norm_ffn_g, w_dense_gate, w_dense_up, w_dense_down, w_router, w_exp_gate, w_exp_up, w_exp_down, norm_final_g):
    batch, seq, d_model = x_prompt.shape
    db, dec_seq, _ = x_sample.shape
    assert dec_seq == 1, "one new token per sample row"
    depth, n_pool, page, n_ha, a_width = cache_a_k.shape
    assert page == PAGE_SIZE
    dqk = a_width // 2
    dv = cache_a_v.shape[4]
    n_kv, dh = cache_b_k.shape[3:]
    d_idx = cache_idx_k.shape[3]
    n_hb = w_branch_b.shape[1] // dh
    n_in = w_in.shape[2]
    wa, wva, wqb, wkb = n_ha * 2 * dqk, n_ha * dv, n_hb * dh, n_kv * dh
    n_idx = (n_in - (2 * wa + wva + wqb + 2 * wkb + d_idx + 2 * d_model)) // (d_idx + 1)
    assert dv == 2 * dqk == LANES and dh == LANES and 2 * d_idx == LANES
    n_pages = page_table.shape[1]
    past_len = n_pages * page
    topk_p = min(TOPK_MAX, seq // 4)
    topk_s = min(TOPK_MAX, (past_len + dec_seq) // 4)
    n_exp = w_router.shape[2]

    offs = [0]
    for wdt in (wa, wa, wva, wqb, wkb, wkb, n_idx * d_idx, d_idx, n_idx, d_model, d_model):
        offs.append(offs[-1] + wdt)
    o_qa, o_ka, o_va, o_qb, o_kb, o_vb, o_qi, o_ki, o_wi, o_ga, o_gb, _ = offs

    pos_p = jnp.arange(seq, dtype=F32)
    pos_s = jnp.full((db,), float(past_len), F32)
    tabs = {}
    for name, pos in (("p", pos_p), ("s", pos_s)):
        tabs[name, "a"] = rope_table(pos, dqk)
        tabs[name, "b"] = rope_table(pos, dh)
        tabs[name, "i"] = rope_table(pos, d_idx, valid_lanes=d_idx)

    n_groups = -(-n_in // LANES)
    kind = {}
    for start, stop, name in ((o_qa, o_va, "a"), (o_qb, o_vb, "b"), (o_qi, o_ki, "a"), (o_ki, o_ki + LANES, "i")):
        assert start % LANES == 0 and stop % LANES == 0
        for g in range(start // LANES, stop // LANES):
            kind[g] = name
    identity = jnp.stack([jnp.ones((db, LANES), F32), jnp.zeros((db, LANES), F32), jnp.zeros((db, LANES), F32)])
    tab_s = jnp.concatenate([tabs["s", kind[g]][0] if g in kind else identity for g in range(n_groups)], axis=2)
    halves_s = tuple(tabs["s", kind[g]][1] if g in kind else 0 for g in range(n_groups))

    ck_b = cache_b_k.reshape(depth, n_pool, page * n_kv, dh)
    cv_b = cache_b_v.reshape(depth, n_pool, page * n_kv, dh)
    ci_t = jnp.swapaxes(cache_idx_k, 2, 3)
    w_in_t = jnp.swapaxes(w_in, 1, 2)

    xp = x_prompt.reshape(batch * seq, d_model)
    xs = x_sample.reshape(db * dec_seq, d_model)
    rows = {"p": [[] for _ in range(5)], "s": [[] for _ in range(5)]}

    for l in range(depth):
        lam_init = 0.8 - 0.6 * math.exp(-0.3 * l)
        lq, lk = lambda_q[l], lambda_k[l]
        wl = w_in_t[l].astype(BF16)
        w_kiwi = jnp.pad(wl[o_ki:o_ga], ((0, LANES - d_idx - n_idx), (0, 0)))
        mix_w = {"p": dict(pa=w_branch_a[l].astype(BF16), pb=w_branch_b[l].astype(BF16), o=w_out[l].astype(BF16)),
                 "s": dict(pa=w_branch_a[l], pb=w_branch_b[l], o=w_out[l])}

        def in_proj_prompt(x):
            xn = rmsnorm(x, norm_mix_g[l], BF16)
            ta, ha = tabs["p", "a"]
            tb, hb = tabs["p", "b"]
            ti, hi = tabs["p", "i"]
            z = {}
            z["qa"], = project(xn, wl[o_qa:o_ka], [BF16], ta, ha)
            z["ka32"], z["ka"] = project(xn, wl[o_ka:o_va], [F32, BF16], ta, ha)
            z["va32"], z["va"] = project(xn, wl[o_va:o_qb], [F32, BF16])
            z["qb"], = project(xn, wl[o_qb:o_kb], [BF16], tb, hb)
            z["kb32"], z["kb"] = project(xn, wl[o_kb:o_vb], [F32, BF16], tb, hb)
            z["vb32"], z["vb"] = project(xn, wl[o_vb:o_qi], [F32, BF16])
            z["qi"], = project(xn, wl[o_qi:o_ki], [BF16], ta, ha)
            z["kiwi"], = project(xn, w_kiwi, [F32], ti, hi)
            z["ga"], = project(xn, wl[o_ga:o_gb], [F32])
            z["gb"], = project(xn, wl[o_gb:], [F32])
            return z

        def in_proj_sample(x):
            xn = rmsnorm(x, norm_mix_g[l], F32)
            zz = sample_project(xn, w_in_t, l, tab_s, halves_s)
            z = {name: zz[:, a:b] for name, a, b in (
                ("qa", o_qa, o_ka), ("ka32", o_ka, o_va), ("va32", o_va, o_qb), ("qb", o_qb, o_kb),
                ("kb32", o_kb, o_vb), ("vb32", o_vb, o_qi), ("qi", o_qi, o_ki), ("kiwi", o_ki, o_ki + LANES),
                ("ga", o_ga, o_gb), ("gb", o_gb, n_in))}
            return z

        def mix_out(x, z, oa, ob, grp):
            w = mix_w[grp]
            u = merge_branches(oa, ob, w["pa"], w["pb"], z["ga"], z["gb"])
            return matmul_residual(u, w["o"], x)

        zp = in_proj_prompt(xp)
        oa = diff_attention_prompt(zp["qa"], zp["ka"], zp["va"].T, lq, lk, subln_g[l], batch=batch, seq=seq,
                                   n_heads=n_ha, dqk=dqk, lam_init=lam_init)
        ob = dsa_prompt(zp["qi"], zp["kiwi"], zp["qb"], zp["kb"], zp["vb"], batch=batch, seq=seq, n_idx=n_idx,
                        d_idx=d_idx, n_heads=n_hb, n_kv=n_kv, dh=dh, topk=topk_p)
        xp = mix_out(xp, zp, oa, ob, "p")

        zs = in_proj_sample(xs)
        ki_s = zs["kiwi"][:, :d_idx]
        wi_s = zs["kiwi"][:, d_idx:d_idx + n_idx]
        oa, scores, score_new = sample_attention(
            page_table, l, cache_a_k, cache_a_v, ci_t, zs["qa"], zs["qi"], wi_s, zs["ka32"], zs["va32"], ki_s,
            lq, lk, subln_g[l], n_heads=n_ha, dqk=dqk, lam_init=lam_init, n_idx=n_idx, d_idx=d_idx)
        ob = sample_dsa(page_table, l, ck_b, cv_b, scores, score_new, zs["qb"], zs["kb32"], zs["vb32"],
                        n_heads=n_hb, n_kv=n_kv, dh=dh, topk=topk_s)
        xs = mix_out(xs, zs, oa, ob, "s")

        for grp, z, bt, tt in (("p", zp, batch, seq), ("s", zs, db, dec_seq)):
            rows[grp][0].append(z["ka32"].reshape(bt, tt, n_ha, 2 * dqk))
            rows[grp][1].append(z["va32"].reshape(bt, tt, n_ha, dv))
            rows[grp][2].append(z["kb32"].reshape(bt, tt, n_kv, dh))
            rows[grp][3].append(z["vb32"].reshape(bt, tt, n_kv, dh))
            rows[grp][4].append(z["kiwi"][:, :d_idx].reshape(bt, tt, d_idx))

        i = l // 2
        if l % 2 == 0:
            hp = rmsnorm(xp, norm_ffn_g[l], BF16)
            hs = rmsnorm(xs, norm_ffn_g[l], F32)
            xp = matmul_residual(gate_up(hp, w_dense_gate[i], w_dense_up[i]), w_dense_down[i].astype(BF16), xp)
            xs = matmul_residual(gate_up(hs, w_dense_gate[i], w_dense_up[i]), w_dense_down[i], xs)
        else:
            xp, xs = moe_ffn(xp, xs, norm_ffn_g[l], w_router[i], w_exp_gate[i], w_exp_up[i], w_exp_down[i])

    y_prompt = rmsnorm(xp, norm_final_g, F32).reshape(batch, seq, d_model)
    y_sample = rmsnorm(xs, norm_final_g, F32).reshape(db, dec_seq, d_model)
    outs_p = [jnp.stack(r, axis=0) for r in rows["p"]]
    outs_s = [jnp.stack(r, axis=0) for r in rows["s"]]
    return (y_prompt, y_sample, *outs_p, *outs_s)
```
